```python
import math
import jax
import jax.numpy as jnp
from jax import lax
import numpy as np

D_MODEL = 2048
BATCH = 2
SEQ = 4096
DEPTH = 2

GRID_W = 64
CTX_LEN = 256
HEAD_DIM = 128
ROPE_BASE = 10000.0
BLOCK = 128
WINDOW = 128
A_Q_HEADS = D_MODEL // HEAD_DIM // 2
A_KV_HEADS = A_Q_HEADS // 4
B_Q_HEADS = D_MODEL // HEAD_DIM // 2
B_KV_HEADS = B_Q_HEADS // 4
AB_SIZES = (A_Q_HEADS * HEAD_DIM, A_KV_HEADS * HEAD_DIM, A_KV_HEADS * HEAD_DIM,
            B_Q_HEADS * HEAD_DIM, B_KV_HEADS * HEAD_DIM, B_KV_HEADS * HEAD_DIM)
AB_IN = sum(AB_SIZES)
AB_OUT = (A_Q_HEADS + B_Q_HEADS) * HEAD_DIM
C_HEADS = D_MODEL // (2 * HEAD_DIM)
C_V_DIM = 2 * HEAD_DIM
C_SIZES = (C_HEADS * 2 * HEAD_DIM, C_HEADS * 2 * HEAD_DIM, C_HEADS * C_V_DIM)
C_IN = sum(C_SIZES)
C_OUT = C_HEADS * C_V_DIM
N_EXPERTS = 16
EC_FACTOR = 2
D_EXPERT = D_MODEL
EPS = 1e-6
NEG_INF = -1e30

kernel_name = 'hybrid_diffusion_prefix_block'


def rms_norm(x, g):
    xf = x.astype(jnp.float32)
    xf = xf * lax.rsqrt(jnp.mean(xf * xf, axis=-1, keepdims=True) + EPS)
    return xf.astype(x.dtype) * g


def rope_tables(n_rows, dtype):
    row = jnp.repeat(jnp.arange(n_rows), GRID_W).astype(jnp.float32)
    col = jnp.tile(jnp.arange(GRID_W), n_rows).astype(jnp.float32)
    n_freq = HEAD_DIM // 4
    inv = ROPE_BASE ** (-jnp.arange(n_freq, dtype=jnp.float32) / n_freq)
    ar = row[:, None] * inv
    ac = col[:, None] * inv
    return tuple(t.astype(dtype) for t in (jnp.cos(ar), jnp.sin(ar), jnp.cos(ac), jnp.sin(ac)))


def apply_rope(x, rope):
    cr, sr, cc, sc = (t[:, None, :] for t in rope)
    r1, r2, c1, c2 = jnp.split(x, 4, axis=-1)
    return jnp.concatenate([r1 * cr - r2 * sr, r2 * cr + r1 * sr,
                            c1 * cc - c2 * sc, c2 * cc + c1 * sc], axis=-1)


def _split(p, sizes):
    return jnp.split(p, [int(s) for s in np.cumsum(sizes)[:-1]], axis=-1)


def _to_heads(t):
    return t.reshape(t.shape[:-1] + (t.shape[-1] // HEAD_DIM, HEAD_DIM))


def _blocked(fn, q):
    B, T = q.shape[:2]
    nb = T // BLOCK
    qb = jnp.swapaxes(q.reshape((B, nb, BLOCK) + q.shape[2:]), 0, 1)
    o = lax.map(fn, qb)
    return jnp.swapaxes(o, 0, 1).reshape((B, T) + o.shape[3:])


def dense_gqa(q, k, v, sink=None):
    B, L, Hq, d = q.shape
    Hkv = k.shape[2]
    G = Hq // Hkv
    qg = q.reshape(B, L, Hkv, G, d)
    s = jnp.einsum('bqhgd,bkhd->bhgqk', qg, k).astype(jnp.float32) * (d ** -0.5)
    if sink is not None:
        s_sink = jnp.broadcast_to(sink.astype(jnp.float32).reshape(1, Hkv, G, 1, 1), s.shape[:-1] + (1,))
        p = jax.nn.softmax(jnp.concatenate([s, s_sink], axis=-1), axis=-1)[..., :-1]
    else:
        p = jax.nn.softmax(s, axis=-1)
    o = jnp.einsum('bhgqk,bkhd->bqhgd', p.astype(v.dtype), v)
    return o.reshape(B, L, Hq * d)


def blocked_gqa(q, k, v):
    return _blocked(lambda qb: dense_gqa(qb, k, v), q)


def windowed_gqa_latent(q, k, v, kc, vc, sink):
    B, T, Hq, d = q.shape
    Hkv = k.shape[2]
    G = Hq // Hkv
    nb = T // BLOCK
    L = kc.shape[1]
    qb = q.reshape(B, nb, BLOCK, Hkv, G, d)
    pad = ((0, 0), (BLOCK, BLOCK), (0, 0), (0, 0))
    kp = jnp.pad(k, pad).reshape(B, nb + 2, BLOCK, Hkv, d)
    vp = jnp.pad(v, pad).reshape(B, nb + 2, BLOCK, Hkv, d)
    kband = jnp.concatenate([kp[:, :-2], kp[:, 1:-1], kp[:, 2:]], axis=2)
    vband = jnp.concatenate([vp[:, :-2], vp[:, 1:-1], vp[:, 2:]], axis=2)
    scale = d ** -0.5
    s_band = jnp.einsum('bnqhgd,bnkhd->bnhgqk', qb, kband).astype(jnp.float32) * scale
    s_ctx = jnp.einsum('bnqhgd,bchd->bnhgqc', qb, kc).astype(jnp.float32) * scale
    q_pos = jnp.arange(nb)[:, None, None] * BLOCK + jnp.arange(BLOCK)[None, :, None]
    k_pos = (jnp.arange(nb)[:, None, None] - 1) * BLOCK + jnp.arange(3 * BLOCK)[None, None, :]
    mask = (jnp.abs(q_pos - k_pos) <= WINDOW) & (k_pos >= 0) & (k_pos < T)
    s_band = jnp.where(mask[None, :, None, None], s_band, NEG_INF)
    s_sink = jnp.broadcast_to(sink.astype(jnp.float32).reshape(1, 1, Hkv, G, 1, 1), s_band.shape[:-1] + (1,))
    p = jax.nn.softmax(jnp.concatenate([s_ctx, s_band, s_sink], axis=-1), axis=-1).astype(v.dtype)
    o = (jnp.einsum('bnhgqc,bchd->bnqhgd', p[..., :L], vc)
         + jnp.einsum('bnhgqk,bnkhd->bnqhgd', p[..., L:L + 3 * BLOCK], vband))
    return o.reshape(B, T, Hq * d)


def diff_attn(q, k, v, lam):
    d = q.shape[-1]
    s = jnp.einsum('bqhmd,bkhmd->bmhqk', q, k).astype(jnp.float32) * (d ** -0.5)
    p = jax.nn.softmax(s, axis=-1)
    a = p[:, 0] - lam * p[:, 1]
    return jnp.einsum('bhqk,bkhe->bqhe', a.astype(v.dtype), v)


def mixer_ab(hl, hc, w_in, w_out, sink, q_g, k_g, rope, need_ctx):
    qla, kla, vla, qlb, klb, vlb = [_to_heads(t) for t in _split(hl @ w_in, AB_SIZES)]
    qca, kca, vca, qcb, kcb, vcb = [_to_heads(t) for t in _split(hc @ w_in, AB_SIZES)]
    qlb, klb = rms_norm(qlb, q_g), rms_norm(klb, k_g)
    qcb, kcb = rms_norm(qcb, q_g), rms_norm(kcb, k_g)
    ya = windowed_gqa_latent(apply_rope(qla, rope), apply_rope(kla, rope), vla, kca, vca, sink)
    yb = blocked_gqa(apply_rope(qlb, rope),
                     jnp.concatenate([kcb, apply_rope(klb, rope)], axis=1),
                     jnp.concatenate([vcb, vlb], axis=1))
    yl = jnp.concatenate([ya, yb], axis=-1) @ w_out
    yc = None
    if need_ctx:
        yc = jnp.concatenate([dense_gqa(qca, kca, vca, sink), dense_gqa(qcb, kcb, vcb)], axis=-1) @ w_out
    return yl, yc


def mixer_c(hl, hc, w_in, w_out, lam_p, sub_g, lambda_init, rope, need_ctx):
    def heads(h):
        q, k, v = _split(h @ w_in, C_SIZES)
        lead = h.shape[:-1]
        return (q.reshape(lead + (C_HEADS, 2, HEAD_DIM)), k.reshape(lead + (C_HEADS, 2, HEAD_DIM)),
                v.reshape(lead + (C_HEADS, C_V_DIM)))

    def rot(t):
        B, T = t.shape[:2]
        return apply_rope(t.reshape(B, T, 2 * C_HEADS, HEAD_DIM), rope).reshape(t.shape)

    def finish(o):
        lead = o.shape[:-2]
        return (rms_norm(o, sub_g) * (1.0 - lambda_init)).reshape(lead + (C_OUT,)) @ w_out

    ql, kl, vl = heads(hl)
    qc, kc, vc = heads(hc)
    lp = lam_p.astype(jnp.float32)
    lam = jnp.exp(jnp.sum(lp[0] * lp[1])) - jnp.exp(jnp.sum(lp[2] * lp[3])) + lambda_init
    k_all = jnp.concatenate([kc, rot(kl)], axis=1)
    v_all = jnp.concatenate([vc, vl], axis=1)
    yl = finish(_blocked(lambda qb: diff_attn(qb, k_all, v_all, lam), rot(ql)))
    yc = finish(diff_attn(qc, kc, vc, lam)) if need_ctx else None
    return yl, yc


def ec_moe(h, w_r, w_g, w_u, w_d):
    B, N, D = h.shape
    cap = EC_FACTOR * N // N_EXPERTS
    aff = jax.nn.softmax(jnp.einsum('bnd,de->bne', h, w_r).astype(jnp.float32), axis=-1)
    gate, idx = lax.top_k(jnp.swapaxes(aff, 1, 2), cap)
    xg = jax.vmap(lambda hb, ib: hb[ib])(h, idx)
    a = jnp.einsum('becd,edf->becf', xg, w_g)
    u = jnp.einsum('becd,edf->becf', xg, w_u)
    y = jnp.einsum('becf,efd->becd', jax.nn.silu(a) * u, w_d) * gate[..., None].astype(h.dtype)
    return jax.vmap(lambda yb, ib: jnp.zeros((N, D), yb.dtype).at[ib].add(yb))(y, idx)


def _mods(cv, w, b):
    return jnp.split(jax.nn.silu(cv) @ w + b, 6, axis=-1)


def setup_inputs(seed: int = 0) -> dict:
    key = jax.random.key(seed)
    ks = jax.random.split(key, 20)
    D = D_MODEL
    n_even = (DEPTH + 1) // 2
    n_odd = DEPTH // 2

    def nrm(k, shape, s):
        return jax.random.normal(k, shape, jnp.float32) * s

    return {
        'x': nrm(ks[0], (BATCH, SEQ, D), 1.0),
        'c': nrm(ks[1], (BATCH, D), 1.0),
        'ctx': nrm(ks[2], (BATCH, CTX_LEN, D), 1.0),
        'c_ctx': nrm(ks[3], (D,), 1.0),
        'mod_w': nrm(ks[4], (DEPTH, D, 6 * D), 0.5 * D ** -0.5),
        'mod_b': nrm(ks[5], (DEPTH, 6 * D), 0.01),
        'norm_g': 1.0 + nrm(ks[6], (DEPTH, 4, D), 0.02),
        'ab_w_in': nrm(ks[7], (n_even, D, AB_IN), D ** -0.5),
        'ab_w_out': nrm(ks[8], (n_even, AB_OUT, D), AB_OUT ** -0.5),
        'ab_sink': nrm(ks[9], (n_even, A_Q_HEADS), 1.0),
        'ab_q_norm': 1.0 + nrm(ks[10], (n_even, HEAD_DIM), 0.02),
        'ab_k_norm': 1.0 + nrm(ks[11], (n_even, HEAD_DIM), 0.02),
        'dif_w_in': nrm(ks[12], (n_odd, D, C_IN), D ** -0.5),
        'dif_w_out': nrm(ks[13], (n_odd, C_OUT, D), C_OUT ** -0.5),
        'dif_lambda': nrm(ks[14], (n_odd, 4, HEAD_DIM), 0.1),
        'dif_subln': 1.0 + nrm(ks[15], (n_odd, C_V_DIM), 0.02),
        'router_w': nrm(ks[16], (DEPTH, D, N_EXPERTS), D ** -0.5),
        'exp_w_gate': nrm(ks[17], (DEPTH, N_EXPERTS, D, D_EXPERT), D ** -0.5),
        'exp_w_up': nrm(ks[18], (DEPTH, N_EXPERTS, D, D_EXPERT), D ** -0.5),
        'exp_w_down': nrm(ks[19], (DEPTH, N_EXPERTS, D_EXPERT, D), D_EXPERT ** -0.5),
    }


def reference(x, c, ctx, c_ctx, mod_w, mod_b, norm_g, ab_w_in, ab_w_out, ab_sink, ab_q_norm, ab_k_norm,
              dif_w_in, dif_w_out, dif_lambda, dif_subln, router_w, exp_w_gate, exp_w_up, exp_w_down):
    n_rows = x.shape[1] // GRID_W
    rope = rope_tables(n_rows, x.dtype)
    xl, xc = x, ctx
    for i in range(DEPTH):
        need_ctx = i < DEPTH - 1
        sa_l, ca_l, ga_l, sf_l, cf_l, gf_l = [m[:, None, :] for m in _mods(c, mod_w[i], mod_b[i])]
        sa_c, ca_c, ga_c, sf_c, cf_c, gf_c = _mods(c_ctx, mod_w[i], mod_b[i])
        g_pre_a, g_post_a, g_pre_f, g_post_f = norm_g[i]
        hl = rms_norm(xl, g_pre_a) * (1.0 + ca_l) + sa_l
        hc = rms_norm(xc, g_pre_a) * (1.0 + ca_c) + sa_c
        j = i // 2
        if i % 2 == 0:
            yl, yc = mixer_ab(hl, hc, ab_w_in[j], ab_w_out[j], ab_sink[j], ab_q_norm[j], ab_k_norm[j],
                              rope, need_ctx)
        else:
            lambda_init = 0.8 - 0.6 * math.exp(-0.3 * i)
            yl, yc = mixer_c(hl, hc, dif_w_in[j], dif_w_out[j], dif_lambda[j], dif_subln[j], lambda_init,
                             rope, need_ctx)
        xl = xl + ga_l * rms_norm(yl, g_post_a)
        hl = rms_norm(xl, g_pre_f) * (1.0 + cf_l) + sf_l
        xl = xl + gf_l * rms_norm(ec_moe(hl, router_w[i], exp_w_gate[i], exp_w_up[i], exp_w_down[i]), g_post_f)
        if need_ctx:
            xc = xc + ga_c * rms_norm(yc, g_post_a)
            hc = rms_norm(xc, g_pre_f) * (1.0 + cf_c) + sf_c
            xc = xc + gf_c * rms_norm(ec_moe(hc, router_w[i], exp_w_gate[i], exp_w_up[i], exp_w_down[i]), g_post_f)
    return xl
```

```python
import functools
import math

import jax
import jax.numpy as jnp
from jax import lax
from jax.experimental import pallas as pl
from jax.experimental.pallas import tpu as pltpu

D_MODEL = 2048
SEQ = 4096
DEPTH = 2
GRID_W = 64
CTX_LEN = 256
HEAD_DIM = 128
ROPE_BASE = 10000.0
BLOCK = 128
WINDOW = 128
A_Q_HEADS = 8
A_KV_HEADS = 2
B_Q_HEADS = 8
B_KV_HEADS = 2
AB_IN = 3072
C_HEADS = 8
C_V_DIM = 256
C_IN = 6144
N_EXPERTS = 16
EC_FACTOR = 2
EPS = 1e-6
NEG_INF = -1e30

S_ROWS = CTX_LEN + SEQ
ROW_TILE = 256
LANES = 128
MIB = 1024 * 1024

F32 = jnp.float32
BF16 = jnp.bfloat16


def _params(sem, vmem_mib):
    return pltpu.CompilerParams(dimension_semantics=sem, vmem_limit_bytes=vmem_mib * MIB)


def _rms(x):
    return x * lax.rsqrt(jnp.mean(x * x, axis=-1, keepdims=True) + EPS)


def _dot(a, b):
    return jnp.dot(a, b, preferred_element_type=F32)


def _dot_nt(a, b):
    return lax.dot_general(a, b, (((1,), (1,)), ((), ())), preferred_element_type=F32)


def _mods_kernel(cv_ref, w_ref, b_ref, o_ref):
    cv = cv_ref[...]
    s = cv * jax.nn.sigmoid(cv)
    o_ref[...] = _dot(s.astype(BF16), w_ref[...].astype(BF16)) + b_ref[...]


def _mods(cv, mod_w, mod_b):
    tn = 1024
    n = mod_w.shape[-1]
    return pl.pallas_call(
        _mods_kernel,
        grid=(DEPTH, n // tn),
        in_specs=[
            pl.BlockSpec((8, D_MODEL), lambda l, j: (0, 0)),
            pl.BlockSpec((None, D_MODEL, tn), lambda l, j: (l, 0, j)),
            pl.BlockSpec((None, 1, tn), lambda l, j: (l, 0, j)),
        ],
        out_specs=pl.BlockSpec((None, 8, tn), lambda l, j: (l, 0, j)),
        out_shape=jax.ShapeDtypeStruct((DEPTH, 8, n), F32),
        compiler_params=_params(("parallel", "parallel"), 40),
        name="mods",
    )(cv, mod_w, mod_b.reshape(DEPTH, 1, n))


def _cast_kernel(x_ref, o_ref):
    o_ref[...] = x_ref[...].astype(o_ref.dtype)


def _cast_bf16(w):
    k, n = w.shape
    tk = 512
    return pl.pallas_call(
        _cast_kernel,
        grid=(k // tk,),
        in_specs=[pl.BlockSpec((tk, n), lambda i: (i, 0))],
        out_specs=pl.BlockSpec((tk, n), lambda i: (i, 0)),
        out_shape=jax.ShapeDtypeStruct((k, n), BF16),
        compiler_params=_params(("parallel",), 48),
        name="cast_bf16",
    )(w)


def _seg(i):
    return jnp.minimum(i, 1)


def _norm_mod_kernel(x_ref, g_ref, mod_ref, h_ref):
    m = mod_ref[...]
    h = _rms(x_ref[...]) * g_ref[0:1, :] * (1.0 + m[1:2, :]) + m[0:1, :]
    h_ref[...] = h.astype(h_ref.dtype)


def _norm_mod(x, g4, mod_tab):
    b, s, d = x.shape
    return pl.pallas_call(
        _norm_mod_kernel,
        grid=(b, s // ROW_TILE),
        in_specs=[
            pl.BlockSpec((None, ROW_TILE, d), lambda bi, i: (bi, i, 0)),
            pl.BlockSpec((4, d), lambda bi, i: (0, 0)),
            pl.BlockSpec((None, None, 6, d), lambda bi, i: (bi, _seg(i), 0, 0)),
        ],
        out_specs=pl.BlockSpec((None, ROW_TILE, d), lambda bi, i: (bi, i, 0)),
        out_shape=jax.ShapeDtypeStruct((b, s, d), BF16),
        compiler_params=_params(("parallel", "parallel"), 32),
        name="norm_mod",
    )(x, g4, mod_tab)


def _mm_kernel(a_ref, w_ref, o_ref):
    o_ref[...] = _dot(a_ref[...], w_ref[...]).astype(o_ref.dtype)


def _matmul(a, w, out_dtype):
    m, k = a.shape
    n = w.shape[1]
    tm, tn = 1088, 1024
    return pl.pallas_call(
        _mm_kernel,
        grid=(n // tn, m // tm),
        in_specs=[
            pl.BlockSpec((tm, k), lambda j, i: (i, 0)),
            pl.BlockSpec((k, tn), lambda j, i: (0, j)),
        ],
        out_specs=pl.BlockSpec((tm, tn), lambda j, i: (i, j)),
        out_shape=jax.ShapeDtypeStruct((m, n), out_dtype),
        compiler_params=_params(("parallel", "parallel"), 48),
        name="in_proj",
    )(a, w)


def _rope_tables():
    n_rows = SEQ // GRID_W
    row = jnp.repeat(jnp.arange(n_rows), GRID_W).astype(F32)
    col = jnp.tile(jnp.arange(GRID_W), n_rows).astype(F32)
    n_freq = HEAD_DIM // 4
    inv = ROPE_BASE ** (-jnp.arange(n_freq, dtype=F32) / n_freq)
    ar = row[:, None] * inv
    ac = col[:, None] * inv
    cr, sr, cc, sc = jnp.cos(ar), jnp.sin(ar), jnp.cos(ac), jnp.sin(ac)
    z = jnp.zeros_like(sr)
    c = jnp.concatenate([cr, cr, cc, cc], axis=-1)
    sa = jnp.concatenate([-sr, z, -sc, z], axis=-1)
    sb = jnp.concatenate([z, sr, z, sc], axis=-1)
    pad = jnp.zeros((CTX_LEN, HEAD_DIM), F32)
    return (jnp.concatenate([pad + 1.0, c], axis=0), jnp.concatenate([pad, sa], axis=0),
            jnp.concatenate([pad, sb], axis=0))


def _prep_kernel(p_ref, c_ref, sa_ref, sb_ref, gq_ref, gk_ref, o_ref, *, plan):
    c, sa, sb = c_ref[...], sa_ref[...], sb_ref[...]
    for s, (norm, scale) in enumerate(plan):
        sl = slice(s * LANES, (s + 1) * LANES)
        if scale is None:
            o_ref[:, sl] = p_ref[:, sl]
            continue
        x = p_ref[:, sl].astype(F32)
        if norm is not None:
            x = _rms(x) * (gq_ref[...] if norm == "q" else gk_ref[...])
        r = x * c + pltpu.roll(x, 96, 1) * sa + pltpu.roll(x, 32, 1) * sb
        if scale != 1.0:
            r = r * scale
        o_ref[:, sl] = r.astype(o_ref.dtype)


def _prep(p, tables, gq, gk, plan):
    b, s, n = p.shape
    tab_spec = pl.BlockSpec((ROW_TILE, LANES), lambda bi, i: (i, 0))
    g_spec = pl.BlockSpec((1, LANES), lambda bi, i: (0, 0))
    return pl.pallas_call(
        functools.partial(_prep_kernel, plan=plan),
        grid=(b, s // ROW_TILE),
        in_specs=[pl.BlockSpec((None, ROW_TILE, n), lambda bi, i: (bi, i, 0)),
                  tab_spec, tab_spec, tab_spec, g_spec, g_spec],
        out_specs=pl.BlockSpec((None, ROW_TILE, n), lambda bi, i: (bi, i, 0)),
        out_shape=jax.ShapeDtypeStruct((b, s, n), BF16),
        compiler_params=_params(("parallel", "parallel"), 32),
        name="qk_prep",
    )(p, *tables, gq, gk)


def _attn_oneshot_kernel(sink_ref, q_ref, k_ref, v_ref, o_ref, *, groups, tq, band, use_sink):
    h = pl.program_id(1)
    qi = pl.program_id(2)
    kc = k_ref[0:CTX_LEN, :]
    vc = v_ref[0:CTX_LEN, :]
    if band:
        nband = tq + 2 * BLOCK
        start = jnp.minimum(CTX_LEN + qi * tq - BLOCK, S_ROWS - nband)
        start = pl.multiple_of(start, BLOCK)
        kb = k_ref[pl.ds(start, nband), :]
        vb = v_ref[pl.ds(start, nband), :]
        q_pos = qi * tq + lax.broadcasted_iota(jnp.int32, (tq, nband), 0)
        k_pos = start - CTX_LEN + lax.broadcasted_iota(jnp.int32, (tq, nband), 1)
        valid = (jnp.abs(q_pos - k_pos) <= WINDOW) & (k_pos >= 0)
    for g in range(groups):
        sl = slice(g * LANES, (g + 1) * LANES)
        q = q_ref[:, sl]
        s_c = _dot_nt(q, kc)
        m = jnp.max(s_c, axis=-1, keepdims=True)
        if band:
            s_b = jnp.where(valid, _dot_nt(q, kb), NEG_INF)
            m = jnp.maximum(m, jnp.max(s_b, axis=-1, keepdims=True))
        if use_sink:
            sink = sink_ref[h * groups + g]
            m = jnp.maximum(m, sink)
        p_c = jnp.exp(s_c - m)
        l = jnp.sum(p_c, axis=-1, keepdims=True)
        o = _dot(p_c.astype(BF16), vc)
        if band:
            p_b = jnp.exp(s_b - m)
            l = l + jnp.sum(p_b, axis=-1, keepdims=True)
            o = o + _dot(p_b.astype(BF16), vb)
        if use_sink:
            l = l + jnp.exp(sink - m)
        o_ref[:, sl] = (o / l).astype(o_ref.dtype)


def _attn_oneshot(sink, qkv, *, q_col, k_col, v_col, groups, kv_heads, q_row0, n_q, tq, band, use_sink):
    b = qkv.shape[0]
    k_rows = S_ROWS if band else CTX_LEN
    gw = groups * LANES
    return pl.pallas_call(
        functools.partial(_attn_oneshot_kernel, groups=groups, tq=tq, band=band, use_sink=use_sink),
        grid=(b, kv_heads, n_q),
        in_specs=[
            pl.BlockSpec(memory_space=pltpu.SMEM),
            pl.BlockSpec((None, tq, gw), lambda bi, h, i: (bi, i + q_row0, q_col // groups + h)),
            pl.BlockSpec((None, k_rows, LANES), lambda bi, h, i: (bi, 0, k_col + h)),
            pl.BlockSpec((None, k_rows, LANES), lambda bi, h, i: (bi, 0, v_col + h)),
        ],
        out_specs=pl.BlockSpec((None, tq, gw), lambda bi, h, i: (bi, i, h)),
        out_shape=jax.ShapeDtypeStruct((b, n_q * tq, kv_heads * gw), BF16),
        compiler_params=_params(("parallel", "parallel", "parallel"), 32),
        name="attn_oneshot",
    )(sink, qkv, qkv, qkv)


def _attn_flash_kernel(lam_ref, subg_ref, q_ref, k_ref, v_ref, o_ref, m_sc, l_sc, acc_sc,
                       *, streams, shared_k, tk, n_kv, diff_scale):
    m_sc[...] = jnp.full(m_sc.shape, NEG_INF, F32)
    l_sc[...] = jnp.zeros(l_sc.shape, F32)
    acc_sc[...] = jnp.zeros(acc_sc.shape, F32)

    def body(c, carry):
        off = pl.multiple_of(c * tk, tk)
        kc = k_ref[pl.ds(off, tk), :]
        vc = v_ref[pl.ds(off, tk), :]
        for g in range(streams):
            q = q_ref[:, g * LANES:(g + 1) * LANES]
            kg = kc if shared_k else kc[:, g * LANES:(g + 1) * LANES]
            s = _dot_nt(q, kg)
            m_prev = m_sc[g]
            m_new = jnp.maximum(m_prev, jnp.max(s, axis=-1, keepdims=True))
            alpha = jnp.exp(m_prev - m_new)
            p = jnp.exp(s - m_new)
            l_sc[g] = alpha * l_sc[g] + jnp.sum(p, axis=-1, keepdims=True)
            acc_sc[g] = alpha * acc_sc[g] + _dot(p.astype(BF16), vc)
            m_sc[g] = m_new
        return carry

    lax.fori_loop(0, n_kv, body, 0)
    if diff_scale is None:
        for g in range(streams):
            o_ref[:, g * LANES:(g + 1) * LANES] = (acc_sc[g] / l_sc[g]).astype(o_ref.dtype)
    else:
        lambda_init, out_scale = diff_scale
        lp = lam_ref[...]
        lam = (jnp.exp(jnp.sum(lp[0:1, :] * lp[1:2, :], axis=-1, keepdims=True))
               - jnp.exp(jnp.sum(lp[2:3, :] * lp[3:4, :], axis=-1, keepdims=True)) + lambda_init)
        o = acc_sc[0] / l_sc[0] - lam * (acc_sc[1] / l_sc[1])
        o_ref[...] = (_rms(o) * subg_ref[...] * out_scale).astype(o_ref.dtype)


def _attn_flash(lam, subg, qkv, *, q_col, k_col, v_col, streams, shared_k, dv, heads, tq, tk, diff_scale):
    b = qkv.shape[0]
    qw = streams * LANES
    kw = LANES if shared_k else streams * LANES
    n_q = SEQ // tq
    q_row0 = CTX_LEN // tq
    ow = dv if diff_scale is not None else qw
    return pl.pallas_call(
        functools.partial(_attn_flash_kernel, streams=streams, shared_k=shared_k, tk=tk,
                          n_kv=S_ROWS // tk, diff_scale=diff_scale),
        grid=(b, heads, n_q),
        in_specs=[
            pl.BlockSpec((4, LANES), lambda bi, h, i: (0, 0)),
            pl.BlockSpec((1, dv), lambda bi, h, i: (0, 0)),
            pl.BlockSpec((None, tq, qw), lambda bi, h, i: (bi, i + q_row0, q_col + h)),
            pl.BlockSpec((None, S_ROWS, kw), lambda bi, h, i: (bi, 0, k_col + h)),
            pl.BlockSpec((None, S_ROWS, dv), lambda bi, h, i: (bi, 0, v_col + h)),
        ],
        out_specs=pl.BlockSpec((None, tq, ow), lambda bi, h, i: (bi, i, h)),
        out_shape=jax.ShapeDtypeStruct((b, SEQ, heads * ow), BF16),
        scratch_shapes=[pltpu.VMEM((streams, tq, 1), F32), pltpu.VMEM((streams, tq, 1), F32),
                        pltpu.VMEM((streams, tq, dv), F32)],
        compiler_params=_params(("parallel", "parallel", "parallel"), 40),
        name="attn_flash",
    )(lam, subg, qkv, qkv, qkv)


def _outproj_kernel(y_ref, w_ref, x_ref, mod_ref, g_ref, wr_ref, xo_ref, hf_ref, aff_ref):
    m = mod_ref[...]
    g = g_ref[...]
    y = _dot(y_ref[...], w_ref[...])
    x1 = x_ref[...] + m[2:3, :] * (_rms(y) * g[1:2, :])
    xo_ref[...] = x1
    hf = _rms(x1) * g[2:3, :] * (1.0 + m[4:5, :]) + m[3:4, :]
    hf_ref[...] = hf
    logits = _dot(hf.astype(BF16), wr_ref[...])
    lane = lax.broadcasted_iota(jnp.int32, logits.shape, 1)
    logits = jnp.where(lane < N_EXPERTS, logits, NEG_INF)
    e = jnp.exp(logits - jnp.max(logits, axis=-1, keepdims=True))
    aff_ref[...] = e / jnp.sum(e, axis=-1, keepdims=True)


def _outproj(y, w_bf, x, mod_tab, g4, wr_bf, *, x_row0, latent_only):
    b, sy, d = y.shape
    seg = (lambda i: 1) if latent_only else _seg
    row = lambda bi, i: (bi, i, 0)
    out_f32 = jax.ShapeDtypeStruct((b, sy, d), F32)
    return pl.pallas_call(
        _outproj_kernel,
        grid=(b, sy // ROW_TILE),
        in_specs=[
            pl.BlockSpec((None, ROW_TILE, d), row),
            pl.BlockSpec((d, d), lambda bi, i: (0, 0)),
            pl.BlockSpec((None, ROW_TILE, d), lambda bi, i: (bi, i + x_row0, 0)),
            pl.BlockSpec((None, None, 6, d), lambda bi, i: (bi, seg(i), 0, 0)),
            pl.BlockSpec((4, d), lambda bi, i: (0, 0)),
            pl.BlockSpec((d, LANES), lambda bi, i: (0, 0)),
        ],
        out_specs=[pl.BlockSpec((None, ROW_TILE, d), row), pl.BlockSpec((None, ROW_TILE, d), row),
                   pl.BlockSpec((None, ROW_TILE, LANES), row)],
        out_shape=[out_f32, out_f32, jax.ShapeDtypeStruct((b, sy, LANES), F32)],
        compiler_params=_params(("parallel", "parallel"), 48),
        name="out_proj",
    )(y, w_bf, x, mod_tab, g4, wr_bf)


def _moe_kernel(xg_ref, wg_ref, wu_ref, wd_ref, gate_ref, o_ref):
    f = pl.program_id(1)
    x = xg_ref[...]
    a = _dot(x, wg_ref[...].astype(BF16))
    u = _dot(x, wu_ref[...].astype(BF16))
    hmid = (a * jax.nn.sigmoid(a) * u).astype(BF16)
    y = _dot(hmid, wd_ref[...].astype(BF16))

    @pl.when(f == 0)
    def _():
        o_ref[...] = y

    @pl.when(f > 0)
    def _():
        o_ref[...] += y

    @pl.when(f == pl.num_programs(1) - 1)
    def _():
        o_ref[...] = o_ref[...] * gate_ref[...]


def _moe(xg, w_g, w_u, w_d, gates):
    e, m, d = xg.shape
    f_dim = w_g.shape[-1]
    tf = 256
    return pl.pallas_call(
        _moe_kernel,
        grid=(e, f_dim // tf),
        in_specs=[
            pl.BlockSpec((None, m, d), lambda ei, f: (ei, 0, 0)),
            pl.BlockSpec((None, d, tf), lambda ei, f: (ei, 0, f)),
            pl.BlockSpec((None, d, tf), lambda ei, f: (ei, 0, f)),
            pl.BlockSpec((None, tf, d), lambda ei, f: (ei, f, 0)),
            pl.BlockSpec((None, m, 1), lambda ei, f: (ei, 0, 0)),
        ],
        out_specs=pl.BlockSpec((None, m, d), lambda ei, f: (ei, 0, 0)),
        out_shape=jax.ShapeDtypeStruct((e, m, d), F32),
        compiler_params=_params(("parallel", "arbitrary"), 56),
        name="moe_experts",
    )(xg, w_g, w_u, w_d, gates)


def _post_moe_kernel(x_ref, moe_ref, mod_ref, g_ref, modn_ref, gn_ref, xo_ref, h_ref):
    m = mod_ref[...]
    x2 = x_ref[...] + m[5:6, :] * (_rms(moe_ref[...]) * g_ref[3:4, :])
    xo_ref[...] = x2
    if h_ref is not None:
        mn = modn_ref[...]
        h_ref[...] = (_rms(x2) * gn_ref[0:1, :] * (1.0 + mn[1:2, :]) + mn[0:1, :]).astype(h_ref.dtype)


def _post_moe_last_kernel(x_ref, moe_ref, mod_ref, g_ref, xo_ref):
    _post_moe_kernel(x_ref, moe_ref, mod_ref, g_ref, None, None, xo_ref, None)


def _post_moe(x, moe, mod_tab, g4, mod_next, g4_next, *, latent_only):
    b, s, d = x.shape
    seg = (lambda i: 1) if latent_only else _seg
    row = pl.BlockSpec((None, ROW_TILE, d), lambda bi, i: (bi, i, 0))
    mod_spec = pl.BlockSpec((None, None, 6, d), lambda bi, i: (bi, seg(i), 0, 0))
    g_spec = pl.BlockSpec((4, d), lambda bi, i: (0, 0))
    has_next = mod_next is not None
    if has_next:
        kern, ins, in_specs = _post_moe_kernel, (x, moe, mod_tab, g4, mod_next, g4_next), \
            [row, row, mod_spec, g_spec, mod_spec, g_spec]
        out_specs = [row, row]
        out_shape = [jax.ShapeDtypeStruct((b, s, d), F32), jax.ShapeDtypeStruct((b, s, d), BF16)]
    else:
        kern, ins, in_specs = _post_moe_last_kernel, (x, moe, mod_tab, g4), [row, row, mod_spec, g_spec]
        out_specs = row
        out_shape = jax.ShapeDtypeStruct((b, s, d), F32)
    return pl.pallas_call(
        kern, grid=(b, s // ROW_TILE), in_specs=in_specs, out_specs=out_specs, out_shape=out_shape,
        compiler_params=_params(("parallel", "parallel"), 40), name="post_moe",
    )(*ins)


def _route_and_run_experts(hf, aff, segments, w_g, w_u, w_d):
    b, sx, d = hf.shape
    idx_all, gate_all = [], []
    for row0, n in segments:
        cap = EC_FACTOR * n // N_EXPERTS
        a = jnp.swapaxes(aff[:, row0:row0 + n, :N_EXPERTS], 1, 2)
        gate, idx = lax.top_k(a, cap)
        idx = idx + row0 + (jnp.arange(b, dtype=idx.dtype) * sx)[:, None, None]
        idx_all.append(jnp.swapaxes(idx, 0, 1).reshape(N_EXPERTS, b * cap))
        gate_all.append(jnp.swapaxes(gate, 0, 1).reshape(N_EXPERTS, b * cap))
    idx = jnp.concatenate(idx_all, axis=1)
    gates = jnp.concatenate(gate_all, axis=1)
    hflat = hf.reshape(b * sx, d)
    xg = jnp.take(hflat, idx, axis=0).astype(BF16)
    y = _moe(xg, w_g, w_u, w_d, gates[..., None])
    out = jnp.zeros((b * sx, d), F32).at[idx.reshape(-1)].add(y.reshape(-1, d))
    return out.reshape(b, sx, d)


_SCALE = HEAD_DIM ** -0.5
_PLAN_AB = tuple([(None, _SCALE)] * 8 + [(None, 1.0)] * 2 + [(None, None)] * 2
                 + [("q", _SCALE)] * 8 + [("k", 1.0)] * 2 + [(None, None)] * 2)
_PLAN_C = tuple([(None, _SCALE)] * 16 + [(None, 1.0)] * 16 + [(None, None)] * 16)


def kernel(x, c, ctx, c_ctx, mod_w, mod_b, norm_g, ab_w_in, ab_w_out, ab_sink, ab_q_norm, ab_k_norm,
           dif_w_in, dif_w_out, dif_lambda, dif_subln, router_w, exp_w_gate, exp_w_up, exp_w_down):
    b = x.shape[0]
    d = D_MODEL
    cv = jnp.concatenate([c, c_ctx[None, :], jnp.zeros((8 - b - 1, d), F32)], axis=0)
    mods = _mods(cv, mod_w, mod_b).reshape(DEPTH, 8, 6, d)
    mod_tab = jnp.stack([jnp.broadcast_to(mods[:, b][:, None], (DEPTH, b, 6, d)), mods[:, :b]], axis=2)
    tables = _rope_tables()
    ones = jnp.ones((1, LANES), F32)
    wr = jnp.pad(router_w, ((0, 0), (0, 0), (0, LANES - N_EXPERTS))).astype(BF16)

    xs = jnp.concatenate([ctx, x], axis=1)

    h0 = _norm_mod(xs, norm_g[0], mod_tab[0])
    p0 = _matmul(h0.reshape(b * S_ROWS, d), _cast_bf16(ab_w_in[0]), BF16).reshape(b, S_ROWS, AB_IN)
    qkv0 = _prep(p0, tables, ab_q_norm[0][None, :], ab_k_norm[0][None, :], _PLAN_AB)
    sink = ab_sink[0]
    common_a = dict(q_col=0, k_col=8, v_col=10, groups=4, kv_heads=A_KV_HEADS)
    common_b = dict(q_col=12, k_col=20, v_col=22, groups=4, kv_heads=B_KV_HEADS)
    ya_l = _attn_oneshot(sink, qkv0, **common_a, q_row0=CTX_LEN // BLOCK, n_q=SEQ // BLOCK, tq=BLOCK,
                         band=True, use_sink=True)
    ya_c = _attn_oneshot(sink, qkv0, **common_a, q_row0=0, n_q=1, tq=CTX_LEN, band=False, use_sink=True)
    yb_c = _attn_oneshot(sink, qkv0, **common_b, q_row0=0, n_q=1, tq=CTX_LEN, band=False, use_sink=False)
    yb_l = _attn_flash(jnp.zeros((4, LANES), F32), ones, qkv0, q_col=3, k_col=20, v_col=22, streams=4,
                       shared_k=True, dv=HEAD_DIM, heads=B_KV_HEADS, tq=256, tk=256, diff_scale=None)
    y0 = jnp.concatenate([jnp.concatenate([ya_c, yb_c], axis=-1), jnp.concatenate([ya_l, yb_l], axis=-1)], axis=1)
    x1, hf0, aff0 = _outproj(y0, _cast_bf16(ab_w_out[0]), xs, mod_tab[0], norm_g[0], wr[0],
                             x_row0=0, latent_only=False)
    moe0 = _route_and_run_experts(hf0, aff0, [(CTX_LEN, SEQ), (0, CTX_LEN)],
                                  exp_w_gate[0], exp_w_up[0], exp_w_down[0])
    x2, h1 = _post_moe(x1, moe0, mod_tab[0], norm_g[0], mod_tab[1], norm_g[1], latent_only=False)

    p1 = _matmul(h1.reshape(b * S_ROWS, d), _cast_bf16(dif_w_in[0]), BF16).reshape(b, S_ROWS, C_IN)
    qkv1 = _prep(p1, tables, ones, ones, _PLAN_C)
    lambda_init = 0.8 - 0.6 * math.exp(-0.3 * 1)
    y1 = _attn_flash(dif_lambda[0], dif_subln[0][None, :], qkv1, q_col=0, k_col=8, v_col=16, streams=2,
                     shared_k=False, dv=C_V_DIM, heads=C_HEADS, tq=256, tk=256,
                     diff_scale=(lambda_init, 1.0 - lambda_init))
    x3, hf1, aff1 = _outproj(y1, _cast_bf16(dif_w_out[0]), x2, mod_tab[1], norm_g[1], wr[1],
                             x_row0=CTX_LEN // ROW_TILE, latent_only=True)
    moe1 = _route_and_run_experts(hf1, aff1, [(0, SEQ)], exp_w_gate[1], exp_w_up[1], exp_w_down[1])
    return _post_moe(x3, moe1, mod_tab[1], norm_g[1], None, None, latent_only=True)
```

```python
import functools
import math

import jax
import jax.numpy as jnp
from jax import lax
from jax.experimental import pallas as pl
from jax.experimental.pallas import tpu as pltpu

D_MODEL = 2048
SEQ = 4096
DEPTH = 2
GRID_W = 64
CTX_LEN = 256
HEAD_DIM = 128
ROPE_BASE = 10000.0
BLOCK = 128
WINDOW = 128
A_Q_HEADS = 8
A_KV_HEADS = 2
B_Q_HEADS = 8
B_KV_HEADS = 2
AB_IN = 3072
C_HEADS = 8
C_V_DIM = 256
C_IN = 6144
N_EXPERTS = 16
EC_FACTOR = 2
EPS = 1e-6
NEG_INF = -1e30

S_ROWS = SEQ + CTX_LEN
ROW_TILE = 256
LANES = 128
MIB = 1024 * 1024
LOG2E = math.log2(math.e)

F32 = jnp.float32
BF16 = jnp.bfloat16


def _params(sem, vmem_mib):
    return pltpu.CompilerParams(dimension_semantics=sem, vmem_limit_bytes=vmem_mib * MIB)


def _rms(x):
    return x * lax.rsqrt(jnp.mean(x * x, axis=-1, keepdims=True) + EPS)


def _dot(a, b):
    return jnp.dot(a, b, preferred_element_type=F32)


def _dot_nt(a, b):
    return lax.dot_general(a, b, (((1,), (1,)), ((), ())), preferred_element_type=F32)


def _mods_kernel(cv_ref, w_ref, b_ref, o_ref):
    cv = cv_ref[...]
    s = cv * jax.nn.sigmoid(cv)
    o_ref[...] = _dot(s.astype(BF16), w_ref[...].astype(BF16)) + b_ref[...]


def _mods(cv, mod_w, mod_b):
    tn = 1024
    n = mod_w.shape[-1]
    return pl.pallas_call(
        _mods_kernel,
        grid=(DEPTH, n // tn),
        in_specs=[
            pl.BlockSpec((8, D_MODEL), lambda l, j: (0, 0)),
            pl.BlockSpec((None, D_MODEL, tn), lambda l, j: (l, 0, j)),
            pl.BlockSpec((None, 1, tn), lambda l, j: (l, 0, j)),
        ],
        out_specs=pl.BlockSpec((None, 8, tn), lambda l, j: (l, 0, j)),
        out_shape=jax.ShapeDtypeStruct((DEPTH, 8, n), F32),
        compiler_params=_params(("parallel", "parallel"), 40),
        name="mods",
    )(cv, mod_w, mod_b.reshape(DEPTH, 1, n))


def _cast_kernel(x_ref, o_ref):
    o_ref[...] = x_ref[...].astype(o_ref.dtype)


def _cast_bf16(w):
    k, n = w.shape
    tk = 512
    return pl.pallas_call(
        _cast_kernel,
        grid=(k // tk,),
        in_specs=[pl.BlockSpec((tk, n), lambda i: (i, 0))],
        out_specs=pl.BlockSpec((tk, n), lambda i: (i, 0)),
        out_shape=jax.ShapeDtypeStruct((k, n), BF16),
        compiler_params=_params(("parallel",), 48),
        name="cast_bf16",
    )(w)


def _seg(i):
    return jnp.where(i < SEQ // ROW_TILE, 1, 0)


def _norm_mod_kernel(x_ref, g_ref, mod_ref, h_ref):
    m = mod_ref[...]
    h = _rms(x_ref[...]) * g_ref[0:1, :] * (1.0 + m[1:2, :]) + m[0:1, :]
    h_ref[...] = h.astype(h_ref.dtype)


def _norm_mod(x, g4, mod_tab):
    b, s, d = x.shape
    return pl.pallas_call(
        _norm_mod_kernel,
        grid=(b, s // ROW_TILE),
        in_specs=[
            pl.BlockSpec((None, ROW_TILE, d), lambda bi, i: (bi, i, 0)),
            pl.BlockSpec((4, d), lambda bi, i: (0, 0)),
            pl.BlockSpec((None, None, 6, d), lambda bi, i: (bi, _seg(i), 0, 0)),
        ],
        out_specs=pl.BlockSpec((None, ROW_TILE, d), lambda bi, i: (bi, i, 0)),
        out_shape=jax.ShapeDtypeStruct((b, s, d), BF16),
        compiler_params=_params(("parallel", "parallel"), 32),
        name="norm_mod",
    )(x, g4, mod_tab)


def _mm_kernel(a_ref, w_ref, o_ref):
    o_ref[...] = _dot(a_ref[...], w_ref[...]).astype(o_ref.dtype)


def _matmul(a, w, out_dtype):
    m, k = a.shape
    n = w.shape[1]
    tm, tn = 1088, 1024
    return pl.pallas_call(
        _mm_kernel,
        grid=(n // tn, m // tm),
        in_specs=[
            pl.BlockSpec((tm, k), lambda j, i: (i, 0)),
            pl.BlockSpec((k, tn), lambda j, i: (0, j)),
        ],
        out_specs=pl.BlockSpec((tm, tn), lambda j, i: (i, j)),
        out_shape=jax.ShapeDtypeStruct((m, n), out_dtype),
        compiler_params=_params(("parallel", "parallel"), 48),
        name="in_proj",
    )(a, w)


def _rope_tables():
    n_rows = SEQ // GRID_W
    row = jnp.repeat(jnp.arange(n_rows), GRID_W).astype(F32)
    col = jnp.tile(jnp.arange(GRID_W), n_rows).astype(F32)
    n_freq = HEAD_DIM // 4
    inv = ROPE_BASE ** (-jnp.arange(n_freq, dtype=F32) / n_freq)
    ar = row[:, None] * inv
    ac = col[:, None] * inv
    cr, sr, cc, sc = jnp.cos(ar), jnp.sin(ar), jnp.cos(ac), jnp.sin(ac)
    z = jnp.zeros_like(sr)
    c = jnp.concatenate([cr, cr, cc, cc], axis=-1)
    sa = jnp.concatenate([-sr, z, -sc, z], axis=-1)
    sb = jnp.concatenate([z, sr, z, sc], axis=-1)
    pad = jnp.zeros((CTX_LEN, HEAD_DIM), F32)
    return (jnp.concatenate([c, pad + 1.0], axis=0), jnp.concatenate([sa, pad], axis=0),
            jnp.concatenate([sb, pad], axis=0))


def _prep_kernel(p_ref, c_ref, sa_ref, sb_ref, gq_ref, gk_ref, o_ref, *, plan):
    c, sa, sb = c_ref[...], sa_ref[...], sb_ref[...]
    for s, (norm, scale) in enumerate(plan):
        sl = slice(s * LANES, (s + 1) * LANES)
        if scale is None:
            o_ref[:, sl] = p_ref[:, sl]
            continue
        x = p_ref[:, sl].astype(F32)
        if norm is not None:
            x = _rms(x) * (gq_ref[...] if norm == "q" else gk_ref[...])
        r = x * c + pltpu.roll(x, 96, 1) * sa + pltpu.roll(x, 32, 1) * sb
        if scale != 1.0:
            r = r * scale
        o_ref[:, sl] = r.astype(o_ref.dtype)


def _prep(p, tables, gq, gk, plan):
    b, s, n = p.shape
    tab_spec = pl.BlockSpec((ROW_TILE, LANES), lambda bi, i: (i, 0))
    g_spec = pl.BlockSpec((1, LANES), lambda bi, i: (0, 0))
    return pl.pallas_call(
        functools.partial(_prep_kernel, plan=plan),
        grid=(b, s // ROW_TILE),
        in_specs=[pl.BlockSpec((None, ROW_TILE, n), lambda bi, i: (bi, i, 0)),
                  tab_spec, tab_spec, tab_spec, g_spec, g_spec],
        out_specs=pl.BlockSpec((None, ROW_TILE, n), lambda bi, i: (bi, i, 0)),
        out_shape=jax.ShapeDtypeStruct((b, s, n), BF16),
        compiler_params=_params(("parallel", "parallel"), 32),
        name="qk_prep",
    )(p, *tables, gq, gk)


def _lane_tiles(s):
    return [s[:, j * LANES:(j + 1) * LANES] for j in range(s.shape[1] // LANES)]


def _stack_heads(q_ref, groups):
    return jnp.concatenate([q_ref[:, g * LANES:(g + 1) * LANES] for g in range(groups)], axis=0)


def _attn_oneshot_kernel(sink_ref, q_ref, k_ref, v_ref, o_ref, *, groups, tq, band, use_sink):
    h = pl.program_id(1)
    qi = pl.program_id(2)
    rows = groups * tq
    ctx0 = SEQ if band else 0
    q = _stack_heads(q_ref, groups)
    tiles = _lane_tiles(_dot_nt(q, k_ref[ctx0:ctx0 + CTX_LEN, :]))
    n_ctx_tiles = len(tiles)
    if band:
        nband = tq + 2 * BLOCK
        start = jnp.clip(qi * tq - BLOCK, 0, SEQ - nband)
        start = pl.multiple_of(start, BLOCK)
        row = lax.broadcasted_iota(jnp.int32, (rows, nband), 0)
        q_pos = qi * tq + (row & (tq - 1))
        k_pos = start + lax.broadcasted_iota(jnp.int32, (rows, nband), 1)
        valid = jnp.abs(q_pos - k_pos) <= WINDOW
        tiles += _lane_tiles(jnp.where(valid, _dot_nt(q, k_ref[pl.ds(start, nband), :]), NEG_INF))
    m = jnp.max(functools.reduce(jnp.maximum, tiles), axis=-1, keepdims=True)
    if use_sink:
        row1 = lax.broadcasted_iota(jnp.int32, (rows, 1), 0)
        sink = jnp.zeros((rows, 1), F32)
        for g in range(groups):
            sink = jnp.where(row1 >= g * tq, sink_ref[h * groups + g] * LOG2E, sink)
        m = jnp.maximum(m, sink)
    m_b = jnp.broadcast_to(m, (rows, LANES))
    ps = [jnp.exp2(t - m_b) for t in tiles]
    l = jnp.sum(functools.reduce(jnp.add, ps), axis=-1, keepdims=True)
    if use_sink:
        l = l + jnp.exp2(sink - m)
    p_c = jnp.concatenate([t.astype(BF16) for t in ps[:n_ctx_tiles]], axis=1)
    o = _dot(p_c, v_ref[ctx0:ctx0 + CTX_LEN, :])
    if band:
        p_b = jnp.concatenate([t.astype(BF16) for t in ps[n_ctx_tiles:]], axis=1)
        o = o + _dot(p_b, v_ref[pl.ds(start, nband), :])
    o = o / l
    for g in range(groups):
        o_ref[:, g * LANES:(g + 1) * LANES] = o[g * tq:(g + 1) * tq, :].astype(o_ref.dtype)


def _attn_oneshot(sink, qkv, *, q_col, k_col, v_col, groups, kv_heads, q_row0, n_q, tq, band, use_sink):
    b = qkv.shape[0]
    k_rows = S_ROWS if band else CTX_LEN
    k_row0 = 0 if band else SEQ // CTX_LEN
    gw = groups * LANES
    return pl.pallas_call(
        functools.partial(_attn_oneshot_kernel, groups=groups, tq=tq, band=band, use_sink=use_sink),
        grid=(b, kv_heads, n_q),
        in_specs=[
            pl.BlockSpec(memory_space=pltpu.SMEM),
            pl.BlockSpec((None, tq, gw), lambda bi, h, i: (bi, i + q_row0, q_col // groups + h)),
            pl.BlockSpec((None, k_rows, LANES), lambda bi, h, i: (bi, k_row0, k_col + h)),
            pl.BlockSpec((None, k_rows, LANES), lambda bi, h, i: (bi, k_row0, v_col + h)),
        ],
        out_specs=pl.BlockSpec((None, tq, gw), lambda bi, h, i: (bi, i, h)),
        out_shape=jax.ShapeDtypeStruct((b, n_q * tq, kv_heads * gw), BF16),
        compiler_params=_params(("parallel", "parallel", "parallel"), 32),
        name="attn_oneshot",
    )(sink, qkv, qkv, qkv)


def _attn_flash_kernel(lam_ref, subg_ref, q_ref, k_ref, v_ref, o_ref, q_sc, m_sc, l_sc, acc_sc,
                       *, streams, shared_k, tq, tk, diff_scale):
    dv = acc_sc.shape[1]
    for g in range(streams):
        q_sc[g * tq:(g + 1) * tq, :] = q_ref[:, g * LANES:(g + 1) * LANES]
    m_sc[...] = jnp.full(m_sc.shape, NEG_INF, F32)
    l_sc[...] = jnp.zeros(l_sc.shape, F32)
    acc_sc[...] = jnp.zeros(acc_sc.shape, F32)

    def chunk(off, size):
        kc = k_ref[pl.ds(off, size), :]
        if shared_k:
            s = _dot_nt(q_sc[...], kc)
        else:
            s = jnp.concatenate([_dot_nt(q_sc[g * tq:(g + 1) * tq, :], kc[:, g * LANES:(g + 1) * LANES])
                                 for g in range(streams)], axis=0)
        tiles = _lane_tiles(s)
        m_prev = m_sc[...]
        m_new = jnp.maximum(m_prev, jnp.max(functools.reduce(jnp.maximum, tiles), axis=-1, keepdims=True))
        alpha = jnp.exp2(m_prev - m_new)
        ps = [jnp.exp2(t - m_new) for t in tiles]
        m_sc[...] = m_new
        l_sc[...] = alpha * l_sc[...] + functools.reduce(jnp.add, ps)
        p = jnp.concatenate([t.astype(BF16) for t in ps], axis=1)
        pv = _dot(p, v_ref[pl.ds(off, size), :])
        acc_sc[...] = jnp.concatenate([alpha] * (dv // LANES), axis=1) * acc_sc[...] + pv

    chunk(SEQ, CTX_LEN)

    def body(c, carry):
        chunk(pl.multiple_of(c * tk, tk), tk)
        return carry

    lax.fori_loop(0, SEQ // tk, body, 0)
    o = acc_sc[...] / jnp.sum(l_sc[...], axis=-1, keepdims=True)
    if diff_scale is None:
        for g in range(streams):
            o_ref[:, g * LANES:(g + 1) * LANES] = o[g * tq:(g + 1) * tq, :].astype(o_ref.dtype)
    else:
        lambda_init, out_scale = diff_scale
        lp = lam_ref[...]
        lam = (jnp.exp(jnp.sum(lp[0:1, :] * lp[1:2, :], axis=-1, keepdims=True))
               - jnp.exp(jnp.sum(lp[2:3, :] * lp[3:4, :], axis=-1, keepdims=True)) + lambda_init)
        o = o[0:tq, :] - lam * o[tq:2 * tq, :]
        o_ref[...] = (_rms(o) * subg_ref[...] * out_scale).astype(o_ref.dtype)


def _attn_flash(lam, subg, qkv, *, q_col, k_col, v_col, streams, shared_k, dv, heads, tq, tk, diff_scale):
    b = qkv.shape[0]
    qw = streams * LANES
    kw = LANES if shared_k else streams * LANES
    n_q = SEQ // tq
    ow = dv if diff_scale is not None else qw
    rows = streams * tq
    return pl.pallas_call(
        functools.partial(_attn_flash_kernel, streams=streams, shared_k=shared_k, tq=tq, tk=tk,
                          diff_scale=diff_scale),
        grid=(b, heads, n_q),
        in_specs=[
            pl.BlockSpec((4, LANES), lambda bi, h, i: (0, 0)),
            pl.BlockSpec((1, dv), lambda bi, h, i: (0, 0)),
            pl.BlockSpec((None, tq, qw), lambda bi, h, i: (bi, i, q_col + h)),
            pl.BlockSpec((None, S_ROWS, kw), lambda bi, h, i: (bi, 0, k_col + h)),
            pl.BlockSpec((None, S_ROWS, dv), lambda bi, h, i: (bi, 0, v_col + h)),
        ],
        out_specs=pl.BlockSpec((None, tq, ow), lambda bi, h, i: (bi, i, h)),
        out_shape=jax.ShapeDtypeStruct((b, SEQ, heads * ow), BF16),
        scratch_shapes=[pltpu.VMEM((rows, LANES), BF16), pltpu.VMEM((rows, LANES), F32),
                        pltpu.VMEM((rows, LANES), F32), pltpu.VMEM((rows, dv), F32)],
        compiler_params=_params(("parallel", "parallel", "parallel"), 48),
        name="attn_flash",
    )(lam, subg, qkv, qkv, qkv)


def _outproj_kernel(y_ref, w_ref, x_ref, mod_ref, g_ref, wr_ref, xo_ref, hf_ref, aff_ref):
    m = mod_ref[...]
    g = g_ref[...]
    y = _dot(y_ref[...], w_ref[...])
    x1 = x_ref[...] + m[2:3, :] * (_rms(y) * g[1:2, :])
    xo_ref[...] = x1
    hf = _rms(x1) * g[2:3, :] * (1.0 + m[4:5, :]) + m[3:4, :]
    hf_ref[...] = hf
    logits = _dot(hf.astype(BF16), wr_ref[...])
    lane = lax.broadcasted_iota(jnp.int32, logits.shape, 1)
    logits = jnp.where(lane < N_EXPERTS, logits, NEG_INF)
    e = jnp.exp(logits - jnp.max(logits, axis=-1, keepdims=True))
    aff_ref[...] = e / jnp.sum(e, axis=-1, keepdims=True)


def _outproj(y, w_bf, x, mod_tab, g4, wr_bf, *, latent_only):
    b, sy, d = y.shape
    seg = (lambda i: 1) if latent_only else _seg
    row = lambda bi, i: (bi, i, 0)
    out_f32 = jax.ShapeDtypeStruct((b, sy, d), F32)
    return pl.pallas_call(
        _outproj_kernel,
        grid=(b, sy // ROW_TILE),
        in_specs=[
            pl.BlockSpec((None, ROW_TILE, d), row),
            pl.BlockSpec((d, d), lambda bi, i: (0, 0)),
            pl.BlockSpec((None, ROW_TILE, d), row),
            pl.BlockSpec((None, None, 6, d), lambda bi, i: (bi, seg(i), 0, 0)),
            pl.BlockSpec((4, d), lambda bi, i: (0, 0)),
            pl.BlockSpec((d, LANES), lambda bi, i: (0, 0)),
        ],
        out_specs=[pl.BlockSpec((None, ROW_TILE, d), row), pl.BlockSpec((None, ROW_TILE, d), row),
                   pl.BlockSpec((None, ROW_TILE, LANES), row)],
        out_shape=[out_f32, out_f32, jax.ShapeDtypeStruct((b, sy, LANES), F32)],
        compiler_params=_params(("parallel", "parallel"), 48),
        name="out_proj",
    )(y, w_bf, x, mod_tab, g4, wr_bf)


def _moe_kernel(xg_ref, wg_ref, wu_ref, wd_ref, gate_ref, o_ref):
    f = pl.program_id(1)
    x = xg_ref[...]
    a = _dot(x, wg_ref[...].astype(BF16))
    u = _dot(x, wu_ref[...].astype(BF16))
    hmid = (a * jax.nn.sigmoid(a) * u).astype(BF16)
    y = _dot(hmid, wd_ref[...].astype(BF16))

    @pl.when(f == 0)
    def _():
        o_ref[...] = y

    @pl.when(f > 0)
    def _():
        o_ref[...] += y

    @pl.when(f == pl.num_programs(1) - 1)
    def _():
        o_ref[...] = o_ref[...] * gate_ref[...]


def _moe(xg, w_g, w_u, w_d, gates):
    e, m, d = xg.shape
    f_dim = w_g.shape[-1]
    tf = 256
    return pl.pallas_call(
        _moe_kernel,
        grid=(e, f_dim // tf),
        in_specs=[
            pl.BlockSpec((None, m, d), lambda ei, f: (ei, 0, 0)),
            pl.BlockSpec((None, d, tf), lambda ei, f: (ei, 0, f)),
            pl.BlockSpec((None, d, tf), lambda ei, f: (ei, 0, f)),
            pl.BlockSpec((None, tf, d), lambda ei, f: (ei, f, 0)),
            pl.BlockSpec((None, m, 1), lambda ei, f: (ei, 0, 0)),
        ],
        out_specs=pl.BlockSpec((None, m, d), lambda ei, f: (ei, 0, 0)),
        out_shape=jax.ShapeDtypeStruct((e, m, d), F32),
        compiler_params=_params(("parallel", "arbitrary"), 56),
        name="moe_experts",
    )(xg, w_g, w_u, w_d, gates)


def _post_moe_kernel(x_ref, moe_ref, mod_ref, g_ref, modn_ref, gn_ref, xo_ref, h_ref):
    m = mod_ref[...]
    x2 = x_ref[...] + m[5:6, :] * (_rms(moe_ref[...]) * g_ref[3:4, :])
    xo_ref[...] = x2
    if h_ref is not None:
        mn = modn_ref[...]
        h_ref[...] = (_rms(x2) * gn_ref[0:1, :] * (1.0 + mn[1:2, :]) + mn[0:1, :]).astype(h_ref.dtype)


def _post_moe_last_kernel(x_ref, moe_ref, mod_ref, g_ref, xo_ref):
    _post_moe_kernel(x_ref, moe_ref, mod_ref, g_ref, None, None, xo_ref, None)


def _post_moe(x, moe, mod_tab, g4, mod_next, g4_next, *, latent_only):
    b, s, d = x.shape
    seg = (lambda i: 1) if latent_only else _seg
    row = pl.BlockSpec((None, ROW_TILE, d), lambda bi, i: (bi, i, 0))
    mod_spec = pl.BlockSpec((None, None, 6, d), lambda bi, i: (bi, seg(i), 0, 0))
    g_spec = pl.BlockSpec((4, d), lambda bi, i: (0, 0))
    has_next = mod_next is not None
    if has_next:
        kern, ins, in_specs = _post_moe_kernel, (x, moe, mod_tab, g4, mod_next, g4_next), \
            [row, row, mod_spec, g_spec, mod_spec, g_spec]
        out_specs = [row, row]
        out_shape = [jax.ShapeDtypeStruct((b, s, d), F32), jax.ShapeDtypeStruct((b, s, d), BF16)]
    else:
        kern, ins, in_specs = _post_moe_last_kernel, (x, moe, mod_tab, g4), [row, row, mod_spec, g_spec]
        out_specs = row
        out_shape = jax.ShapeDtypeStruct((b, s, d), F32)
    return pl.pallas_call(
        kern, grid=(b, s // ROW_TILE), in_specs=in_specs, out_specs=out_specs, out_shape=out_shape,
        compiler_params=_params(("parallel", "parallel"), 40), name="post_moe",
    )(*ins)


def _route_and_run_experts(hf, aff, segments, w_g, w_u, w_d):
    b, sx, d = hf.shape
    idx_all, gate_all = [], []
    for row0, n in segments:
        cap = EC_FACTOR * n // N_EXPERTS
        a = jnp.swapaxes(aff[:, row0:row0 + n, :N_EXPERTS], 1, 2)
        gate, idx = lax.top_k(a, cap)
        idx = idx + row0 + (jnp.arange(b, dtype=idx.dtype) * sx)[:, None, None]
        idx_all.append(jnp.swapaxes(idx, 0, 1).reshape(N_EXPERTS, b * cap))
        gate_all.append(jnp.swapaxes(gate, 0, 1).reshape(N_EXPERTS, b * cap))
    idx = jnp.concatenate(idx_all, axis=1)
    gates = jnp.concatenate(gate_all, axis=1)
    hflat = hf.reshape(b * sx, d)
    xg = jnp.take(hflat, idx, axis=0).astype(BF16)
    y = _moe(xg, w_g, w_u, w_d, gates[..., None])
    out = jnp.zeros((b * sx, d), F32).at[idx.reshape(-1)].add(y.reshape(-1, d))
    return out.reshape(b, sx, d)


_SCALE = HEAD_DIM ** -0.5 * LOG2E
_PLAN_AB = tuple([(None, _SCALE)] * 8 + [(None, 1.0)] * 2 + [(None, None)] * 2
                 + [("q", _SCALE)] * 8 + [("k", 1.0)] * 2 + [(None, None)] * 2)
_PLAN_C = tuple([(None, _SCALE)] * 16 + [(None, 1.0)] * 16 + [(None, None)] * 16)


def kernel(x, c, ctx, c_ctx, mod_w, mod_b, norm_g, ab_w_in, ab_w_out, ab_sink, ab_q_norm, ab_k_norm,
           dif_w_in, dif_w_out, dif_lambda, dif_subln, router_w, exp_w_gate, exp_w_up, exp_w_down):
    b = x.shape[0]
    d = D_MODEL
    cv = jnp.concatenate([c, c_ctx[None, :], jnp.zeros((8 - b - 1, d), F32)], axis=0)
    mods = _mods(cv, mod_w, mod_b).reshape(DEPTH, 8, 6, d)
    mod_tab = jnp.stack([jnp.broadcast_to(mods[:, b][:, None], (DEPTH, b, 6, d)), mods[:, :b]], axis=2)
    tables = _rope_tables()
    ones = jnp.ones((1, LANES), F32)
    wr = jnp.pad(router_w, ((0, 0), (0, 0), (0, LANES - N_EXPERTS))).astype(BF16)

    xs = jnp.concatenate([x, ctx], axis=1)

    h0 = _norm_mod(xs, norm_g[0], mod_tab[0])
    p0 = _matmul(h0.reshape(b * S_ROWS, d), _cast_bf16(ab_w_in[0]), BF16).reshape(b, S_ROWS, AB_IN)
    qkv0 = _prep(p0, tables, ab_q_norm[0][None, :], ab_k_norm[0][None, :], _PLAN_AB)
    sink = ab_sink[0]
    common_a = dict(q_col=0, k_col=8, v_col=10, groups=4, kv_heads=A_KV_HEADS)
    common_b = dict(q_col=12, k_col=20, v_col=22, groups=4, kv_heads=B_KV_HEADS)
    ctx_q = dict(q_row0=SEQ // CTX_LEN, n_q=1, tq=CTX_LEN, band=False)
    ya_l = _attn_oneshot(sink, qkv0, **common_a, q_row0=0, n_q=SEQ // 256, tq=256, band=True, use_sink=True)
    ya_c = _attn_oneshot(sink, qkv0, **common_a, **ctx_q, use_sink=True)
    yb_c = _attn_oneshot(sink, qkv0, **common_b, **ctx_q, use_sink=False)
    yb_l = _attn_flash(jnp.zeros((4, LANES), F32), ones, qkv0, q_col=3, k_col=20, v_col=22, streams=4,
                       shared_k=True, dv=HEAD_DIM, heads=B_KV_HEADS, tq=256, tk=512, diff_scale=None)
    y0 = jnp.concatenate([jnp.concatenate([ya_l, yb_l], axis=-1), jnp.concatenate([ya_c, yb_c], axis=-1)], axis=1)
    x1, hf0, aff0 = _outproj(y0, _cast_bf16(ab_w_out[0]), xs, mod_tab[0], norm_g[0], wr[0], latent_only=False)
    moe0 = _route_and_run_experts(hf0, aff0, [(0, SEQ), (SEQ, CTX_LEN)],
                                  exp_w_gate[0], exp_w_up[0], exp_w_down[0])
    x2, h1 = _post_moe(x1, moe0, mod_tab[0], norm_g[0], mod_tab[1], norm_g[1], latent_only=False)

    p1 = _matmul(h1.reshape(b * S_ROWS, d), _cast_bf16(dif_w_in[0]), BF16).reshape(b, S_ROWS, C_IN)
    qkv1 = _prep(p1, tables, ones, ones, _PLAN_C)
    lambda_init = 0.8 - 0.6 * math.exp(-0.3 * 1)
    y1 = _attn_flash(dif_lambda[0], dif_subln[0][None, :], qkv1, q_col=0, k_col=8, v_col=16, streams=2,
                     shared_k=False, dv=C_V_DIM, heads=C_HEADS, tq=512, tk=512,
                     diff_scale=(lambda_init, 1.0 - lambda_init))
    x3, hf1, aff1 = _outproj(y1, _cast_bf16(dif_w_out[0]), x2, mod_tab[1], norm_g[1], wr[1], latent_only=True)
    moe1 = _route_and_run_experts(hf1, aff1, [(0, SEQ)], exp_w_gate[1], exp_w_up[1], exp_w_down[1])
    return _post_moe(x3, moe1, mod_tab[1], norm_g[1], None, None, latent_only=True)
```

```python
import functools
import math

import jax
import jax.numpy as jnp
from jax import lax
from jax.experimental import pallas as pl
from jax.experimental.pallas import tpu as pltpu

D_MODEL = 2048
SEQ = 4096
DEPTH = 2
GRID_W = 64
CTX_LEN = 256
HEAD_DIM = 128
ROPE_BASE = 10000.0
BLOCK = 128
WINDOW = 128
A_Q_HEADS = 8
A_KV_HEADS = 2
B_Q_HEADS = 8
B_KV_HEADS = 2
AB_IN = 3072
C_HEADS = 8
C_V_DIM = 256
C_IN = 6144
N_EXPERTS = 16
EC_FACTOR = 2
EPS = 1e-6
NEG_INF = -1e30

S_ROWS = SEQ + CTX_LEN
ROW_TILE = 256
LANES = 128
MIB = 1024 * 1024
LOG2E = math.log2(math.e)

F32 = jnp.float32
BF16 = jnp.bfloat16


def _params(sem, vmem_mib):
    return pltpu.CompilerParams(dimension_semantics=sem, vmem_limit_bytes=vmem_mib * MIB)


def _rms(x):
    return x * lax.rsqrt(jnp.mean(x * x, axis=-1, keepdims=True) + EPS)


def _dot(a, b):
    return jnp.dot(a, b, preferred_element_type=F32)


def _dot_nt(a, b):
    return lax.dot_general(a, b, (((1,), (1,)), ((), ())), preferred_element_type=F32)


def _mods_kernel(cv_ref, w_ref, b_ref, o_ref):
    cv = cv_ref[...]
    s = cv * jax.nn.sigmoid(cv)
    o_ref[...] = _dot(s.astype(BF16), w_ref[...].astype(BF16)) + b_ref[...]


def _mods(cv, mod_w, mod_b):
    tn = 1024
    n = mod_w.shape[-1]
    return pl.pallas_call(
        _mods_kernel,
        grid=(DEPTH, n // tn),
        in_specs=[
            pl.BlockSpec((8, D_MODEL), lambda l, j: (0, 0)),
            pl.BlockSpec((None, D_MODEL, tn), lambda l, j: (l, 0, j)),
            pl.BlockSpec((None, 1, tn), lambda l, j: (l, 0, j)),
        ],
        out_specs=pl.BlockSpec((None, 8, tn), lambda l, j: (l, 0, j)),
        out_shape=jax.ShapeDtypeStruct((DEPTH, 8, n), F32),
        compiler_params=_params(("parallel", "parallel"), 40),
        name="mods",
    )(cv, mod_w, mod_b.reshape(DEPTH, 1, n))


def _cast_kernel(x_ref, o_ref):
    o_ref[...] = x_ref[...].astype(o_ref.dtype)


def _cast_bf16(w):
    k, n = w.shape
    tk = 512
    return pl.pallas_call(
        _cast_kernel,
        grid=(k // tk,),
        in_specs=[pl.BlockSpec((tk, n), lambda i: (i, 0))],
        out_specs=pl.BlockSpec((tk, n), lambda i: (i, 0)),
        out_shape=jax.ShapeDtypeStruct((k, n), BF16),
        compiler_params=_params(("parallel",), 48),
        name="cast_bf16",
    )(w)


def _seg(i):
    return jnp.where(i < SEQ // ROW_TILE, 1, 0)


def _norm_mod_kernel(x_ref, g_ref, mod_ref, h_ref):
    m = mod_ref[...]
    h = _rms(x_ref[...]) * g_ref[0:1, :] * (1.0 + m[1:2, :]) + m[0:1, :]
    h_ref[...] = h.astype(h_ref.dtype)


def _norm_mod(x, g4, mod_tab):
    b, s, d = x.shape
    return pl.pallas_call(
        _norm_mod_kernel,
        grid=(b, s // ROW_TILE),
        in_specs=[
            pl.BlockSpec((None, ROW_TILE, d), lambda bi, i: (bi, i, 0)),
            pl.BlockSpec((4, d), lambda bi, i: (0, 0)),
            pl.BlockSpec((None, None, 6, d), lambda bi, i: (bi, _seg(i), 0, 0)),
        ],
        out_specs=pl.BlockSpec((None, ROW_TILE, d), lambda bi, i: (bi, i, 0)),
        out_shape=jax.ShapeDtypeStruct((b, s, d), BF16),
        compiler_params=_params(("parallel", "parallel"), 32),
        name="norm_mod",
    )(x, g4, mod_tab)


def _mm_kernel(a_ref, w_ref, o_ref):
    o_ref[...] = _dot(a_ref[...], w_ref[...]).astype(o_ref.dtype)


def _matmul(a, w, out_dtype):
    m, k = a.shape
    n = w.shape[1]
    tm, tn = 1088, 1024
    return pl.pallas_call(
        _mm_kernel,
        grid=(n // tn, m // tm),
        in_specs=[
            pl.BlockSpec((tm, k), lambda j, i: (i, 0)),
            pl.BlockSpec((k, tn), lambda j, i: (0, j)),
        ],
        out_specs=pl.BlockSpec((tm, tn), lambda j, i: (i, j)),
        out_shape=jax.ShapeDtypeStruct((m, n), out_dtype),
        compiler_params=_params(("parallel", "parallel"), 48),
        name="in_proj",
    )(a, w)


def _rope_tables():
    n_rows = SEQ // GRID_W
    row = jnp.repeat(jnp.arange(n_rows), GRID_W).astype(F32)
    col = jnp.tile(jnp.arange(GRID_W), n_rows).astype(F32)
    n_freq = HEAD_DIM // 4
    inv = ROPE_BASE ** (-jnp.arange(n_freq, dtype=F32) / n_freq)
    ar = row[:, None] * inv
    ac = col[:, None] * inv
    cr, sr, cc, sc = jnp.cos(ar), jnp.sin(ar), jnp.cos(ac), jnp.sin(ac)
    z = jnp.zeros_like(sr)
    c = jnp.concatenate([cr, cr, cc, cc], axis=-1)
    sa = jnp.concatenate([-sr, z, -sc, z], axis=-1)
    sb = jnp.concatenate([z, sr, z, sc], axis=-1)
    pad = jnp.zeros((CTX_LEN, HEAD_DIM), F32)
    return (jnp.concatenate([c, pad + 1.0], axis=0), jnp.concatenate([sa, pad], axis=0),
            jnp.concatenate([sb, pad], axis=0))


def _prep_kernel(p_ref, c_ref, sa_ref, sb_ref, gq_ref, gk_ref, o_ref, *, plan):
    c, sa, sb = c_ref[...], sa_ref[...], sb_ref[...]
    for s, (norm, scale) in enumerate(plan):
        sl = slice(s * LANES, (s + 1) * LANES)
        if scale is None:
            o_ref[:, sl] = p_ref[:, sl]
            continue
        x = p_ref[:, sl].astype(F32)
        if norm is not None:
            x = _rms(x) * (gq_ref[...] if norm == "q" else gk_ref[...])
        r = x * c + pltpu.roll(x, 96, 1) * sa + pltpu.roll(x, 32, 1) * sb
        if scale != 1.0:
            r = r * scale
        o_ref[:, sl] = r.astype(o_ref.dtype)


def _prep(p, tables, gq, gk, plan):
    b, s, n = p.shape
    tab_spec = pl.BlockSpec((ROW_TILE, LANES), lambda bi, i: (i, 0))
    g_spec = pl.BlockSpec((1, LANES), lambda bi, i: (0, 0))
    return pl.pallas_call(
        functools.partial(_prep_kernel, plan=plan),
        grid=(b, s // ROW_TILE),
        in_specs=[pl.BlockSpec((None, ROW_TILE, n), lambda bi, i: (bi, i, 0)),
                  tab_spec, tab_spec, tab_spec, g_spec, g_spec],
        out_specs=pl.BlockSpec((None, ROW_TILE, n), lambda bi, i: (bi, i, 0)),
        out_shape=jax.ShapeDtypeStruct((b, s, n), BF16),
        compiler_params=_params(("parallel", "parallel"), 32),
        name="qk_prep",
    )(p, *tables, gq, gk)


def _lane_tiles(s):
    return [s[:, j * LANES:(j + 1) * LANES] for j in range(s.shape[1] // LANES)]


def _stack_heads(q_ref, groups):
    return jnp.concatenate([q_ref[:, g * LANES:(g + 1) * LANES] for g in range(groups)], axis=0)


def _attn_oneshot_kernel(sink_ref, q_ref, k_ref, v_ref, o_ref, *, groups, tq, band, use_sink):
    h = pl.program_id(1)
    qi = pl.program_id(2)
    rows = groups * tq
    ctx0 = SEQ if band else 0
    q = _stack_heads(q_ref, groups)
    tiles = _lane_tiles(_dot_nt(q, k_ref[ctx0:ctx0 + CTX_LEN, :]))
    n_ctx_tiles = len(tiles)
    if band:
        nband = tq + 2 * BLOCK
        start = jnp.clip(qi * tq - BLOCK, 0, SEQ - nband)
        start = pl.multiple_of(start, BLOCK)
        row = lax.broadcasted_iota(jnp.int32, (rows, nband), 0)
        q_pos = qi * tq + (row & (tq - 1))
        k_pos = start + lax.broadcasted_iota(jnp.int32, (rows, nband), 1)
        valid = jnp.abs(q_pos - k_pos) <= WINDOW
        tiles += _lane_tiles(jnp.where(valid, _dot_nt(q, k_ref[pl.ds(start, nband), :]), NEG_INF))
    m = jnp.max(functools.reduce(jnp.maximum, tiles), axis=-1, keepdims=True)
    if use_sink:
        row1 = lax.broadcasted_iota(jnp.int32, (rows, 1), 0)
        sink = jnp.zeros((rows, 1), F32)
        for g in range(groups):
            sink = jnp.where(row1 >= g * tq, sink_ref[h * groups + g] * LOG2E, sink)
        m = jnp.maximum(m, sink)
    m_b = jnp.broadcast_to(m, (rows, LANES))
    ps = [jnp.exp2(t - m_b) for t in tiles]
    l = jnp.sum(functools.reduce(jnp.add, ps), axis=-1, keepdims=True)
    if use_sink:
        l = l + jnp.exp2(sink - m)
    p_c = jnp.concatenate([t.astype(BF16) for t in ps[:n_ctx_tiles]], axis=1)
    o = _dot(p_c, v_ref[ctx0:ctx0 + CTX_LEN, :])
    if band:
        p_b = jnp.concatenate([t.astype(BF16) for t in ps[n_ctx_tiles:]], axis=1)
        o = o + _dot(p_b, v_ref[pl.ds(start, nband), :])
    o = o / l
    for g in range(groups):
        o_ref[:, g * LANES:(g + 1) * LANES] = o[g * tq:(g + 1) * tq, :].astype(o_ref.dtype)


def _attn_oneshot(sink, qkv, *, q_col, k_col, v_col, groups, kv_heads, q_row0, n_q, tq, band, use_sink):
    b = qkv.shape[0]
    k_rows = S_ROWS if band else CTX_LEN
    k_row0 = 0 if band else SEQ // CTX_LEN
    gw = groups * LANES
    return pl.pallas_call(
        functools.partial(_attn_oneshot_kernel, groups=groups, tq=tq, band=band, use_sink=use_sink),
        grid=(b, kv_heads, n_q),
        in_specs=[
            pl.BlockSpec(memory_space=pltpu.SMEM),
            pl.BlockSpec((None, tq, gw), lambda bi, h, i: (bi, i + q_row0, q_col // groups + h)),
            pl.BlockSpec((None, k_rows, LANES), lambda bi, h, i: (bi, k_row0, k_col + h)),
            pl.BlockSpec((None, k_rows, LANES), lambda bi, h, i: (bi, k_row0, v_col + h)),
        ],
        out_specs=pl.BlockSpec((None, tq, gw), lambda bi, h, i: (bi, i, h)),
        out_shape=jax.ShapeDtypeStruct((b, n_q * tq, kv_heads * gw), BF16),
        compiler_params=_params(("parallel", "parallel", "parallel"), 32),
        name="attn_oneshot",
    )(sink, qkv, qkv, qkv)


def _attn_flash_kernel(lam_ref, subg_ref, q_ref, k_ref, v_ref, o_ref, q_sc, m_sc, l_sc, acc_sc,
                       *, streams, shared_k, tq, tk, diff_scale):
    dv = acc_sc.shape[1]
    for g in range(streams):
        q_sc[g * tq:(g + 1) * tq, :] = q_ref[:, g * LANES:(g + 1) * LANES]
    m_sc[...] = jnp.full(m_sc.shape, NEG_INF, F32)
    l_sc[...] = jnp.zeros(l_sc.shape, F32)
    acc_sc[...] = jnp.zeros(acc_sc.shape, F32)

    def chunk(off, size):
        kc = k_ref[pl.ds(off, size), :]
        if shared_k:
            s = _dot_nt(q_sc[...], kc)
        else:
            s = jnp.concatenate([_dot_nt(q_sc[g * tq:(g + 1) * tq, :], kc[:, g * LANES:(g + 1) * LANES])
                                 for g in range(streams)], axis=0)
        tiles = _lane_tiles(s)
        m_prev = m_sc[...]
        m_new = jnp.maximum(m_prev, jnp.max(functools.reduce(jnp.maximum, tiles), axis=-1, keepdims=True))
        alpha = jnp.exp2(m_prev - m_new)
        ps = [jnp.exp2(t - m_new) for t in tiles]
        m_sc[...] = m_new
        l_sc[...] = alpha * l_sc[...] + functools.reduce(jnp.add, ps)
        p = jnp.concatenate([t.astype(BF16) for t in ps], axis=1)
        pv = _dot(p, v_ref[pl.ds(off, size), :])
        acc_sc[...] = jnp.concatenate([alpha] * (dv // LANES), axis=1) * acc_sc[...] + pv

    chunk(SEQ, CTX_LEN)

    def body(c, carry):
        chunk(pl.multiple_of(c * tk, tk), tk)
        return carry

    lax.fori_loop(0, SEQ // tk, body, 0)
    o = acc_sc[...] / jnp.sum(l_sc[...], axis=-1, keepdims=True)
    if diff_scale is None:
        for g in range(streams):
            o_ref[:, g * LANES:(g + 1) * LANES] = o[g * tq:(g + 1) * tq, :].astype(o_ref.dtype)
    else:
        lambda_init, out_scale = diff_scale
        lp = lam_ref[...]
        lam = (jnp.exp(jnp.sum(lp[0:1, :] * lp[1:2, :], axis=-1, keepdims=True))
               - jnp.exp(jnp.sum(lp[2:3, :] * lp[3:4, :], axis=-1, keepdims=True)) + lambda_init)
        o = o[0:tq, :] - lam * o[tq:2 * tq, :]
        o_ref[...] = (_rms(o) * subg_ref[...] * out_scale).astype(o_ref.dtype)


def _attn_flash(lam, subg, qkv, *, q_col, k_col, v_col, streams, shared_k, dv, heads, tq, tk, diff_scale):
    b = qkv.shape[0]
    qw = streams * LANES
    kw = LANES if shared_k else streams * LANES
    n_q = SEQ // tq
    ow = dv if diff_scale is not None else qw
    rows = streams * tq
    return pl.pallas_call(
        functools.partial(_attn_flash_kernel, streams=streams, shared_k=shared_k, tq=tq, tk=tk,
                          diff_scale=diff_scale),
        grid=(b, heads, n_q),
        in_specs=[
            pl.BlockSpec((4, LANES), lambda bi, h, i: (0, 0)),
            pl.BlockSpec((1, dv), lambda bi, h, i: (0, 0)),
            pl.BlockSpec((None, tq, qw), lambda bi, h, i: (bi, i, q_col + h)),
            pl.BlockSpec((None, S_ROWS, kw), lambda bi, h, i: (bi, 0, k_col + h)),
            pl.BlockSpec((None, S_ROWS, dv), lambda bi, h, i: (bi, 0, v_col + h)),
        ],
        out_specs=pl.BlockSpec((None, tq, ow), lambda bi, h, i: (bi, i, h)),
        out_shape=jax.ShapeDtypeStruct((b, SEQ, heads * ow), BF16),
        scratch_shapes=[pltpu.VMEM((rows, LANES), BF16), pltpu.VMEM((rows, LANES), F32),
                        pltpu.VMEM((rows, LANES), F32), pltpu.VMEM((rows, dv), F32)],
        compiler_params=_params(("parallel", "parallel", "parallel"), 48),
        name="attn_flash",
    )(lam, subg, qkv, qkv, qkv)


def _outproj_kernel(y_ref, w_ref, x_ref, mod_ref, g_ref, wr_ref, xo_ref, hf_ref, aff_ref):
    m = mod_ref[...]
    g = g_ref[...]
    y = _dot(y_ref[...], w_ref[...])
    x1 = x_ref[...] + m[2:3, :] * (_rms(y) * g[1:2, :])
    xo_ref[...] = x1
    hf = _rms(x1) * g[2:3, :] * (1.0 + m[4:5, :]) + m[3:4, :]
    hf_ref[...] = hf
    logits = _dot(hf.astype(BF16), wr_ref[...]).T[0:N_EXPERTS, :]
    e = jnp.exp(logits - jnp.max(logits, axis=0, keepdims=True))
    aff_ref[...] = e / jnp.sum(e, axis=0, keepdims=True)


def _outproj(y, w_bf, x, mod_tab, g4, wr_bf, *, latent_only):
    b, sy, d = y.shape
    seg = (lambda i: 1) if latent_only else _seg
    row = lambda bi, i: (bi, i, 0)
    out_f32 = jax.ShapeDtypeStruct((b, sy, d), F32)
    return pl.pallas_call(
        _outproj_kernel,
        grid=(b, sy // ROW_TILE),
        in_specs=[
            pl.BlockSpec((None, ROW_TILE, d), row),
            pl.BlockSpec((d, d), lambda bi, i: (0, 0)),
            pl.BlockSpec((None, ROW_TILE, d), row),
            pl.BlockSpec((None, None, 6, d), lambda bi, i: (bi, seg(i), 0, 0)),
            pl.BlockSpec((4, d), lambda bi, i: (0, 0)),
            pl.BlockSpec((d, LANES), lambda bi, i: (0, 0)),
        ],
        out_specs=[pl.BlockSpec((None, ROW_TILE, d), row), pl.BlockSpec((None, ROW_TILE, d), row),
                   pl.BlockSpec((None, N_EXPERTS, ROW_TILE), lambda bi, i: (bi, 0, i))],
        out_shape=[out_f32, out_f32, jax.ShapeDtypeStruct((b, N_EXPERTS, sy), F32)],
        compiler_params=_params(("parallel", "parallel"), 48),
        name="out_proj",
    )(y, w_bf, x, mod_tab, g4, wr_bf)


def _row_copy(h_hbm, stage_sc, sem, src_row, dst_row, n_rows):
    return pltpu.make_async_copy(h_hbm.at[pl.ds(src_row, n_rows), :], stage_sc.at[pl.ds(dst_row, n_rows), :], sem)


def _moe_kernel(idx_ref, h_hbm, wg_ref, wu_ref, wd_ref, gate_ref, o_ref, stage_sc, x_sc, acc_sc, sem, *, n_f):
    e = pl.program_id(0)
    f = pl.program_id(1)
    last = (e == pl.num_programs(0) - 1) & (f == n_f - 1)
    m = x_sc.shape[0]
    rows_per_step = m // n_f

    def issue_one(expert, row):
        _row_copy(h_hbm, stage_sc, sem, idx_ref[expert, row], row, 1).start()

    @pl.when(f == 0)
    def _():
        @pl.when(e == 0)
        def _():
            def body(i, carry):
                issue_one(0, i)
                return carry

            lax.fori_loop(0, m, body, 0, unroll=8)

        _row_copy(h_hbm, stage_sc, sem, 0, 0, m).wait()
        x_sc[...] = stage_sc[...].astype(BF16)

    for i in range(rows_per_step):
        issue_one(e + 1, f * rows_per_step + i)

    x = x_sc[...]
    a = _dot(x, wg_ref[...].astype(BF16))
    u = _dot(x, wu_ref[...].astype(BF16))
    hmid = (a * jax.nn.sigmoid(a) * u).astype(BF16)
    wd = wd_ref[...].astype(BF16)
    half = m // 2
    for r in (0, half):
        y = _dot(hmid[r:r + half, :], wd)

        @pl.when(f == 0)
        def _():
            acc_sc[r:r + half, :] = y

        @pl.when(f > 0)
        def _():
            acc_sc[r:r + half, :] += y

    @pl.when(f == n_f - 1)
    def _():
        o_ref[...] = (acc_sc[...] * gate_ref[...]).astype(o_ref.dtype)

    @pl.when(last)
    def _():
        _row_copy(h_hbm, stage_sc, sem, 0, 0, m).wait()


def _moe(idx, h, w_g, w_u, w_d, layer, gates):
    e, m = idx.shape
    d = w_g.shape[2]
    f_dim = w_g.shape[-1]
    tf = 256
    n_f = f_dim // tf
    idx = jnp.concatenate([idx, jnp.zeros((1, m), idx.dtype)], axis=0)
    grid_spec = pltpu.PrefetchScalarGridSpec(
        num_scalar_prefetch=1,
        grid=(e, n_f),
        in_specs=[
            pl.BlockSpec(memory_space=pl.ANY),
            pl.BlockSpec((None, None, d, tf), lambda ei, f, idx_ref: (layer, ei, 0, f)),
            pl.BlockSpec((None, None, d, tf), lambda ei, f, idx_ref: (layer, ei, 0, f)),
            pl.BlockSpec((None, None, tf, d), lambda ei, f, idx_ref: (layer, ei, f, 0)),
            pl.BlockSpec((None, m, 1), lambda ei, f, idx_ref: (ei, 0, 0)),
        ],
        out_specs=pl.BlockSpec((None, m, d), lambda ei, f, idx_ref: (ei, 0, 0)),
        scratch_shapes=[pltpu.VMEM((m, d), F32), pltpu.VMEM((m, d), BF16), pltpu.VMEM((m, d), F32),
                        pltpu.SemaphoreType.DMA(())],
    )
    return pl.pallas_call(
        functools.partial(_moe_kernel, n_f=n_f),
        grid_spec=grid_spec,
        out_shape=jax.ShapeDtypeStruct((e, m, d), BF16),
        compiler_params=_params(("arbitrary", "arbitrary"), 58),
        name="moe_experts",
    )(idx, h, w_g, w_u, w_d, gates)


def _combine_kernel(tab_ref, y_ref, x_ref, mod_ref, g_ref, *rest, n_latent, has_next):
    if has_next:
        modn_ref, gn_ref, xo_ref, h_ref, acc_sc = rest
    else:
        xo_ref, acc_sc = rest
    i = pl.program_id(1)
    e = pl.program_id(2)
    tt = acc_sc.shape[0]
    k = y_ref.shape[0]
    k_main = k // LANES * LANES
    lane = lax.broadcasted_iota(jnp.int32, (tt, LANES), 1)
    slot = jnp.sum(jnp.where(lane == e, tab_ref[...], 0.0), axis=1, keepdims=True)
    slot_b = jnp.broadcast_to(slot, (tt, LANES))
    lane_f = lane.astype(F32)

    def one_hot(j):
        return jnp.where(slot_b == lane_f + float(j * LANES), 1.0, 0.0).astype(BF16)

    contrib = _dot(jnp.concatenate([one_hot(j) for j in range(k_main // LANES)], axis=1), y_ref[0:k_main, :])
    if k > k_main:
        pad = jnp.zeros((LANES - (k - k_main), y_ref.shape[1]), y_ref.dtype)
        contrib = contrib + _dot(one_hot(k_main // LANES), jnp.concatenate([y_ref[k_main:k, :], pad], axis=0))

    @pl.when(e == 0)
    def _():
        acc_sc[...] = contrib

    @pl.when(e > 0)
    def _():
        acc_sc[...] += contrib

    @pl.when(e == pl.num_programs(2) - 1)
    def _():
        is_lat = (i * tt + lax.broadcasted_iota(jnp.int32, (tt, 1), 0)) < n_latent

        def mod(ref, kk):
            return jnp.where(is_lat, ref[1, kk:kk + 1, :], ref[0, kk:kk + 1, :])

        x2 = x_ref[...] + mod(mod_ref, 5) * (_rms(acc_sc[...]) * g_ref[3:4, :])
        xo_ref[...] = x2
        if has_next:
            h_ref[...] = (_rms(x2) * gn_ref[0:1, :] * (1.0 + mod(modn_ref, 1)) + mod(modn_ref, 0)).astype(h_ref.dtype)


def _combine(tab, y, x, mod_tab, g4, mod_next, g4_next, *, tt, n_latent):
    b, sx, d = x.shape
    k = y.shape[1] // b
    has_next = mod_next is not None
    row = pl.BlockSpec((None, tt, d), lambda bi, i, e: (bi, i, 0))
    mod_spec = pl.BlockSpec((None, 2, 6, d), lambda bi, i, e: (bi, 0, 0, 0))
    g_spec = pl.BlockSpec((4, d), lambda bi, i, e: (0, 0))
    in_specs = [pl.BlockSpec((None, tt, LANES), lambda bi, i, e: (bi, i, 0)),
                pl.BlockSpec((None, k, d), lambda bi, i, e: (e, bi, 0)), row, mod_spec, g_spec]
    ins = [tab, y, x, mod_tab, g4]
    out_specs, out_shape = [row], [jax.ShapeDtypeStruct((b, sx, d), F32)]
    if has_next:
        in_specs += [mod_spec, g_spec]
        ins += [mod_next, g4_next]
        out_specs.append(row)
        out_shape.append(jax.ShapeDtypeStruct((b, sx, d), BF16))
    return pl.pallas_call(
        functools.partial(_combine_kernel, n_latent=n_latent, has_next=has_next),
        grid=(b, sx // tt, N_EXPERTS), in_specs=in_specs, out_specs=out_specs, out_shape=out_shape,
        scratch_shapes=[pltpu.VMEM((tt, d), F32)],
        compiler_params=_params(("parallel", "parallel", "arbitrary"), 48), name="moe_combine",
    )(*ins)


def _prefix_count(mask_tiles, tri):
    off = jnp.zeros((N_EXPERTS, 1), F32)
    out = []
    for mt in mask_tiles:
        w = _dot(jnp.where(mt, 1.0, 0.0).astype(BF16), tri) + off
        out.append(w)
        off = w[:, LANES - 1:LANES]
    return out


_MIN_EXP = -160.0
_EXP_STEPS = 8
_VALUE_STEPS = 40


def _route_kernel(aff_ref, tab_ref, idx_ref, gate_ref, slot_sc, a_sc, *, segments, rows_per_sample):
    row0 = (pl.program_id(0) * rows_per_sample).astype(F32)
    r_i = lax.broadcasted_iota(jnp.int32, (LANES, LANES), 0)
    c_i = lax.broadcasted_iota(jnp.int32, (LANES, LANES), 1)
    tri = jnp.where(r_i <= c_i, 1.0, 0.0).astype(BF16)
    lane1 = lax.broadcasted_iota(jnp.int32, (1, LANES), 1)
    shape = (N_EXPERTS, LANES)
    base = 0
    tile0 = 0
    for t0, n, cap in segments:
        nt = n // LANES
        a_t = [aff_ref[:, t0 + j * LANES:t0 + (j + 1) * LANES] for j in range(nt)]

        def count_ge(v, a_t=a_t):
            v_b = jnp.broadcast_to(v, shape)
            cnt = functools.reduce(jnp.add, [jnp.where(t >= v_b, 1.0, 0.0) for t in a_t])
            return jnp.sum(cnt, axis=1, keepdims=True)

        e_lo = jnp.full((N_EXPERTS, 1), _MIN_EXP, F32)
        e_hi = jnp.full((N_EXPERTS, 1), 1.0, F32)
        lo = jnp.zeros((N_EXPERTS, 1), F32)
        hi = jnp.full((N_EXPERTS, 1), 2.0, F32)
        for _ in range(_EXP_STEPS):
            e_mid = jnp.floor((e_lo + e_hi) * 0.5)
            v = jnp.exp2(e_mid)
            ok = count_ge(v) >= cap
            e_lo, lo = jnp.where(ok, e_mid, e_lo), jnp.where(ok, v, lo)
            e_hi, hi = jnp.where(ok, e_hi, e_mid), jnp.where(ok, hi, v)
        for _ in range(_VALUE_STEPS):
            v = (lo + hi) * 0.5
            ok = count_ge(v) >= cap
            lo, hi = jnp.where(ok, v, lo), jnp.where(ok, hi, v)
        lo_b = jnp.broadcast_to(lo, shape)
        hi_b = jnp.broadcast_to(hi, shape)
        gt = [t >= hi_b for t in a_t]
        eq = [(t >= lo_b) & (t < hi_b) for t in a_t]
        need = jnp.broadcast_to(cap - count_ge(hi), shape)
        eq_rank = _prefix_count(eq, tri)
        sel = [g | (q & (r <= need)) for g, q, r in zip(gt, eq, eq_rank)]
        pos = _prefix_count(sel, tri)
        for j in range(nt):
            slot = jnp.where(sel[j], pos[j] + (base - 1.0), -1.0)
            slot_sc[tile0 + j] = slot
            a_sc[tile0 + j] = a_t[j]
            padded = jnp.concatenate([slot, jnp.full((LANES - N_EXPERTS, LANES), -1.0, F32)], axis=0)
            tab_ref[t0 + j * LANES:t0 + (j + 1) * LANES, :] = padded.T

        def expert_body(e, carry, t0=t0, nt=nt, cap=cap, base=base, tile0=tile0):
            for c in range(-(-cap // LANES)):
                slot_id = (r_i + (base + c * LANES)).astype(F32)

                def tile_body(j, acc):
                    pe = jnp.broadcast_to(slot_sc[tile0 + j, pl.ds(e, 1), :], (LANES, LANES))
                    ae = jnp.broadcast_to(a_sc[tile0 + j, pl.ds(e, 1), :], (LANES, LANES))
                    tok = jnp.broadcast_to((lane1 + (t0 + j * LANES)).astype(F32) + row0, (LANES, LANES))
                    hit = pe == slot_id
                    return acc[0] + jnp.where(hit, tok, 0.0), acc[1] + jnp.where(hit, ae, 0.0)

                zero = jnp.zeros((LANES, LANES), F32)
                acc_i, acc_g = lax.fori_loop(0, nt, tile_body, (zero, zero))
                w = min(LANES, cap - c * LANES)
                lo = base + c * LANES
                col_i = jnp.broadcast_to(jnp.sum(acc_i, axis=1, keepdims=True), (LANES, LANES))
                col_g = jnp.broadcast_to(jnp.sum(acc_g, axis=1, keepdims=True), (LANES, LANES))
                idx_ref[e, :, lo:lo + w] = col_i.T[0:1, 0:w].astype(jnp.int32)
                gate_ref[e, :, lo:lo + w] = col_g.T[0:1, 0:w]
            return carry

        lax.fori_loop(0, N_EXPERTS, expert_body, 0)
        base += cap
        tile0 += nt


def _route(aff_t, segments, rows_per_sample):
    b, e, sy = aff_t.shape
    k = sum(cap for _, _, cap in segments)
    n_tiles = sum(n // LANES for _, n, _ in segments)
    return pl.pallas_call(
        functools.partial(_route_kernel, segments=segments, rows_per_sample=rows_per_sample),
        grid=(b,),
        in_specs=[pl.BlockSpec((None, e, sy), lambda bi: (bi, 0, 0))],
        out_specs=[pl.BlockSpec((None, sy, LANES), lambda bi: (bi, 0, 0)),
                   pl.BlockSpec((None, e, 1, k), lambda bi: (bi, 0, 0, 0)),
                   pl.BlockSpec((None, e, 1, k), lambda bi: (bi, 0, 0, 0))],
        out_shape=[jax.ShapeDtypeStruct((b, sy, LANES), F32), jax.ShapeDtypeStruct((b, e, 1, k), jnp.int32),
                   jax.ShapeDtypeStruct((b, e, 1, k), F32)],
        scratch_shapes=[pltpu.VMEM((n_tiles, e, LANES), F32), pltpu.VMEM((n_tiles, e, LANES), F32)],
        compiler_params=_params(("parallel",), 32),
        name="route_topk",
    )(aff_t)


def _moe_block(x, hf, aff_t, segments, w_g, w_u, w_d, layer, mod_tab, g4, mod_next, g4_next, *, tt, n_latent):
    b, sy = aff_t.shape[0], aff_t.shape[2]
    tab, idx, gates = _route(aff_t, segments, sy)
    k = idx.shape[-1]
    idx_em = jnp.swapaxes(idx[:, :, 0, :], 0, 1).reshape(N_EXPERTS, b * k)
    gates_em = jnp.swapaxes(gates[:, :, 0, :], 0, 1).reshape(N_EXPERTS, b * k, 1)
    y = _moe(idx_em, hf.reshape(b * sy, hf.shape[-1]), w_g, w_u, w_d, layer, gates_em)
    return _combine(tab, y, x, mod_tab, g4, mod_next, g4_next, tt=tt, n_latent=n_latent)


_SCALE = HEAD_DIM ** -0.5 * LOG2E
_PLAN_AB = tuple([(None, _SCALE)] * 8 + [(None, 1.0)] * 2 + [(None, None)] * 2
                 + [("q", _SCALE)] * 8 + [("k", 1.0)] * 2 + [(None, None)] * 2)
_PLAN_C = tuple([(None, _SCALE)] * 16 + [(None, 1.0)] * 16 + [(None, None)] * 16)


def kernel(x, c, ctx, c_ctx, mod_w, mod_b, norm_g, ab_w_in, ab_w_out, ab_sink, ab_q_norm, ab_k_norm,
           dif_w_in, dif_w_out, dif_lambda, dif_subln, router_w, exp_w_gate, exp_w_up, exp_w_down):
    b = x.shape[0]
    d = D_MODEL
    cv = jnp.concatenate([c, c_ctx[None, :], jnp.zeros((8 - b - 1, d), F32)], axis=0)
    mods = _mods(cv, mod_w, mod_b).reshape(DEPTH, 8, 6, d)
    mod_tab = jnp.stack([jnp.broadcast_to(mods[:, b][:, None], (DEPTH, b, 6, d)), mods[:, :b]], axis=2)
    tables = _rope_tables()
    ones = jnp.ones((1, LANES), F32)
    wr = jnp.pad(router_w, ((0, 0), (0, 0), (0, LANES - N_EXPERTS))).astype(BF16)

    xs = jnp.concatenate([x, ctx], axis=1)

    h0 = _norm_mod(xs, norm_g[0], mod_tab[0])
    p0 = _matmul(h0.reshape(b * S_ROWS, d), _cast_bf16(ab_w_in[0]), BF16).reshape(b, S_ROWS, AB_IN)
    qkv0 = _prep(p0, tables, ab_q_norm[0][None, :], ab_k_norm[0][None, :], _PLAN_AB)
    sink = ab_sink[0]
    common_a = dict(q_col=0, k_col=8, v_col=10, groups=4, kv_heads=A_KV_HEADS)
    common_b = dict(q_col=12, k_col=20, v_col=22, groups=4, kv_heads=B_KV_HEADS)
    ctx_q = dict(q_row0=SEQ // CTX_LEN, n_q=1, tq=CTX_LEN, band=False)
    ya_l = _attn_oneshot(sink, qkv0, **common_a, q_row0=0, n_q=SEQ // 256, tq=256, band=True, use_sink=True)
    ya_c = _attn_oneshot(sink, qkv0, **common_a, **ctx_q, use_sink=True)
    yb_c = _attn_oneshot(sink, qkv0, **common_b, **ctx_q, use_sink=False)
    yb_l = _attn_flash(jnp.zeros((4, LANES), F32), ones, qkv0, q_col=3, k_col=20, v_col=22, streams=4,
                       shared_k=True, dv=HEAD_DIM, heads=B_KV_HEADS, tq=256, tk=512, diff_scale=None)
    y0 = jnp.concatenate([jnp.concatenate([ya_l, yb_l], axis=-1), jnp.concatenate([ya_c, yb_c], axis=-1)], axis=1)
    x1, hf0, aff0 = _outproj(y0, _cast_bf16(ab_w_out[0]), xs, mod_tab[0], norm_g[0], wr[0], latent_only=False)
    cap_l = EC_FACTOR * SEQ // N_EXPERTS
    cap_c = EC_FACTOR * CTX_LEN // N_EXPERTS
    x2, h1 = _moe_block(x1, hf0, aff0, ((0, SEQ, cap_l), (SEQ, CTX_LEN, cap_c)),
                        exp_w_gate, exp_w_up, exp_w_down, 0, mod_tab[0], norm_g[0], mod_tab[1], norm_g[1],
                        tt=(cap_l + cap_c), n_latent=SEQ)

    p1 = _matmul(h1.reshape(b * S_ROWS, d), _cast_bf16(dif_w_in[0]), BF16).reshape(b, S_ROWS, C_IN)
    qkv1 = _prep(p1, tables, ones, ones, _PLAN_C)
    lambda_init = 0.8 - 0.6 * math.exp(-0.3 * 1)
    y1 = _attn_flash(dif_lambda[0], dif_subln[0][None, :], qkv1, q_col=0, k_col=8, v_col=16, streams=2,
                     shared_k=False, dv=C_V_DIM, heads=C_HEADS, tq=512, tk=512,
                     diff_scale=(lambda_init, 1.0 - lambda_init))
    x3, hf1, aff1 = _outproj(y1, _cast_bf16(dif_w_out[0]), x2, mod_tab[1], norm_g[1], wr[1], latent_only=True)
    (out,) = _moe_block(x3, hf1, aff1, ((0, SEQ, cap_l),), exp_w_gate, exp_w_up, exp_w_down, 1,
                        mod_tab[1], norm_g[1], None, None, tt=512, n_latent=SEQ)
    return out
```

```python
import functools
import math

import jax
import jax.numpy as jnp
from jax import lax
from jax.experimental import pallas as pl
from jax.experimental.pallas import tpu as pltpu

D_MODEL = 2048
SEQ = 4096
DEPTH = 2
GRID_W = 64
CTX_LEN = 256
HEAD_DIM = 128
ROPE_BASE = 10000.0
BLOCK = 128
WINDOW = 128
A_Q_HEADS = 8
A_KV_HEADS = 2
B_Q_HEADS = 8
B_KV_HEADS = 2
AB_IN = 3072
C_HEADS = 8
C_V_DIM = 256
C_IN = 6144
N_EXPERTS = 16
EC_FACTOR = 2
EPS = 1e-6
NEG_INF = -1e30

S_ROWS = SEQ + CTX_LEN
ROW_TILE = 256
LANES = 128
MIB = 1024 * 1024
LOG2E = math.log2(math.e)

F32 = jnp.float32
BF16 = jnp.bfloat16


def _params(sem, vmem_mib):
    return pltpu.CompilerParams(dimension_semantics=sem, vmem_limit_bytes=vmem_mib * MIB)


def _rms(x):
    return x * lax.rsqrt(jnp.mean(x * x, axis=-1, keepdims=True) + EPS)


def _dot(a, b):
    return jnp.dot(a, b, preferred_element_type=F32)


def _dot_nt(a, b):
    return lax.dot_general(a, b, (((1,), (1,)), ((), ())), preferred_element_type=F32)


def _mods_kernel(cv_ref, w_ref, b_ref, o_ref):
    cv = cv_ref[...]
    s = cv * jax.nn.sigmoid(cv)
    o_ref[...] = _dot(s.astype(BF16), w_ref[...].astype(BF16)) + b_ref[...]


def _mods(cv, mod_w, mod_b):
    tn = 1024
    n = mod_w.shape[-1]
    return pl.pallas_call(
        _mods_kernel,
        grid=(DEPTH, n // tn),
        in_specs=[
            pl.BlockSpec((8, D_MODEL), lambda l, j: (0, 0)),
            pl.BlockSpec((None, D_MODEL, tn), lambda l, j: (l, 0, j)),
            pl.BlockSpec((None, 1, tn), lambda l, j: (l, 0, j)),
        ],
        out_specs=pl.BlockSpec((None, 8, tn), lambda l, j: (l, 0, j)),
        out_shape=jax.ShapeDtypeStruct((DEPTH, 8, n), F32),
        compiler_params=_params(("parallel", "parallel"), 40),
        name="mods",
    )(cv, mod_w, mod_b.reshape(DEPTH, 1, n))


def _cast_kernel(x_ref, o_ref):
    o_ref[...] = x_ref[...].astype(o_ref.dtype)


def _cast_bf16(w):
    k, n = w.shape
    tk = 512
    return pl.pallas_call(
        _cast_kernel,
        grid=(k // tk,),
        in_specs=[pl.BlockSpec((tk, n), lambda i: (i, 0))],
        out_specs=pl.BlockSpec((tk, n), lambda i: (i, 0)),
        out_shape=jax.ShapeDtypeStruct((k, n), BF16),
        compiler_params=_params(("parallel",), 48),
        name="cast_bf16",
    )(w)


def _seg(i):
    return jnp.where(i < SEQ // ROW_TILE, 1, 0)


def _norm_mod_kernel(x_ref, g_ref, mod_ref, h_ref):
    m = mod_ref[...]
    h = _rms(x_ref[...]) * g_ref[0:1, :] * (1.0 + m[1:2, :]) + m[0:1, :]
    h_ref[...] = h.astype(h_ref.dtype)


def _norm_mod(x, g4, mod_tab):
    b, s, d = x.shape
    return pl.pallas_call(
        _norm_mod_kernel,
        grid=(b, s // ROW_TILE),
        in_specs=[
            pl.BlockSpec((None, ROW_TILE, d), lambda bi, i: (bi, i, 0)),
            pl.BlockSpec((4, d), lambda bi, i: (0, 0)),
            pl.BlockSpec((None, None, 6, d), lambda bi, i: (bi, _seg(i), 0, 0)),
        ],
        out_specs=pl.BlockSpec((None, ROW_TILE, d), lambda bi, i: (bi, i, 0)),
        out_shape=jax.ShapeDtypeStruct((b, s, d), BF16),
        compiler_params=_params(("parallel", "parallel"), 32),
        name="norm_mod",
    )(x, g4, mod_tab)


def _mm_kernel(a_ref, w_ref, o_ref):
    o_ref[...] = _dot(a_ref[...], w_ref[...]).astype(o_ref.dtype)


def _matmul(a, w, out_dtype):
    m, k = a.shape
    n = w.shape[1]
    tm, tn = 1088, 1024
    return pl.pallas_call(
        _mm_kernel,
        grid=(n // tn, m // tm),
        in_specs=[
            pl.BlockSpec((tm, k), lambda j, i: (i, 0)),
            pl.BlockSpec((k, tn), lambda j, i: (0, j)),
        ],
        out_specs=pl.BlockSpec((tm, tn), lambda j, i: (i, j)),
        out_shape=jax.ShapeDtypeStruct((m, n), out_dtype),
        compiler_params=_params(("parallel", "parallel"), 48),
        name="in_proj",
    )(a, w)


def _rope_tables():
    n_rows = SEQ // GRID_W
    row = jnp.repeat(jnp.arange(n_rows), GRID_W).astype(F32)
    col = jnp.tile(jnp.arange(GRID_W), n_rows).astype(F32)
    n_freq = HEAD_DIM // 4
    inv = ROPE_BASE ** (-jnp.arange(n_freq, dtype=F32) / n_freq)
    ar = row[:, None] * inv
    ac = col[:, None] * inv
    cr, sr, cc, sc = jnp.cos(ar), jnp.sin(ar), jnp.cos(ac), jnp.sin(ac)
    z = jnp.zeros_like(sr)
    c = jnp.concatenate([cr, cr, cc, cc], axis=-1)
    sa = jnp.concatenate([-sr, z, -sc, z], axis=-1)
    sb = jnp.concatenate([z, sr, z, sc], axis=-1)
    pad = jnp.zeros((CTX_LEN, HEAD_DIM), F32)
    return (jnp.concatenate([c, pad + 1.0], axis=0), jnp.concatenate([sa, pad], axis=0),
            jnp.concatenate([sb, pad], axis=0))


def _prep_kernel(p_ref, c_ref, sa_ref, sb_ref, gq_ref, gk_ref, o_ref, *, plan):
    c, sa, sb = c_ref[...], sa_ref[...], sb_ref[...]
    for s, (norm, scale) in enumerate(plan):
        sl = slice(s * LANES, (s + 1) * LANES)
        if scale is None:
            o_ref[:, sl] = p_ref[:, sl]
            continue
        x = p_ref[:, sl].astype(F32)
        if norm is not None:
            x = _rms(x) * (gq_ref[...] if norm == "q" else gk_ref[...])
        r = x * c + pltpu.roll(x, 96, 1) * sa + pltpu.roll(x, 32, 1) * sb
        if scale != 1.0:
            r = r * scale
        o_ref[:, sl] = r.astype(o_ref.dtype)


def _prep(p, tables, gq, gk, plan):
    b, s, n = p.shape
    tab_spec = pl.BlockSpec((ROW_TILE, LANES), lambda bi, i: (i, 0))
    g_spec = pl.BlockSpec((1, LANES), lambda bi, i: (0, 0))
    return pl.pallas_call(
        functools.partial(_prep_kernel, plan=plan),
        grid=(b, s // ROW_TILE),
        in_specs=[pl.BlockSpec((None, ROW_TILE, n), lambda bi, i: (bi, i, 0)),
                  tab_spec, tab_spec, tab_spec, g_spec, g_spec],
        out_specs=pl.BlockSpec((None, ROW_TILE, n), lambda bi, i: (bi, i, 0)),
        out_shape=jax.ShapeDtypeStruct((b, s, n), BF16),
        compiler_params=_params(("parallel", "parallel"), 32),
        name="qk_prep",
    )(p, *tables, gq, gk)


def _lane_tiles(s):
    return [s[:, j * LANES:(j + 1) * LANES] for j in range(s.shape[1] // LANES)]


def _stack_heads(q_ref, groups):
    return jnp.concatenate([q_ref[:, g * LANES:(g + 1) * LANES] for g in range(groups)], axis=0)


def _attn_oneshot_kernel(sink_ref, q_ref, k_ref, v_ref, o_ref, *, groups, tq, band, use_sink):
    h = pl.program_id(1)
    qi = pl.program_id(2)
    rows = groups * tq
    ctx0 = SEQ if band else 0
    q = _stack_heads(q_ref, groups)
    tiles = _lane_tiles(_dot_nt(q, k_ref[ctx0:ctx0 + CTX_LEN, :]))
    n_ctx_tiles = len(tiles)
    if band:
        nband = tq + 2 * BLOCK
        start = jnp.clip(qi * tq - BLOCK, 0, SEQ - nband)
        start = pl.multiple_of(start, BLOCK)
        row = lax.broadcasted_iota(jnp.int32, (rows, nband), 0)
        q_pos = qi * tq + (row & (tq - 1))
        k_pos = start + lax.broadcasted_iota(jnp.int32, (rows, nband), 1)
        valid = jnp.abs(q_pos - k_pos) <= WINDOW
        tiles += _lane_tiles(jnp.where(valid, _dot_nt(q, k_ref[pl.ds(start, nband), :]), NEG_INF))
    m = jnp.max(functools.reduce(jnp.maximum, tiles), axis=-1, keepdims=True)
    if use_sink:
        row1 = lax.broadcasted_iota(jnp.int32, (rows, 1), 0)
        sink = jnp.zeros((rows, 1), F32)
        for g in range(groups):
            sink = jnp.where(row1 >= g * tq, sink_ref[h * groups + g] * LOG2E, sink)
        m = jnp.maximum(m, sink)
    m_b = jnp.broadcast_to(m, (rows, LANES))
    ps = [jnp.exp2(t - m_b) for t in tiles]
    l = jnp.sum(functools.reduce(jnp.add, ps), axis=-1, keepdims=True)
    if use_sink:
        l = l + jnp.exp2(sink - m)
    p_c = jnp.concatenate([t.astype(BF16) for t in ps[:n_ctx_tiles]], axis=1)
    o = _dot(p_c, v_ref[ctx0:ctx0 + CTX_LEN, :])
    if band:
        p_b = jnp.concatenate([t.astype(BF16) for t in ps[n_ctx_tiles:]], axis=1)
        o = o + _dot(p_b, v_ref[pl.ds(start, nband), :])
    o = o / l
    for g in range(groups):
        o_ref[:, g * LANES:(g + 1) * LANES] = o[g * tq:(g + 1) * tq, :].astype(o_ref.dtype)


def _attn_oneshot(sink, qkv, *, q_col, k_col, v_col, groups, kv_heads, q_row0, n_q, tq, band, use_sink):
    b = qkv.shape[0]
    k_rows = S_ROWS if band else CTX_LEN
    k_row0 = 0 if band else SEQ // CTX_LEN
    gw = groups * LANES
    return pl.pallas_call(
        functools.partial(_attn_oneshot_kernel, groups=groups, tq=tq, band=band, use_sink=use_sink),
        grid=(b, kv_heads, n_q),
        in_specs=[
            pl.BlockSpec(memory_space=pltpu.SMEM),
            pl.BlockSpec((None, tq, gw), lambda bi, h, i: (bi, i + q_row0, q_col // groups + h)),
            pl.BlockSpec((None, k_rows, LANES), lambda bi, h, i: (bi, k_row0, k_col + h)),
            pl.BlockSpec((None, k_rows, LANES), lambda bi, h, i: (bi, k_row0, v_col + h)),
        ],
        out_specs=pl.BlockSpec((None, tq, gw), lambda bi, h, i: (bi, i, h)),
        out_shape=jax.ShapeDtypeStruct((b, n_q * tq, kv_heads * gw), BF16),
        compiler_params=_params(("parallel", "parallel", "parallel"), 32),
        name="attn_oneshot",
    )(sink, qkv, qkv, qkv)


def _attn_flash_kernel(lam_ref, subg_ref, q_ref, k_ref, v_ref, o_ref, q_sc, m_sc, l_sc, acc_sc,
                       *, streams, shared_k, tq, tk, diff_scale):
    dv = acc_sc.shape[1]
    for g in range(streams):
        q_sc[g * tq:(g + 1) * tq, :] = q_ref[:, g * LANES:(g + 1) * LANES]
    m_sc[...] = jnp.full(m_sc.shape, NEG_INF, F32)
    l_sc[...] = jnp.zeros(l_sc.shape, F32)
    acc_sc[...] = jnp.zeros(acc_sc.shape, F32)

    def chunk(off, size):
        kc = k_ref[pl.ds(off, size), :]
        if shared_k:
            s = _dot_nt(q_sc[...], kc)
        else:
            s = jnp.concatenate([_dot_nt(q_sc[g * tq:(g + 1) * tq, :], kc[:, g * LANES:(g + 1) * LANES])
                                 for g in range(streams)], axis=0)
        tiles = _lane_tiles(s)
        m_prev = m_sc[...]
        m_new = jnp.maximum(m_prev, jnp.max(functools.reduce(jnp.maximum, tiles), axis=-1, keepdims=True))
        alpha = jnp.exp2(m_prev - m_new)
        ps = [jnp.exp2(t - m_new) for t in tiles]
        m_sc[...] = m_new
        l_sc[...] = alpha * l_sc[...] + functools.reduce(jnp.add, ps)
        p = jnp.concatenate([t.astype(BF16) for t in ps], axis=1)
        pv = _dot(p, v_ref[pl.ds(off, size), :])
        acc_sc[...] = jnp.concatenate([alpha] * (dv // LANES), axis=1) * acc_sc[...] + pv

    chunk(SEQ, CTX_LEN)

    def body(c, carry):
        chunk(pl.multiple_of(c * tk, tk), tk)
        return carry

    lax.fori_loop(0, SEQ // tk, body, 0)
    o = acc_sc[...] / jnp.sum(l_sc[...], axis=-1, keepdims=True)
    if diff_scale is None:
        for g in range(streams):
            o_ref[:, g * LANES:(g + 1) * LANES] = o[g * tq:(g + 1) * tq, :].astype(o_ref.dtype)
    else:
        lambda_init, out_scale = diff_scale
        lp = lam_ref[...]
        lam = (jnp.exp(jnp.sum(lp[0:1, :] * lp[1:2, :], axis=-1, keepdims=True))
               - jnp.exp(jnp.sum(lp[2:3, :] * lp[3:4, :], axis=-1, keepdims=True)) + lambda_init)
        o = o[0:tq, :] - lam * o[tq:2 * tq, :]
        o_ref[...] = (_rms(o) * subg_ref[...] * out_scale).astype(o_ref.dtype)


def _attn_flash(lam, subg, qkv, *, q_col, k_col, v_col, streams, shared_k, dv, heads, tq, tk, diff_scale):
    b = qkv.shape[0]
    qw = streams * LANES
    kw = LANES if shared_k else streams * LANES
    n_q = SEQ // tq
    ow = dv if diff_scale is not None else qw
    rows = streams * tq
    return pl.pallas_call(
        functools.partial(_attn_flash_kernel, streams=streams, shared_k=shared_k, tq=tq, tk=tk,
                          diff_scale=diff_scale),
        grid=(b, heads, n_q),
        in_specs=[
            pl.BlockSpec((4, LANES), lambda bi, h, i: (0, 0)),
            pl.BlockSpec((1, dv), lambda bi, h, i: (0, 0)),
            pl.BlockSpec((None, tq, qw), lambda bi, h, i: (bi, i, q_col + h)),
            pl.BlockSpec((None, S_ROWS, kw), lambda bi, h, i: (bi, 0, k_col + h)),
            pl.BlockSpec((None, S_ROWS, dv), lambda bi, h, i: (bi, 0, v_col + h)),
        ],
        out_specs=pl.BlockSpec((None, tq, ow), lambda bi, h, i: (bi, i, h)),
        out_shape=jax.ShapeDtypeStruct((b, SEQ, heads * ow), BF16),
        scratch_shapes=[pltpu.VMEM((rows, LANES), BF16), pltpu.VMEM((rows, LANES), F32),
                        pltpu.VMEM((rows, LANES), F32), pltpu.VMEM((rows, dv), F32)],
        compiler_params=_params(("parallel", "parallel", "parallel"), 48),
        name="attn_flash",
    )(lam, subg, qkv, qkv, qkv)


def _outproj_kernel(y_ref, w_ref, x_ref, mod_ref, g_ref, wr_ref, xo_ref, hf_ref, aff_ref):
    m = mod_ref[...]
    g = g_ref[...]
    y = _dot(y_ref[...], w_ref[...])
    x1 = x_ref[...] + m[2:3, :] * (_rms(y) * g[1:2, :])
    xo_ref[...] = x1
    hf = _rms(x1) * g[2:3, :] * (1.0 + m[4:5, :]) + m[3:4, :]
    hf_ref[...] = hf
    logits = _dot(hf.astype(BF16), wr_ref[...]).T[0:N_EXPERTS, :]
    e = jnp.exp(logits - jnp.max(logits, axis=0, keepdims=True))
    aff_ref[...] = e / jnp.sum(e, axis=0, keepdims=True)


def _outproj(y, w_bf, x, mod_tab, g4, wr_bf, *, latent_only):
    b, sy, d = y.shape
    seg = (lambda i: 1) if latent_only else _seg
    row = lambda bi, i: (bi, i, 0)
    out_f32 = jax.ShapeDtypeStruct((b, sy, d), F32)
    return pl.pallas_call(
        _outproj_kernel,
        grid=(b, sy // ROW_TILE),
        in_specs=[
            pl.BlockSpec((None, ROW_TILE, d), row),
            pl.BlockSpec((d, d), lambda bi, i: (0, 0)),
            pl.BlockSpec((None, ROW_TILE, d), row),
            pl.BlockSpec((None, None, 6, d), lambda bi, i: (bi, seg(i), 0, 0)),
            pl.BlockSpec((4, d), lambda bi, i: (0, 0)),
            pl.BlockSpec((d, LANES), lambda bi, i: (0, 0)),
        ],
        out_specs=[pl.BlockSpec((None, ROW_TILE, d), row), pl.BlockSpec((None, ROW_TILE, d), row),
                   pl.BlockSpec((None, N_EXPERTS, ROW_TILE), lambda bi, i: (bi, 0, i))],
        out_shape=[out_f32, out_f32, jax.ShapeDtypeStruct((b, N_EXPERTS, sy), F32)],
        compiler_params=_params(("parallel", "parallel"), 48),
        name="out_proj",
    )(y, w_bf, x, mod_tab, g4, wr_bf)


def _row_copy(h_hbm, stage_sc, sem, src_row, dst_row, n_rows):
    return pltpu.make_async_copy(h_hbm.at[pl.ds(src_row, n_rows), :], stage_sc.at[pl.ds(dst_row, n_rows), :], sem)


def _moe_kernel(idx_ref, h_hbm, wg_ref, wu_ref, wd_ref, gate_ref, o_ref, stage_sc, x_sc, acc_sc, sem, *, n_f):
    e = pl.program_id(0)
    f = pl.program_id(1)
    last = (e == pl.num_programs(0) - 1) & (f == n_f - 1)
    m = x_sc.shape[0]
    rows_per_step = m // n_f

    def issue_one(expert, row):
        _row_copy(h_hbm, stage_sc, sem, idx_ref[expert, row], row, 1).start()

    @pl.when(f == 0)
    def _():
        @pl.when(e == 0)
        def _():
            def body(i, carry):
                issue_one(0, i)
                return carry

            lax.fori_loop(0, m, body, 0, unroll=8)

        _row_copy(h_hbm, stage_sc, sem, 0, 0, m).wait()
        x_sc[...] = stage_sc[...].astype(BF16)

    for i in range(rows_per_step):
        issue_one(e + 1, f * rows_per_step + i)

    x = x_sc[...]
    a = _dot(x, wg_ref[...].astype(BF16))
    u = _dot(x, wu_ref[...].astype(BF16))
    hmid = (a * jax.nn.sigmoid(a) * u).astype(BF16)
    wd = wd_ref[...].astype(BF16)
    half = m // 2
    for r in (0, half):
        y = _dot(hmid[r:r + half, :], wd)

        @pl.when(f == 0)
        def _():
            acc_sc[r:r + half, :] = y

        @pl.when(f > 0)
        def _():
            acc_sc[r:r + half, :] += y

    @pl.when(f == n_f - 1)
    def _():
        o_ref[...] = (acc_sc[...] * gate_ref[...]).astype(o_ref.dtype)

    @pl.when(last)
    def _():
        _row_copy(h_hbm, stage_sc, sem, 0, 0, m).wait()


def _moe(idx, h, w_g, w_u, w_d, layer, gates):
    e, m = idx.shape
    d = w_g.shape[2]
    f_dim = w_g.shape[-1]
    tf = 256
    n_f = f_dim // tf
    idx = jnp.concatenate([idx, jnp.zeros((1, m), idx.dtype)], axis=0)
    grid_spec = pltpu.PrefetchScalarGridSpec(
        num_scalar_prefetch=1,
        grid=(e, n_f),
        in_specs=[
            pl.BlockSpec(memory_space=pl.ANY),
            pl.BlockSpec((None, None, d, tf), lambda ei, f, idx_ref: (layer, ei, 0, f)),
            pl.BlockSpec((None, None, d, tf), lambda ei, f, idx_ref: (layer, ei, 0, f)),
            pl.BlockSpec((None, None, tf, d), lambda ei, f, idx_ref: (layer, ei, f, 0)),
            pl.BlockSpec((None, m, 1), lambda ei, f, idx_ref: (ei, 0, 0)),
        ],
        out_specs=pl.BlockSpec((None, m, d), lambda ei, f, idx_ref: (ei, 0, 0)),
        scratch_shapes=[pltpu.VMEM((m, d), F32), pltpu.VMEM((m, d), BF16), pltpu.VMEM((m, d), F32),
                        pltpu.SemaphoreType.DMA(())],
    )
    return pl.pallas_call(
        functools.partial(_moe_kernel, n_f=n_f),
        grid_spec=grid_spec,
        out_shape=jax.ShapeDtypeStruct((e, m, d), BF16),
        compiler_params=_params(("arbitrary", "arbitrary"), 58),
        name="moe_experts",
    )(idx, h, w_g, w_u, w_d, gates)


_BF16_ROWS = 16
_WIN = LANES + _BF16_ROWS


def _combine_kernel(starts_ref, tab_ref, y_hbm, x_ref, mod_ref, g_ref, *rest, cap_t, n_latent, has_next):
    if has_next:
        modn_ref, gn_ref, xo_ref, h_ref, buf, sem = rest
    else:
        xo_ref, buf, sem = rest
    b = pl.program_id(0)
    j = pl.program_id(1)
    nb = pl.num_programs(0)
    n_tiles = pl.num_programs(1)
    t = b * n_tiles + j
    cur = lax.rem(t, 2)

    def window_start(bb, jj, e):
        s = starts_ref[bb * N_EXPERTS + e, jj]
        return jnp.minimum(s // _BF16_ROWS * _BF16_ROWS, cap_t - _WIN)

    def window_copy(row, e, slot, n_rows):
        return pltpu.make_async_copy(y_hbm.at[pl.ds(row, n_rows), :], buf.at[slot, pl.ds(e * _WIN, n_rows), :],
                                     sem.at[slot])

    def fetch(bb, jj, slot):
        for e in range(N_EXPERTS):
            row = (e * nb + bb) * cap_t + window_start(bb, jj, e)
            window_copy(pl.multiple_of(row, _BF16_ROWS), e, slot, _WIN).start()

    @pl.when(t == 0)
    def _():
        fetch(0, 0, 0)

    @pl.when(t + 1 < nb * n_tiles)
    def _():
        t1 = t + 1
        fetch(t1 // n_tiles, lax.rem(t1, n_tiles), 1 - cur)

    window_copy(0, 0, cur, N_EXPERTS * _WIN).wait()

    tab = tab_ref[...]
    lane_f = lax.broadcasted_iota(jnp.int32, (LANES, LANES), 1).astype(F32)
    col_of = []
    for e in range(N_EXPERTS):
        slot_e = tab[:, e:e + 1]
        rel = slot_e - window_start(b, j, e).astype(F32) + float(e * _WIN)
        col_of.append(jnp.broadcast_to(jnp.where(slot_e >= 0.0, rel, -1.0), (LANES, LANES)))
    tiles = []
    for i in range(N_EXPERTS * _WIN // LANES):
        cols = lane_f + float(i * LANES)
        e_lo, e_hi = i * LANES // _WIN, (i * LANES + LANES - 1) // _WIN
        hit = col_of[e_lo] == cols
        if e_hi != e_lo:
            hit = hit | (col_of[e_hi] == cols)
        tiles.append(jnp.where(hit, 1.0, 0.0).astype(BF16))
    moe = _dot(jnp.concatenate(tiles, axis=1), buf[cur])

    is_lat = (j * LANES + lax.broadcasted_iota(jnp.int32, (LANES, 1), 0)) < n_latent

    def mod(ref, kk):
        return jnp.where(is_lat, ref[1, kk:kk + 1, :], ref[0, kk:kk + 1, :])

    x2 = x_ref[...] + mod(mod_ref, 5) * (_rms(moe) * g_ref[3:4, :])
    xo_ref[...] = x2
    if has_next:
        h_ref[...] = (_rms(x2) * gn_ref[0:1, :] * (1.0 + mod(modn_ref, 1)) + mod(modn_ref, 0)).astype(h_ref.dtype)


def _combine(starts, tab, y, x, mod_tab, g4, mod_next, g4_next, *, n_latent):
    b, sx, d = x.shape
    cap_t = y.shape[0] // (b * N_EXPERTS)
    has_next = mod_next is not None
    row = pl.BlockSpec((None, LANES, d), lambda bi, i, s: (bi, i, 0))
    mod_spec = pl.BlockSpec((None, 2, 6, d), lambda bi, i, s: (bi, 0, 0, 0))
    g_spec = pl.BlockSpec((4, d), lambda bi, i, s: (0, 0))
    in_specs = [pl.BlockSpec((None, LANES, LANES), lambda bi, i, s: (bi, i, 0)),
                pl.BlockSpec(memory_space=pl.ANY), row, mod_spec, g_spec]
    ins = [tab, y, x, mod_tab, g4]
    out_specs, out_shape = [row], [jax.ShapeDtypeStruct((b, sx, d), F32)]
    if has_next:
        in_specs += [mod_spec, g_spec]
        ins += [mod_next, g4_next]
        out_specs.append(row)
        out_shape.append(jax.ShapeDtypeStruct((b, sx, d), BF16))
    grid_spec = pltpu.PrefetchScalarGridSpec(
        num_scalar_prefetch=1, grid=(b, sx // LANES), in_specs=in_specs, out_specs=out_specs,
        scratch_shapes=[pltpu.VMEM((2, N_EXPERTS * _WIN, d), BF16), pltpu.SemaphoreType.DMA((2,))])
    return pl.pallas_call(
        functools.partial(_combine_kernel, cap_t=cap_t, n_latent=n_latent, has_next=has_next),
        grid_spec=grid_spec, out_shape=out_shape,
        compiler_params=_params(("arbitrary", "arbitrary"), 48), name="moe_combine",
    )(starts, *ins)


def _prefix_count(mask_tiles, tri):
    off = jnp.zeros((N_EXPERTS, 1), F32)
    out, before = [], []
    for mt in mask_tiles:
        before.append(off)
        w = _dot(jnp.where(mt, 1.0, 0.0).astype(BF16), tri) + off
        out.append(w)
        off = w[:, LANES - 1:LANES]
    return out, before


_MIN_EXP = -160.0
_EXP_STEPS = 8
_VALUE_STEPS = 40


def _route_kernel(aff_ref, tab_ref, idx_ref, gate_ref, starts_ref, slot_sc, a_sc, *, segments, rows_per_sample):
    row0 = (pl.program_id(0) * rows_per_sample).astype(F32)
    r_i = lax.broadcasted_iota(jnp.int32, (LANES, LANES), 0)
    c_i = lax.broadcasted_iota(jnp.int32, (LANES, LANES), 1)
    tri = jnp.where(r_i <= c_i, 1.0, 0.0).astype(BF16)
    lane1 = lax.broadcasted_iota(jnp.int32, (1, LANES), 1)
    shape = (N_EXPERTS, LANES)
    lane_e = lax.broadcasted_iota(jnp.int32, shape, 1)
    starts = jnp.zeros(shape, F32)
    base = 0
    tile0 = 0
    for t0, n, cap in segments:
        nt = n // LANES
        a_t = [aff_ref[:, t0 + j * LANES:t0 + (j + 1) * LANES] for j in range(nt)]

        def count_ge(v, a_t=a_t):
            v_b = jnp.broadcast_to(v, shape)
            cnt = functools.reduce(jnp.add, [jnp.where(t >= v_b, 1.0, 0.0) for t in a_t])
            return jnp.sum(cnt, axis=1, keepdims=True)

        e_lo = jnp.full((N_EXPERTS, 1), _MIN_EXP, F32)
        e_hi = jnp.full((N_EXPERTS, 1), 1.0, F32)
        lo = jnp.zeros((N_EXPERTS, 1), F32)
        hi = jnp.full((N_EXPERTS, 1), 2.0, F32)
        for _ in range(_EXP_STEPS):
            e_mid = jnp.floor((e_lo + e_hi) * 0.5)
            v = jnp.exp2(e_mid)
            ok = count_ge(v) >= cap
            e_lo, lo = jnp.where(ok, e_mid, e_lo), jnp.where(ok, v, lo)
            e_hi, hi = jnp.where(ok, e_hi, e_mid), jnp.where(ok, hi, v)
        for _ in range(_VALUE_STEPS):
            v = (lo + hi) * 0.5
            ok = count_ge(v) >= cap
            lo, hi = jnp.where(ok, v, lo), jnp.where(ok, hi, v)
        lo_b = jnp.broadcast_to(lo, shape)
        hi_b = jnp.broadcast_to(hi, shape)
        gt = [t >= hi_b for t in a_t]
        eq = [(t >= lo_b) & (t < hi_b) for t in a_t]
        need = jnp.broadcast_to(cap - count_ge(hi), shape)
        eq_rank, _ = _prefix_count(eq, tri)
        sel = [g | (q & (r <= need)) for g, q, r in zip(gt, eq, eq_rank)]
        pos, before = _prefix_count(sel, tri)
        for j in range(nt):
            starts = jnp.where(lane_e == tile0 + j, before[j] + float(base), starts)
            slot = jnp.where(sel[j], pos[j] + (base - 1.0), -1.0)
            slot_sc[tile0 + j] = slot
            a_sc[tile0 + j] = a_t[j]
            padded = jnp.concatenate([slot, jnp.full((LANES - N_EXPERTS, LANES), -1.0, F32)], axis=0)
            tab_ref[t0 + j * LANES:t0 + (j + 1) * LANES, :] = padded.T

        def expert_body(e, carry, t0=t0, nt=nt, cap=cap, base=base, tile0=tile0):
            for c in range(-(-cap // LANES)):
                slot_id = (r_i + (base + c * LANES)).astype(F32)

                def tile_body(j, acc):
                    pe = jnp.broadcast_to(slot_sc[tile0 + j, pl.ds(e, 1), :], (LANES, LANES))
                    ae = jnp.broadcast_to(a_sc[tile0 + j, pl.ds(e, 1), :], (LANES, LANES))
                    tok = jnp.broadcast_to((lane1 + (t0 + j * LANES)).astype(F32) + row0, (LANES, LANES))
                    hit = pe == slot_id
                    return acc[0] + jnp.where(hit, tok, 0.0), acc[1] + jnp.where(hit, ae, 0.0)

                zero = jnp.zeros((LANES, LANES), F32)
                acc_i, acc_g = lax.fori_loop(0, nt, tile_body, (zero, zero))
                w = min(LANES, cap - c * LANES)
                lo = base + c * LANES
                col_i = jnp.broadcast_to(jnp.sum(acc_i, axis=1, keepdims=True), (LANES, LANES))
                col_g = jnp.broadcast_to(jnp.sum(acc_g, axis=1, keepdims=True), (LANES, LANES))
                idx_ref[e, :, lo:lo + w] = col_i.T[0:1, 0:w].astype(jnp.int32)
                gate_ref[e, :, lo:lo + w] = col_g.T[0:1, 0:w]
            return carry

        lax.fori_loop(0, N_EXPERTS, expert_body, 0)
        base += cap
        tile0 += nt
    starts_ref[...] = starts


def _route(aff_t, segments, rows_per_sample):
    b, e, sy = aff_t.shape
    k = sum(cap for _, _, cap in segments)
    n_tiles = sum(n // LANES for _, n, _ in segments)
    return pl.pallas_call(
        functools.partial(_route_kernel, segments=segments, rows_per_sample=rows_per_sample),
        grid=(b,),
        in_specs=[pl.BlockSpec((None, e, sy), lambda bi: (bi, 0, 0))],
        out_specs=[pl.BlockSpec((None, sy, LANES), lambda bi: (bi, 0, 0)),
                   pl.BlockSpec((None, e, 1, k), lambda bi: (bi, 0, 0, 0)),
                   pl.BlockSpec((None, e, 1, k), lambda bi: (bi, 0, 0, 0)),
                   pl.BlockSpec((None, e, LANES), lambda bi: (bi, 0, 0))],
        out_shape=[jax.ShapeDtypeStruct((b, sy, LANES), F32), jax.ShapeDtypeStruct((b, e, 1, k), jnp.int32),
                   jax.ShapeDtypeStruct((b, e, 1, k), F32), jax.ShapeDtypeStruct((b, e, LANES), F32)],
        scratch_shapes=[pltpu.VMEM((n_tiles, e, LANES), F32), pltpu.VMEM((n_tiles, e, LANES), F32)],
        compiler_params=_params(("parallel",), 32),
        name="route_topk",
    )(aff_t)


def _moe_block(x, hf, aff_t, segments, w_g, w_u, w_d, layer, mod_tab, g4, mod_next, g4_next, *, n_latent):
    b, sy = aff_t.shape[0], aff_t.shape[2]
    tab, idx, gates, starts = _route(aff_t, segments, sy)
    starts = starts.astype(jnp.int32).reshape(b * N_EXPERTS, LANES)
    k = idx.shape[-1]
    idx_em = jnp.swapaxes(idx[:, :, 0, :], 0, 1).reshape(N_EXPERTS, b * k)
    gates_em = jnp.swapaxes(gates[:, :, 0, :], 0, 1).reshape(N_EXPERTS, b * k, 1)
    y = _moe(idx_em, hf.reshape(b * sy, hf.shape[-1]), w_g, w_u, w_d, layer, gates_em)
    y2d = y.reshape(N_EXPERTS * b * k, y.shape[-1])
    return _combine(starts, tab, y2d, x, mod_tab, g4, mod_next, g4_next, n_latent=n_latent)


_SCALE = HEAD_DIM ** -0.5 * LOG2E
_PLAN_AB = tuple([(None, _SCALE)] * 8 + [(None, 1.0)] * 2 + [(None, None)] * 2
                 + [("q", _SCALE)] * 8 + [("k", 1.0)] * 2 + [(None, None)] * 2)
_PLAN_C = tuple([(None, _SCALE)] * 16 + [(None, 1.0)] * 16 + [(None, None)] * 16)


def kernel(x, c, ctx, c_ctx, mod_w, mod_b, norm_g, ab_w_in, ab_w_out, ab_sink, ab_q_norm, ab_k_norm,
           dif_w_in, dif_w_out, dif_lambda, dif_subln, router_w, exp_w_gate, exp_w_up, exp_w_down):
    b = x.shape[0]
    d = D_MODEL
    cv = jnp.concatenate([c, c_ctx[None, :], jnp.zeros((8 - b - 1, d), F32)], axis=0)
    mods = _mods(cv, mod_w, mod_b).reshape(DEPTH, 8, 6, d)
    mod_tab = jnp.stack([jnp.broadcast_to(mods[:, b][:, None], (DEPTH, b, 6, d)), mods[:, :b]], axis=2)
    tables = _rope_tables()
    ones = jnp.ones((1, LANES), F32)
    wr = jnp.pad(router_w, ((0, 0), (0, 0), (0, LANES - N_EXPERTS))).astype(BF16)

    xs = jnp.concatenate([x, ctx], axis=1)

    h0 = _norm_mod(xs, norm_g[0], mod_tab[0])
    p0 = _matmul(h0.reshape(b * S_ROWS, d), _cast_bf16(ab_w_in[0]), BF16).reshape(b, S_ROWS, AB_IN)
    qkv0 = _prep(p0, tables, ab_q_norm[0][None, :], ab_k_norm[0][None, :], _PLAN_AB)
    sink = ab_sink[0]
    common_a = dict(q_col=0, k_col=8, v_col=10, groups=4, kv_heads=A_KV_HEADS)
    common_b = dict(q_col=12, k_col=20, v_col=22, groups=4, kv_heads=B_KV_HEADS)
    ctx_q = dict(q_row0=SEQ // CTX_LEN, n_q=1, tq=CTX_LEN, band=False)
    ya_l = _attn_oneshot(sink, qkv0, **common_a, q_row0=0, n_q=SEQ // 256, tq=256, band=True, use_sink=True)
    ya_c = _attn_oneshot(sink, qkv0, **common_a, **ctx_q, use_sink=True)
    yb_c = _attn_oneshot(sink, qkv0, **common_b, **ctx_q, use_sink=False)
    yb_l = _attn_flash(jnp.zeros((4, LANES), F32), ones, qkv0, q_col=3, k_col=20, v_col=22, streams=4,
                       shared_k=True, dv=HEAD_DIM, heads=B_KV_HEADS, tq=256, tk=512, diff_scale=None)
    y0 = jnp.concatenate([jnp.concatenate([ya_l, yb_l], axis=-1), jnp.concatenate([ya_c, yb_c], axis=-1)], axis=1)
    x1, hf0, aff0 = _outproj(y0, _cast_bf16(ab_w_out[0]), xs, mod_tab[0], norm_g[0], wr[0], latent_only=False)
    cap_l = EC_FACTOR * SEQ // N_EXPERTS
    cap_c = EC_FACTOR * CTX_LEN // N_EXPERTS
    x2, h1 = _moe_block(x1, hf0, aff0, ((0, SEQ, cap_l), (SEQ, CTX_LEN, cap_c)),
                        exp_w_gate, exp_w_up, exp_w_down, 0, mod_tab[0], norm_g[0], mod_tab[1], norm_g[1],
                        n_latent=SEQ)

    p1 = _matmul(h1.reshape(b * S_ROWS, d), _cast_bf16(dif_w_in[0]), BF16).reshape(b, S_ROWS, C_IN)
    qkv1 = _prep(p1, tables, ones, ones, _PLAN_C)
    lambda_init = 0.8 - 0.6 * math.exp(-0.3 * 1)
    y1 = _attn_flash(dif_lambda[0], dif_subln[0][None, :], qkv1, q_col=0, k_col=8, v_col=16, streams=2,
                     shared_k=False, dv=C_V_DIM, heads=C_HEADS, tq=512, tk=512,
                     diff_scale=(lambda_init, 1.0 - lambda_init))
    x3, hf1, aff1 = _outproj(y1, _cast_bf16(dif_w_out[0]), x2, mod_tab[1], norm_g[1], wr[1], latent_only=True)
    (out,) = _moe_block(x3, hf1, aff1, ((0, SEQ, cap_l),), exp_w_gate, exp_w_up, exp_w_down, 1,
                        mod_tab[1], norm_g[1], None, None, n_latent=SEQ)
    return out
```

```python
import functools
import math

import jax
import jax.numpy as jnp
from jax import lax
from jax.experimental import pallas as pl
from jax.experimental.pallas import tpu as pltpu

D_MODEL = 2048
SEQ = 4096
DEPTH = 2
GRID_W = 64
CTX_LEN = 256
HEAD_DIM = 128
ROPE_BASE = 10000.0
BLOCK = 128
WINDOW = 128
A_Q_HEADS = 8
A_KV_HEADS = 2
B_Q_HEADS = 8
B_KV_HEADS = 2
AB_IN = 3072
C_HEADS = 8
C_V_DIM = 256
C_IN = 6144
N_EXPERTS = 16
EC_FACTOR = 2
EPS = 1e-6
NEG_INF = -1e30

S_ROWS = SEQ + CTX_LEN
ROW_TILE = 256
LANES = 128
MIB = 1024 * 1024
LOG2E = math.log2(math.e)

F32 = jnp.float32
BF16 = jnp.bfloat16


def _params(sem, vmem_mib):
    return pltpu.CompilerParams(dimension_semantics=sem, vmem_limit_bytes=vmem_mib * MIB)


def _rms(x):
    return x * lax.rsqrt(jnp.mean(x * x, axis=-1, keepdims=True) + EPS)


def _dot(a, b):
    return jnp.dot(a, b, preferred_element_type=F32)


def _dot_nt(a, b):
    return lax.dot_general(a, b, (((1,), (1,)), ((), ())), preferred_element_type=F32)


def _mods_kernel(cv_ref, w_ref, b_ref, o_ref):
    cv = cv_ref[...]
    s = cv * jax.nn.sigmoid(cv)
    o_ref[...] = _dot(s.astype(BF16), w_ref[...].astype(BF16)) + b_ref[...]


def _mods(cv, mod_w, mod_b):
    tn = 1024
    n = mod_w.shape[-1]
    return pl.pallas_call(
        _mods_kernel,
        grid=(DEPTH, n // tn),
        in_specs=[
            pl.BlockSpec((8, D_MODEL), lambda l, j: (0, 0)),
            pl.BlockSpec((None, D_MODEL, tn), lambda l, j: (l, 0, j)),
            pl.BlockSpec((None, 1, tn), lambda l, j: (l, 0, j)),
        ],
        out_specs=pl.BlockSpec((None, 8, tn), lambda l, j: (l, 0, j)),
        out_shape=jax.ShapeDtypeStruct((DEPTH, 8, n), F32),
        compiler_params=_params(("parallel", "parallel"), 40),
        name="mods",
    )(cv, mod_w, mod_b.reshape(DEPTH, 1, n))


def _cast_kernel(x_ref, o_ref):
    o_ref[...] = x_ref[...].astype(o_ref.dtype)


def _cast_bf16(w):
    k, n = w.shape
    tk = 512
    return pl.pallas_call(
        _cast_kernel,
        grid=(k // tk,),
        in_specs=[pl.BlockSpec((tk, n), lambda i: (i, 0))],
        out_specs=pl.BlockSpec((tk, n), lambda i: (i, 0)),
        out_shape=jax.ShapeDtypeStruct((k, n), BF16),
        compiler_params=_params(("parallel",), 48),
        name="cast_bf16",
    )(w)


def _seg(i):
    return jnp.where(i < SEQ // ROW_TILE, 1, 0)


def _norm_mod_kernel(x_ref, g_ref, mod_ref, h_ref):
    m = mod_ref[...]
    h = _rms(x_ref[...]) * g_ref[0:1, :] * (1.0 + m[1:2, :]) + m[0:1, :]
    h_ref[...] = h.astype(h_ref.dtype)


def _norm_mod(x, g4, mod_tab):
    b, s, d = x.shape
    return pl.pallas_call(
        _norm_mod_kernel,
        grid=(b, s // ROW_TILE),
        in_specs=[
            pl.BlockSpec((None, ROW_TILE, d), lambda bi, i: (bi, i, 0)),
            pl.BlockSpec((4, d), lambda bi, i: (0, 0)),
            pl.BlockSpec((None, None, 6, d), lambda bi, i: (bi, _seg(i), 0, 0)),
        ],
        out_specs=pl.BlockSpec((None, ROW_TILE, d), lambda bi, i: (bi, i, 0)),
        out_shape=jax.ShapeDtypeStruct((b, s, d), BF16),
        compiler_params=_params(("parallel", "parallel"), 32),
        name="norm_mod",
    )(x, g4, mod_tab)


def _mm_kernel(a_ref, w_ref, o_ref):
    o_ref[...] = _dot(a_ref[...], w_ref[...]).astype(o_ref.dtype)


def _matmul(a, w, out_dtype):
    m, k = a.shape
    n = w.shape[1]
    tm, tn = 1088, 1024
    return pl.pallas_call(
        _mm_kernel,
        grid=(n // tn, m // tm),
        in_specs=[
            pl.BlockSpec((tm, k), lambda j, i: (i, 0)),
            pl.BlockSpec((k, tn), lambda j, i: (0, j)),
        ],
        out_specs=pl.BlockSpec((tm, tn), lambda j, i: (i, j)),
        out_shape=jax.ShapeDtypeStruct((m, n), out_dtype),
        compiler_params=_params(("parallel", "parallel"), 48),
        name="in_proj",
    )(a, w)


def _rope_tables():
    n_rows = SEQ // GRID_W
    row = jnp.repeat(jnp.arange(n_rows), GRID_W).astype(F32)
    col = jnp.tile(jnp.arange(GRID_W), n_rows).astype(F32)
    n_freq = HEAD_DIM // 4
    inv = ROPE_BASE ** (-jnp.arange(n_freq, dtype=F32) / n_freq)
    ar = row[:, None] * inv
    ac = col[:, None] * inv
    cr, sr, cc, sc = jnp.cos(ar), jnp.sin(ar), jnp.cos(ac), jnp.sin(ac)
    z = jnp.zeros_like(sr)
    c = jnp.concatenate([cr, cr, cc, cc], axis=-1)
    sa = jnp.concatenate([-sr, z, -sc, z], axis=-1)
    sb = jnp.concatenate([z, sr, z, sc], axis=-1)
    pad = jnp.zeros((CTX_LEN, HEAD_DIM), F32)
    return (jnp.concatenate([c, pad + 1.0], axis=0), jnp.concatenate([sa, pad], axis=0),
            jnp.concatenate([sb, pad], axis=0))


def _prep_kernel(p_ref, c_ref, sa_ref, sb_ref, gq_ref, gk_ref, o_ref, *, plan):
    c, sa, sb = c_ref[...], sa_ref[...], sb_ref[...]
    for s, (norm, scale) in enumerate(plan):
        sl = slice(s * LANES, (s + 1) * LANES)
        if scale is None:
            o_ref[:, sl] = p_ref[:, sl]
            continue
        x = p_ref[:, sl].astype(F32)
        if norm is not None:
            x = _rms(x) * (gq_ref[...] if norm == "q" else gk_ref[...])
        r = x * c + pltpu.roll(x, 96, 1) * sa + pltpu.roll(x, 32, 1) * sb
        if scale != 1.0:
            r = r * scale
        o_ref[:, sl] = r.astype(o_ref.dtype)


def _prep(p, tables, gq, gk, plan):
    b, s, n = p.shape
    tab_spec = pl.BlockSpec((ROW_TILE, LANES), lambda bi, i: (i, 0))
    g_spec = pl.BlockSpec((1, LANES), lambda bi, i: (0, 0))
    return pl.pallas_call(
        functools.partial(_prep_kernel, plan=plan),
        grid=(b, s // ROW_TILE),
        in_specs=[pl.BlockSpec((None, ROW_TILE, n), lambda bi, i: (bi, i, 0)),
                  tab_spec, tab_spec, tab_spec, g_spec, g_spec],
        out_specs=pl.BlockSpec((None, ROW_TILE, n), lambda bi, i: (bi, i, 0)),
        out_shape=jax.ShapeDtypeStruct((b, s, n), BF16),
        compiler_params=_params(("parallel", "parallel"), 32),
        name="qk_prep",
    )(p, *tables, gq, gk)


def _lane_tiles(s):
    return [s[:, j * LANES:(j + 1) * LANES] for j in range(s.shape[1] // LANES)]


def _stack_heads(q_ref, groups):
    return jnp.concatenate([q_ref[:, g * LANES:(g + 1) * LANES] for g in range(groups)], axis=0)


def _attn_oneshot_kernel(sink_ref, q_ref, k_ref, v_ref, o_ref, *, groups, tq, band, use_sink):
    h = pl.program_id(1)
    qi = pl.program_id(2)
    rows = groups * tq
    ctx0 = SEQ if band else 0
    q = _stack_heads(q_ref, groups)
    tiles = _lane_tiles(_dot_nt(q, k_ref[ctx0:ctx0 + CTX_LEN, :]))
    n_ctx_tiles = len(tiles)
    if band:
        nband = tq + 2 * BLOCK
        start = jnp.clip(qi * tq - BLOCK, 0, SEQ - nband)
        start = pl.multiple_of(start, BLOCK)
        row = lax.broadcasted_iota(jnp.int32, (rows, nband), 0)
        q_pos = qi * tq + (row & (tq - 1))
        k_pos = start + lax.broadcasted_iota(jnp.int32, (rows, nband), 1)
        valid = jnp.abs(q_pos - k_pos) <= WINDOW
        tiles += _lane_tiles(jnp.where(valid, _dot_nt(q, k_ref[pl.ds(start, nband), :]), NEG_INF))
    m = jnp.max(functools.reduce(jnp.maximum, tiles), axis=-1, keepdims=True)
    if use_sink:
        row1 = lax.broadcasted_iota(jnp.int32, (rows, 1), 0)
        sink = jnp.zeros((rows, 1), F32)
        for g in range(groups):
            sink = jnp.where(row1 >= g * tq, sink_ref[h * groups + g] * LOG2E, sink)
        m = jnp.maximum(m, sink)
    m_b = jnp.broadcast_to(m, (rows, LANES))
    ps = [jnp.exp2(t - m_b) for t in tiles]
    l = jnp.sum(functools.reduce(jnp.add, ps), axis=-1, keepdims=True)
    if use_sink:
        l = l + jnp.exp2(sink - m)
    p_c = jnp.concatenate([t.astype(BF16) for t in ps[:n_ctx_tiles]], axis=1)
    o = _dot(p_c, v_ref[ctx0:ctx0 + CTX_LEN, :])
    if band:
        p_b = jnp.concatenate([t.astype(BF16) for t in ps[n_ctx_tiles:]], axis=1)
        o = o + _dot(p_b, v_ref[pl.ds(start, nband), :])
    o = o / l
    for g in range(groups):
        o_ref[:, g * LANES:(g + 1) * LANES] = o[g * tq:(g + 1) * tq, :].astype(o_ref.dtype)


def _attn_oneshot(sink, qkv, *, q_col, k_col, v_col, groups, kv_heads, q_row0, n_q, tq, band, use_sink):
    b = qkv.shape[0]
    k_rows = S_ROWS if band else CTX_LEN
    k_row0 = 0 if band else SEQ // CTX_LEN
    gw = groups * LANES
    return pl.pallas_call(
        functools.partial(_attn_oneshot_kernel, groups=groups, tq=tq, band=band, use_sink=use_sink),
        grid=(b, kv_heads, n_q),
        in_specs=[
            pl.BlockSpec(memory_space=pltpu.SMEM),
            pl.BlockSpec((None, tq, gw), lambda bi, h, i: (bi, i + q_row0, q_col // groups + h)),
            pl.BlockSpec((None, k_rows, LANES), lambda bi, h, i: (bi, k_row0, k_col + h)),
            pl.BlockSpec((None, k_rows, LANES), lambda bi, h, i: (bi, k_row0, v_col + h)),
        ],
        out_specs=pl.BlockSpec((None, tq, gw), lambda bi, h, i: (bi, i, h)),
        out_shape=jax.ShapeDtypeStruct((b, n_q * tq, kv_heads * gw), BF16),
        compiler_params=_params(("parallel", "parallel", "parallel"), 32),
        name="attn_oneshot",
    )(sink, qkv, qkv, qkv)


def _attn_flash_kernel(lam_ref, subg_ref, q_ref, k_ref, v_ref, o_ref, q_sc, m_sc, l_sc, acc_sc,
                       *, streams, shared_k, tq, tk, diff_scale):
    dv = acc_sc.shape[1]
    for g in range(streams):
        q_sc[g * tq:(g + 1) * tq, :] = q_ref[:, g * LANES:(g + 1) * LANES]
    m_sc[...] = jnp.full(m_sc.shape, NEG_INF, F32)
    l_sc[...] = jnp.zeros(l_sc.shape, F32)
    acc_sc[...] = jnp.zeros(acc_sc.shape, F32)

    def chunk(off, size):
        kc = k_ref[pl.ds(off, size), :]
        if shared_k:
            s = _dot_nt(q_sc[...], kc)
        else:
            s = jnp.concatenate([_dot_nt(q_sc[g * tq:(g + 1) * tq, :], kc[:, g * LANES:(g + 1) * LANES])
                                 for g in range(streams)], axis=0)
        tiles = _lane_tiles(s)
        m_prev = m_sc[...]
        m_new = jnp.maximum(m_prev, jnp.max(functools.reduce(jnp.maximum, tiles), axis=-1, keepdims=True))
        alpha = jnp.exp2(m_prev - m_new)
        ps = [jnp.exp2(t - m_new) for t in tiles]
        m_sc[...] = m_new
        l_sc[...] = alpha * l_sc[...] + functools.reduce(jnp.add, ps)
        p = jnp.concatenate([t.astype(BF16) for t in ps], axis=1)
        pv = _dot(p, v_ref[pl.ds(off, size), :])
        acc_sc[...] = jnp.concatenate([alpha] * (dv // LANES), axis=1) * acc_sc[...] + pv

    chunk(SEQ, CTX_LEN)

    def body(c, carry):
        chunk(pl.multiple_of(c * tk, tk), tk)
        return carry

    lax.fori_loop(0, SEQ // tk, body, 0)
    o = acc_sc[...] / jnp.sum(l_sc[...], axis=-1, keepdims=True)
    if diff_scale is None:
        for g in range(streams):
            o_ref[:, g * LANES:(g + 1) * LANES] = o[g * tq:(g + 1) * tq, :].astype(o_ref.dtype)
    else:
        lambda_init, out_scale = diff_scale
        lp = lam_ref[...]
        lam = (jnp.exp(jnp.sum(lp[0:1, :] * lp[1:2, :], axis=-1, keepdims=True))
               - jnp.exp(jnp.sum(lp[2:3, :] * lp[3:4, :], axis=-1, keepdims=True)) + lambda_init)
        o = o[0:tq, :] - lam * o[tq:2 * tq, :]
        o_ref[...] = (_rms(o) * subg_ref[...] * out_scale).astype(o_ref.dtype)


def _attn_flash(lam, subg, qkv, *, q_col, k_col, v_col, streams, shared_k, dv, heads, tq, tk, diff_scale):
    b = qkv.shape[0]
    qw = streams * LANES
    kw = LANES if shared_k else streams * LANES
    n_q = SEQ // tq
    ow = dv if diff_scale is not None else qw
    rows = streams * tq
    return pl.pallas_call(
        functools.partial(_attn_flash_kernel, streams=streams, shared_k=shared_k, tq=tq, tk=tk,
                          diff_scale=diff_scale),
        grid=(b, heads, n_q),
        in_specs=[
            pl.BlockSpec((4, LANES), lambda bi, h, i: (0, 0)),
            pl.BlockSpec((1, dv), lambda bi, h, i: (0, 0)),
            pl.BlockSpec((None, tq, qw), lambda bi, h, i: (bi, i, q_col + h)),
            pl.BlockSpec((None, S_ROWS, kw), lambda bi, h, i: (bi, 0, k_col + h)),
            pl.BlockSpec((None, S_ROWS, dv), lambda bi, h, i: (bi, 0, v_col + h)),
        ],
        out_specs=pl.BlockSpec((None, tq, ow), lambda bi, h, i: (bi, i, h)),
        out_shape=jax.ShapeDtypeStruct((b, SEQ, heads * ow), BF16),
        scratch_shapes=[pltpu.VMEM((rows, LANES), BF16), pltpu.VMEM((rows, LANES), F32),
                        pltpu.VMEM((rows, LANES), F32), pltpu.VMEM((rows, dv), F32)],
        compiler_params=_params(("parallel", "parallel", "parallel"), 48),
        name="attn_flash",
    )(lam, subg, qkv, qkv, qkv)


def _outproj_kernel(y_ref, w_ref, x_ref, mod_ref, g_ref, wr_ref, xo_ref, hf_ref, aff_ref):
    m = mod_ref[...]
    g = g_ref[...]
    y = _dot(y_ref[...], w_ref[...])
    x1 = x_ref[...] + m[2:3, :] * (_rms(y) * g[1:2, :])
    xo_ref[...] = x1
    hf = _rms(x1) * g[2:3, :] * (1.0 + m[4:5, :]) + m[3:4, :]
    hf_ref[...] = hf
    logits = _dot(hf.astype(BF16), wr_ref[...]).T[0:N_EXPERTS, :]
    e = jnp.exp(logits - jnp.max(logits, axis=0, keepdims=True))
    aff_ref[...] = e / jnp.sum(e, axis=0, keepdims=True)


def _outproj(y, w_bf, x, mod_tab, g4, wr_bf, *, latent_only):
    b, sy, d = y.shape
    seg = (lambda i: 1) if latent_only else _seg
    row = lambda bi, i: (bi, i, 0)
    out_f32 = jax.ShapeDtypeStruct((b, sy, d), F32)
    return pl.pallas_call(
        _outproj_kernel,
        grid=(b, sy // ROW_TILE),
        in_specs=[
            pl.BlockSpec((None, ROW_TILE, d), row),
            pl.BlockSpec((d, d), lambda bi, i: (0, 0)),
            pl.BlockSpec((None, ROW_TILE, d), row),
            pl.BlockSpec((None, None, 6, d), lambda bi, i: (bi, seg(i), 0, 0)),
            pl.BlockSpec((4, d), lambda bi, i: (0, 0)),
            pl.BlockSpec((d, LANES), lambda bi, i: (0, 0)),
        ],
        out_specs=[pl.BlockSpec((None, ROW_TILE, d), row), pl.BlockSpec((None, ROW_TILE, d), row),
                   pl.BlockSpec((None, N_EXPERTS, ROW_TILE), lambda bi, i: (bi, 0, i))],
        out_shape=[out_f32, out_f32, jax.ShapeDtypeStruct((b, N_EXPERTS, sy), F32)],
        compiler_params=_params(("parallel", "parallel"), 48),
        name="out_proj",
    )(y, w_bf, x, mod_tab, g4, wr_bf)


def _row_copy(h_hbm, stage_sc, sem, src_row, dst_row, n_rows):
    return pltpu.make_async_copy(h_hbm.at[pl.ds(src_row, n_rows), :], stage_sc.at[pl.ds(dst_row, n_rows), :], sem)


def _moe_kernel(idx_ref, h_hbm, wg_ref, wu_ref, wd_ref, gate_ref, o_ref, stage_sc, x_sc, hmid_sc, sem, *, n_up, n_down):
    e = pl.program_id(0)
    s = pl.program_id(1)
    last = (e == pl.num_programs(0) - 1) & (s == n_up + n_down - 1)
    m = x_sc.shape[0]
    rows_per_step = m // n_up

    def issue_one(expert, row):
        _row_copy(h_hbm, stage_sc, sem, idx_ref[expert, row], row, 1).start()

    @pl.when(s == 0)
    def _():
        @pl.when(e == 0)
        def _():
            def body(i, carry):
                issue_one(0, i)
                return carry

            lax.fori_loop(0, m, body, 0, unroll=8)

        _row_copy(h_hbm, stage_sc, sem, 0, 0, m).wait()
        x_sc[...] = stage_sc[...].astype(BF16)

    @pl.when(s < n_up)
    def _():
        for i in range(rows_per_step):
            issue_one(e + 1, s * rows_per_step + i)
        x = x_sc[...]
        a = _dot(x, wg_ref[...].astype(BF16))
        u = _dot(x, wu_ref[...].astype(BF16))
        hmid_sc[s] = (a * jax.nn.sigmoid(a) * u).astype(BF16)

    @pl.when(s >= n_up)
    def _():
        hmid = jnp.concatenate([hmid_sc[f] for f in range(n_up)], axis=1)
        y = _dot(hmid, wd_ref[...].astype(BF16))
        o_ref[...] = (y * gate_ref[...]).astype(o_ref.dtype)

    @pl.when(last)
    def _():
        _row_copy(h_hbm, stage_sc, sem, 0, 0, m).wait()


def _moe(idx, h, w_g, w_u, w_d, layer, gates):
    e, m = idx.shape
    d = w_g.shape[2]
    f_dim = w_g.shape[-1]
    t_up, t_down = 256, 512
    n_up, n_down = f_dim // t_up, d // t_down
    idx = jnp.concatenate([idx, jnp.zeros((1, m), idx.dtype)], axis=0)
    up_tile = lambda ei, s, idx_ref: (layer, ei, 0, jnp.minimum(s, n_up - 1))
    down_tile = lambda s: jnp.maximum(s - n_up, 0)
    grid_spec = pltpu.PrefetchScalarGridSpec(
        num_scalar_prefetch=1,
        grid=(e, n_up + n_down),
        in_specs=[
            pl.BlockSpec(memory_space=pl.ANY),
            pl.BlockSpec((None, None, d, t_up), up_tile),
            pl.BlockSpec((None, None, d, t_up), up_tile),
            pl.BlockSpec((None, None, f_dim, t_down), lambda ei, s, idx_ref: (layer, ei, 0, down_tile(s))),
            pl.BlockSpec((None, m, 1), lambda ei, s, idx_ref: (ei, 0, 0)),
        ],
        out_specs=pl.BlockSpec((None, m, t_down), lambda ei, s, idx_ref: (ei, 0, down_tile(s))),
        scratch_shapes=[pltpu.VMEM((m, d), F32), pltpu.VMEM((m, d), BF16), pltpu.VMEM((n_up, m, t_up), BF16),
                        pltpu.SemaphoreType.DMA(())],
    )
    return pl.pallas_call(
        functools.partial(_moe_kernel, n_up=n_up, n_down=n_down),
        grid_spec=grid_spec,
        out_shape=jax.ShapeDtypeStruct((e, m, d), BF16),
        compiler_params=_params(("arbitrary", "arbitrary"), 58),
        name="moe_experts",
    )(idx, h, w_g, w_u, w_d, gates)


_BF16_ROWS = 16
_WIN = LANES + _BF16_ROWS


def _combine_kernel(starts_ref, tab_ref, y_hbm, x_ref, mod_ref, g_ref, *rest, cap_t, n_latent, has_next):
    if has_next:
        modn_ref, gn_ref, xo_ref, h_ref, buf, sem = rest
    else:
        xo_ref, buf, sem = rest
    b = pl.program_id(0)
    j = pl.program_id(1)
    nb = pl.num_programs(0)
    n_tiles = pl.num_programs(1)
    t = b * n_tiles + j
    cur = lax.rem(t, 2)

    def window_start(bb, jj, e):
        s = starts_ref[bb * N_EXPERTS + e, jj]
        return jnp.minimum(s // _BF16_ROWS * _BF16_ROWS, cap_t - _WIN)

    def window_copy(row, e, slot, n_rows):
        return pltpu.make_async_copy(y_hbm.at[pl.ds(row, n_rows), :], buf.at[slot, pl.ds(e * _WIN, n_rows), :],
                                     sem.at[slot])

    def fetch(bb, jj, slot):
        for e in range(N_EXPERTS):
            row = (e * nb + bb) * cap_t + window_start(bb, jj, e)
            window_copy(pl.multiple_of(row, _BF16_ROWS), e, slot, _WIN).start()

    @pl.when(t == 0)
    def _():
        fetch(0, 0, 0)

    @pl.when(t + 1 < nb * n_tiles)
    def _():
        t1 = t + 1
        fetch(t1 // n_tiles, lax.rem(t1, n_tiles), 1 - cur)

    window_copy(0, 0, cur, N_EXPERTS * _WIN).wait()

    tab = tab_ref[...]
    lane_f = lax.broadcasted_iota(jnp.int32, (LANES, LANES), 1).astype(F32)
    col_of = []
    for e in range(N_EXPERTS):
        slot_e = tab[:, e:e + 1]
        rel = slot_e - window_start(b, j, e).astype(F32) + float(e * _WIN)
        col_of.append(jnp.broadcast_to(jnp.where(slot_e >= 0.0, rel, -1.0), (LANES, LANES)))
    tiles = []
    for i in range(N_EXPERTS * _WIN // LANES):
        cols = lane_f + float(i * LANES)
        e_lo, e_hi = i * LANES // _WIN, (i * LANES + LANES - 1) // _WIN
        hit = col_of[e_lo] == cols
        if e_hi != e_lo:
            hit = hit | (col_of[e_hi] == cols)
        tiles.append(jnp.where(hit, 1.0, 0.0).astype(BF16))
    moe = _dot(jnp.concatenate(tiles, axis=1), buf[cur])

    is_lat = (j * LANES + lax.broadcasted_iota(jnp.int32, (LANES, 1), 0)) < n_latent

    def mod(ref, kk):
        return jnp.where(is_lat, ref[1, kk:kk + 1, :], ref[0, kk:kk + 1, :])

    x2 = x_ref[...] + mod(mod_ref, 5) * (_rms(moe) * g_ref[3:4, :])
    xo_ref[...] = x2
    if has_next:
        h_ref[...] = (_rms(x2) * gn_ref[0:1, :] * (1.0 + mod(modn_ref, 1)) + mod(modn_ref, 0)).astype(h_ref.dtype)


def _combine(starts, tab, y, x, mod_tab, g4, mod_next, g4_next, *, n_latent):
    b, sx, d = x.shape
    cap_t = y.shape[0] // (b * N_EXPERTS)
    has_next = mod_next is not None
    row = pl.BlockSpec((None, LANES, d), lambda bi, i, s: (bi, i, 0))
    mod_spec = pl.BlockSpec((None, 2, 6, d), lambda bi, i, s: (bi, 0, 0, 0))
    g_spec = pl.BlockSpec((4, d), lambda bi, i, s: (0, 0))
    in_specs = [pl.BlockSpec((None, LANES, LANES), lambda bi, i, s: (bi, i, 0)),
                pl.BlockSpec(memory_space=pl.ANY), row, mod_spec, g_spec]
    ins = [tab, y, x, mod_tab, g4]
    out_specs, out_shape = [row], [jax.ShapeDtypeStruct((b, sx, d), F32)]
    if has_next:
        in_specs += [mod_spec, g_spec]
        ins += [mod_next, g4_next]
        out_specs.append(row)
        out_shape.append(jax.ShapeDtypeStruct((b, sx, d), BF16))
    grid_spec = pltpu.PrefetchScalarGridSpec(
        num_scalar_prefetch=1, grid=(b, sx // LANES), in_specs=in_specs, out_specs=out_specs,
        scratch_shapes=[pltpu.VMEM((2, N_EXPERTS * _WIN, d), BF16), pltpu.SemaphoreType.DMA((2,))])
    return pl.pallas_call(
        functools.partial(_combine_kernel, cap_t=cap_t, n_latent=n_latent, has_next=has_next),
        grid_spec=grid_spec, out_shape=out_shape,
        compiler_params=_params(("arbitrary", "arbitrary"), 48), name="moe_combine",
    )(starts, *ins)


def _prefix_count(mask_tiles, tri):
    off = jnp.zeros((N_EXPERTS, 1), F32)
    out, before = [], []
    for mt in mask_tiles:
        before.append(off)
        w = _dot(jnp.where(mt, 1.0, 0.0).astype(BF16), tri) + off
        out.append(w)
        off = w[:, LANES - 1:LANES]
    return out, before


_MIN_EXP = -160.0
_EXP_STEPS = 8
_VALUE_STEPS = 40


def _route_kernel(aff_ref, tab_ref, idx_ref, gate_ref, starts_ref, slot_sc, a_sc, *, segments, rows_per_sample):
    row0 = (pl.program_id(0) * rows_per_sample).astype(F32)
    r_i = lax.broadcasted_iota(jnp.int32, (LANES, LANES), 0)
    c_i = lax.broadcasted_iota(jnp.int32, (LANES, LANES), 1)
    tri = jnp.where(r_i <= c_i, 1.0, 0.0).astype(BF16)
    lane1 = lax.broadcasted_iota(jnp.int32, (1, LANES), 1)
    shape = (N_EXPERTS, LANES)
    lane_e = lax.broadcasted_iota(jnp.int32, shape, 1)
    starts = jnp.zeros(shape, F32)
    base = 0
    tile0 = 0
    for t0, n, cap in segments:
        nt = n // LANES
        a_t = [aff_ref[:, t0 + j * LANES:t0 + (j + 1) * LANES] for j in range(nt)]

        def count_ge(v, a_t=a_t):
            v_b = jnp.broadcast_to(v, shape)
            cnt = functools.reduce(jnp.add, [jnp.where(t >= v_b, 1.0, 0.0) for t in a_t])
            return jnp.sum(cnt, axis=1, keepdims=True)

        e_lo = jnp.full((N_EXPERTS, 1), _MIN_EXP, F32)
        e_hi = jnp.full((N_EXPERTS, 1), 1.0, F32)
        lo = jnp.zeros((N_EXPERTS, 1), F32)
        hi = jnp.full((N_EXPERTS, 1), 2.0, F32)
        for _ in range(_EXP_STEPS):
            e_mid = jnp.floor((e_lo + e_hi) * 0.5)
            v = jnp.exp2(e_mid)
            ok = count_ge(v) >= cap
            e_lo, lo = jnp.where(ok, e_mid, e_lo), jnp.where(ok, v, lo)
            e_hi, hi = jnp.where(ok, e_hi, e_mid), jnp.where(ok, hi, v)
        for _ in range(_VALUE_STEPS):
            v = (lo + hi) * 0.5
            ok = count_ge(v) >= cap
            lo, hi = jnp.where(ok, v, lo), jnp.where(ok, hi, v)
        lo_b = jnp.broadcast_to(lo, shape)
        hi_b = jnp.broadcast_to(hi, shape)
        gt = [t >= hi_b for t in a_t]
        eq = [(t >= lo_b) & (t < hi_b) for t in a_t]
        need = jnp.broadcast_to(cap - count_ge(hi), shape)
        eq_rank, _ = _prefix_count(eq, tri)
        sel = [g | (q & (r <= need)) for g, q, r in zip(gt, eq, eq_rank)]
        pos, before = _prefix_count(sel, tri)
        for j in range(nt):
            starts = jnp.where(lane_e == tile0 + j, before[j] + float(base), starts)
            slot = jnp.where(sel[j], pos[j] + (base - 1.0), -1.0)
            slot_sc[tile0 + j] = slot
            a_sc[tile0 + j] = a_t[j]
            padded = jnp.concatenate([slot, jnp.full((LANES - N_EXPERTS, LANES), -1.0, F32)], axis=0)
            tab_ref[t0 + j * LANES:t0 + (j + 1) * LANES, :] = padded.T

        def expert_body(e, carry, t0=t0, nt=nt, cap=cap, base=base, tile0=tile0):
            for c in range(-(-cap // LANES)):
                slot_id = (r_i + (base + c * LANES)).astype(F32)

                def tile_body(j, acc):
                    pe = jnp.broadcast_to(slot_sc[tile0 + j, pl.ds(e, 1), :], (LANES, LANES))
                    ae = jnp.broadcast_to(a_sc[tile0 + j, pl.ds(e, 1), :], (LANES, LANES))
                    tok = jnp.broadcast_to((lane1 + (t0 + j * LANES)).astype(F32) + row0, (LANES, LANES))
                    hit = pe == slot_id
                    return acc[0] + jnp.where(hit, tok, 0.0), acc[1] + jnp.where(hit, ae, 0.0)

                zero = jnp.zeros((LANES, LANES), F32)
                acc_i, acc_g = lax.fori_loop(0, nt, tile_body, (zero, zero))
                w = min(LANES, cap - c * LANES)
                lo = base + c * LANES
                col_i = jnp.broadcast_to(jnp.sum(acc_i, axis=1, keepdims=True), (LANES, LANES))
                col_g = jnp.broadcast_to(jnp.sum(acc_g, axis=1, keepdims=True), (LANES, LANES))
                idx_ref[e, :, lo:lo + w] = col_i.T[0:1, 0:w].astype(jnp.int32)
                gate_ref[e, :, lo:lo + w] = col_g.T[0:1, 0:w]
            return carry

        lax.fori_loop(0, N_EXPERTS, expert_body, 0)
        base += cap
        tile0 += nt
    starts_ref[...] = starts


def _route(aff_t, segments, rows_per_sample):
    b, e, sy = aff_t.shape
    k = sum(cap for _, _, cap in segments)
    n_tiles = sum(n // LANES for _, n, _ in segments)
    return pl.pallas_call(
        functools.partial(_route_kernel, segments=segments, rows_per_sample=rows_per_sample),
        grid=(b,),
        in_specs=[pl.BlockSpec((None, e, sy), lambda bi: (bi, 0, 0))],
        out_specs=[pl.BlockSpec((None, sy, LANES), lambda bi: (bi, 0, 0)),
                   pl.BlockSpec((None, e, 1, k), lambda bi: (bi, 0, 0, 0)),
                   pl.BlockSpec((None, e, 1, k), lambda bi: (bi, 0, 0, 0)),
                   pl.BlockSpec((None, e, LANES), lambda bi: (bi, 0, 0))],
        out_shape=[jax.ShapeDtypeStruct((b, sy, LANES), F32), jax.ShapeDtypeStruct((b, e, 1, k), jnp.int32),
                   jax.ShapeDtypeStruct((b, e, 1, k), F32), jax.ShapeDtypeStruct((b, e, LANES), F32)],
        scratch_shapes=[pltpu.VMEM((n_tiles, e, LANES), F32), pltpu.VMEM((n_tiles, e, LANES), F32)],
        compiler_params=_params(("parallel",), 32),
        name="route_topk",
    )(aff_t)


def _moe_block(x, hf, aff_t, segments, w_g, w_u, w_d, layer, mod_tab, g4, mod_next, g4_next, *, n_latent):
    b, sy = aff_t.shape[0], aff_t.shape[2]
    tab, idx, gates, starts = _route(aff_t, segments, sy)
    starts = starts.astype(jnp.int32).reshape(b * N_EXPERTS, LANES)
    k = idx.shape[-1]
    idx_em = jnp.swapaxes(idx[:, :, 0, :], 0, 1).reshape(N_EXPERTS, b * k)
    gates_em = jnp.swapaxes(gates[:, :, 0, :], 0, 1).reshape(N_EXPERTS, b * k, 1)
    y = _moe(idx_em, hf.reshape(b * sy, hf.shape[-1]), w_g, w_u, w_d, layer, gates_em)
    y2d = y.reshape(N_EXPERTS * b * k, y.shape[-1])
    return _combine(starts, tab, y2d, x, mod_tab, g4, mod_next, g4_next, n_latent=n_latent)


_SCALE = HEAD_DIM ** -0.5 * LOG2E
_PLAN_AB = tuple([(None, _SCALE)] * 8 + [(None, 1.0)] * 2 + [(None, None)] * 2
                 + [("q", _SCALE)] * 8 + [("k", 1.0)] * 2 + [(None, None)] * 2)
_PLAN_C = tuple([(None, _SCALE)] * 16 + [(None, 1.0)] * 16 + [(None, None)] * 16)


def kernel(x, c, ctx, c_ctx, mod_w, mod_b, norm_g, ab_w_in, ab_w_out, ab_sink, ab_q_norm, ab_k_norm,
           dif_w_in, dif_w_out, dif_lambda, dif_subln, router_w, exp_w_gate, exp_w_up, exp_w_down):
    b = x.shape[0]
    d = D_MODEL
    cv = jnp.concatenate([c, c_ctx[None, :], jnp.zeros((8 - b - 1, d), F32)], axis=0)
    mods = _mods(cv, mod_w, mod_b).reshape(DEPTH, 8, 6, d)
    mod_tab = jnp.stack([jnp.broadcast_to(mods[:, b][:, None], (DEPTH, b, 6, d)), mods[:, :b]], axis=2)
    tables = _rope_tables()
    ones = jnp.ones((1, LANES), F32)
    wr = jnp.pad(router_w, ((0, 0), (0, 0), (0, LANES - N_EXPERTS))).astype(BF16)

    xs = jnp.concatenate([x, ctx], axis=1)

    h0 = _norm_mod(xs, norm_g[0], mod_tab[0])
    p0 = _matmul(h0.reshape(b * S_ROWS, d), _cast_bf16(ab_w_in[0]), BF16).reshape(b, S_ROWS, AB_IN)
    qkv0 = _prep(p0, tables, ab_q_norm[0][None, :], ab_k_norm[0][None, :], _PLAN_AB)
    sink = ab_sink[0]
    common_a = dict(q_col=0, k_col=8, v_col=10, groups=4, kv_heads=A_KV_HEADS)
    common_b = dict(q_col=12, k_col=20, v_col=22, groups=4, kv_heads=B_KV_HEADS)
    ctx_q = dict(q_row0=SEQ // CTX_LEN, n_q=1, tq=CTX_LEN, band=False)
    ya_l = _attn_oneshot(sink, qkv0, **common_a, q_row0=0, n_q=SEQ // 256, tq=256, band=True, use_sink=True)
    ya_c = _attn_oneshot(sink, qkv0, **common_a, **ctx_q, use_sink=True)
    yb_c = _attn_oneshot(sink, qkv0, **common_b, **ctx_q, use_sink=False)
    yb_l = _attn_flash(jnp.zeros((4, LANES), F32), ones, qkv0, q_col=3, k_col=20, v_col=22, streams=4,
                       shared_k=True, dv=HEAD_DIM, heads=B_KV_HEADS, tq=256, tk=2048, diff_scale=None)
    y0 = jnp.concatenate([jnp.concatenate([ya_l, yb_l], axis=-1), jnp.concatenate([ya_c, yb_c], axis=-1)], axis=1)
    x1, hf0, aff0 = _outproj(y0, _cast_bf16(ab_w_out[0]), xs, mod_tab[0], norm_g[0], wr[0], latent_only=False)
    cap_l = EC_FACTOR * SEQ // N_EXPERTS
    cap_c = EC_FACTOR * CTX_LEN // N_EXPERTS
    x2, h1 = _moe_block(x1, hf0, aff0, ((0, SEQ, cap_l), (SEQ, CTX_LEN, cap_c)),
                        exp_w_gate, exp_w_up, exp_w_down, 0, mod_tab[0], norm_g[0], mod_tab[1], norm_g[1],
                        n_latent=SEQ)

    p1 = _matmul(h1.reshape(b * S_ROWS, d), _cast_bf16(dif_w_in[0]), BF16).reshape(b, S_ROWS, C_IN)
    qkv1 = _prep(p1, tables, ones, ones, _PLAN_C)
    lambda_init = 0.8 - 0.6 * math.exp(-0.3 * 1)
    y1 = _attn_flash(dif_lambda[0], dif_subln[0][None, :], qkv1, q_col=0, k_col=8, v_col=16, streams=2,
                     shared_k=False, dv=C_V_DIM, heads=C_HEADS, tq=512, tk=4096,
                     diff_scale=(lambda_init, 1.0 - lambda_init))
    x3, hf1, aff1 = _outproj(y1, _cast_bf16(dif_w_out[0]), x2, mod_tab[1], norm_g[1], wr[1], latent_only=True)
    (out,) = _moe_block(x3, hf1, aff1, ((0, SEQ, cap_l),), exp_w_gate, exp_w_up, exp_w_down, 1,
                        mod_tab[1], norm_g[1], None, None, n_latent=SEQ)
    return out
```

```python
import functools
import math

import jax
import jax.numpy as jnp
from jax import lax
from jax.experimental import pallas as pl
from jax.experimental.pallas import tpu as pltpu

D_MODEL = 2048
SEQ = 4096
DEPTH = 2
GRID_W = 64
CTX_LEN = 256
HEAD_DIM = 128
ROPE_BASE = 10000.0
BLOCK = 128
WINDOW = 128
A_Q_HEADS = 8
A_KV_HEADS = 2
B_Q_HEADS = 8
B_KV_HEADS = 2
AB_IN = 3072
C_HEADS = 8
C_V_DIM = 256
C_IN = 6144
N_EXPERTS = 16
EC_FACTOR = 2
EPS = 1e-6
NEG_INF = -1e30

S_ROWS = SEQ + CTX_LEN
ROW_TILE = 256
LANES = 128
MIB = 1024 * 1024
LOG2E = math.log2(math.e)

F32 = jnp.float32
BF16 = jnp.bfloat16


def _params(sem, vmem_mib):
    return pltpu.CompilerParams(dimension_semantics=sem, vmem_limit_bytes=vmem_mib * MIB)


def _rms(x):
    return x * lax.rsqrt(jnp.mean(x * x, axis=-1, keepdims=True) + EPS)


def _dot(a, b):
    return jnp.dot(a, b, preferred_element_type=F32)


def _dot_nt(a, b):
    return lax.dot_general(a, b, (((1,), (1,)), ((), ())), preferred_element_type=F32)


def _mods_kernel(cv_ref, w_ref, b_ref, o_ref):
    cv = cv_ref[...]
    s = cv * jax.nn.sigmoid(cv)
    o_ref[...] = _dot(s.astype(BF16), w_ref[...].astype(BF16)) + b_ref[...]


def _mods(cv, mod_w, mod_b):
    tn = 1024
    n = mod_w.shape[-1]
    return pl.pallas_call(
        _mods_kernel,
        grid=(DEPTH, n // tn),
        in_specs=[
            pl.BlockSpec((8, D_MODEL), lambda l, j: (0, 0)),
            pl.BlockSpec((None, D_MODEL, tn), lambda l, j: (l, 0, j)),
            pl.BlockSpec((None, 1, tn), lambda l, j: (l, 0, j)),
        ],
        out_specs=pl.BlockSpec((None, 8, tn), lambda l, j: (l, 0, j)),
        out_shape=jax.ShapeDtypeStruct((DEPTH, 8, n), F32),
        compiler_params=_params(("parallel", "parallel"), 40),
        name="mods",
    )(cv, mod_w, mod_b.reshape(DEPTH, 1, n))


def _cast_kernel(x_ref, o_ref):
    o_ref[...] = x_ref[...].astype(o_ref.dtype)


def _cast_bf16(w):
    k, n = w.shape
    tk = 512
    return pl.pallas_call(
        _cast_kernel,
        grid=(k // tk,),
        in_specs=[pl.BlockSpec((tk, n), lambda i: (i, 0))],
        out_specs=pl.BlockSpec((tk, n), lambda i: (i, 0)),
        out_shape=jax.ShapeDtypeStruct((k, n), BF16),
        compiler_params=_params(("parallel",), 48),
        name="cast_bf16",
    )(w)


def _seg(i):
    return jnp.where(i < SEQ // ROW_TILE, 1, 0)


_LAT_TILES = SEQ // ROW_TILE


def _lat_row(bi, i):
    return (bi, jnp.minimum(i, _LAT_TILES - 1), 0)


def _ctx_row(bi, i):
    return (bi, 0, 0)


def _norm_mod_kernel(x_ref, ctx_ref, g_ref, mod_ref, h_ref):
    m = mod_ref[...]
    xv = jnp.where(pl.program_id(1) < _LAT_TILES, x_ref[...], ctx_ref[...])
    h = _rms(xv) * g_ref[0:1, :] * (1.0 + m[1:2, :]) + m[0:1, :]
    h_ref[...] = h.astype(h_ref.dtype)


def _norm_mod(x, ctx, g4, mod_tab):
    b, _, d = x.shape
    return pl.pallas_call(
        _norm_mod_kernel,
        grid=(b, S_ROWS // ROW_TILE),
        in_specs=[
            pl.BlockSpec((None, ROW_TILE, d), _lat_row),
            pl.BlockSpec((None, ROW_TILE, d), _ctx_row),
            pl.BlockSpec((4, d), lambda bi, i: (0, 0)),
            pl.BlockSpec((None, None, 6, d), lambda bi, i: (bi, _seg(i), 0, 0)),
        ],
        out_specs=pl.BlockSpec((None, ROW_TILE, d), lambda bi, i: (bi, i, 0)),
        out_shape=jax.ShapeDtypeStruct((b, S_ROWS, d), BF16),
        compiler_params=_params(("parallel", "parallel"), 32),
        name="norm_mod",
    )(x, ctx, g4, mod_tab)


def _mm_kernel(a_ref, w_ref, o_ref):
    o_ref[...] = _dot(a_ref[...], w_ref[...]).astype(o_ref.dtype)


def _matmul(a, w, out_dtype):
    m, k = a.shape
    n = w.shape[1]
    tm, tn = 1088, 1024
    return pl.pallas_call(
        _mm_kernel,
        grid=(n // tn, m // tm),
        in_specs=[
            pl.BlockSpec((tm, k), lambda j, i: (i, 0)),
            pl.BlockSpec((k, tn), lambda j, i: (0, j)),
        ],
        out_specs=pl.BlockSpec((tm, tn), lambda j, i: (i, j)),
        out_shape=jax.ShapeDtypeStruct((m, n), out_dtype),
        compiler_params=_params(("parallel", "parallel"), 48),
        name="in_proj",
    )(a, w)


def _rope_tables():
    n_rows = SEQ // GRID_W
    row = jnp.repeat(jnp.arange(n_rows), GRID_W).astype(F32)
    col = jnp.tile(jnp.arange(GRID_W), n_rows).astype(F32)
    n_freq = HEAD_DIM // 4
    inv = ROPE_BASE ** (-jnp.arange(n_freq, dtype=F32) / n_freq)
    ar = row[:, None] * inv
    ac = col[:, None] * inv
    cr, sr, cc, sc = jnp.cos(ar), jnp.sin(ar), jnp.cos(ac), jnp.sin(ac)
    z = jnp.zeros_like(sr)
    c = jnp.concatenate([cr, cr, cc, cc], axis=-1)
    sa = jnp.concatenate([-sr, z, -sc, z], axis=-1)
    sb = jnp.concatenate([z, sr, z, sc], axis=-1)
    pad = jnp.zeros((CTX_LEN, HEAD_DIM), F32)
    return (jnp.concatenate([c, pad + 1.0], axis=0), jnp.concatenate([sa, pad], axis=0),
            jnp.concatenate([sb, pad], axis=0))


def _prep_kernel(p_ref, c_ref, sa_ref, sb_ref, gq_ref, gk_ref, o_ref, *, plan):
    c, sa, sb = c_ref[...], sa_ref[...], sb_ref[...]
    for s, (norm, scale) in enumerate(plan):
        sl = slice(s * LANES, (s + 1) * LANES)
        if scale is None:
            o_ref[:, sl] = p_ref[:, sl]
            continue
        x = p_ref[:, sl].astype(F32)
        if norm is not None:
            x = _rms(x) * (gq_ref[...] if norm == "q" else gk_ref[...])
        r = x * c + pltpu.roll(x, 96, 1) * sa + pltpu.roll(x, 32, 1) * sb
        if scale != 1.0:
            r = r * scale
        o_ref[:, sl] = r.astype(o_ref.dtype)


def _prep(p, tables, gq, gk, plan):
    b, s, n = p.shape
    tab_spec = pl.BlockSpec((ROW_TILE, LANES), lambda bi, i: (i, 0))
    g_spec = pl.BlockSpec((1, LANES), lambda bi, i: (0, 0))
    return pl.pallas_call(
        functools.partial(_prep_kernel, plan=plan),
        grid=(b, s // ROW_TILE),
        in_specs=[pl.BlockSpec((None, ROW_TILE, n), lambda bi, i: (bi, i, 0)),
                  tab_spec, tab_spec, tab_spec, g_spec, g_spec],
        out_specs=pl.BlockSpec((None, ROW_TILE, n), lambda bi, i: (bi, i, 0)),
        out_shape=jax.ShapeDtypeStruct((b, s, n), BF16),
        compiler_params=_params(("parallel", "parallel"), 32),
        name="qk_prep",
    )(p, *tables, gq, gk)


def _lane_tiles(s):
    return [s[:, j * LANES:(j + 1) * LANES] for j in range(s.shape[1] // LANES)]


def _stack_heads(q_ref, groups):
    return jnp.concatenate([q_ref[:, g * LANES:(g + 1) * LANES] for g in range(groups)], axis=0)


def _attn_oneshot_kernel(sink_ref, q_ref, k_ref, v_ref, o_ref, *, groups, tq, band, use_sink):
    h = pl.program_id(1)
    qi = pl.program_id(2)
    rows = groups * tq
    ctx0 = SEQ if band else 0
    q = _stack_heads(q_ref, groups)
    tiles = _lane_tiles(_dot_nt(q, k_ref[ctx0:ctx0 + CTX_LEN, :]))
    n_ctx_tiles = len(tiles)
    if band:
        nband = tq + 2 * BLOCK
        start = jnp.clip(qi * tq - BLOCK, 0, SEQ - nband)
        start = pl.multiple_of(start, BLOCK)
        row = lax.broadcasted_iota(jnp.int32, (rows, nband), 0)
        q_pos = qi * tq + (row & (tq - 1))
        k_pos = start + lax.broadcasted_iota(jnp.int32, (rows, nband), 1)
        valid = jnp.abs(q_pos - k_pos) <= WINDOW
        tiles += _lane_tiles(jnp.where(valid, _dot_nt(q, k_ref[pl.ds(start, nband), :]), NEG_INF))
    m = jnp.max(functools.reduce(jnp.maximum, tiles), axis=-1, keepdims=True)
    if use_sink:
        row1 = lax.broadcasted_iota(jnp.int32, (rows, 1), 0)
        sink = jnp.zeros((rows, 1), F32)
        for g in range(groups):
            sink = jnp.where(row1 >= g * tq, sink_ref[h * groups + g] * LOG2E, sink)
        m = jnp.maximum(m, sink)
    m_b = jnp.broadcast_to(m, (rows, LANES))
    ps = [jnp.exp2(t - m_b) for t in tiles]
    l = jnp.sum(functools.reduce(jnp.add, ps), axis=-1, keepdims=True)
    if use_sink:
        l = l + jnp.exp2(sink - m)
    p_c = jnp.concatenate([t.astype(BF16) for t in ps[:n_ctx_tiles]], axis=1)
    o = _dot(p_c, v_ref[ctx0:ctx0 + CTX_LEN, :])
    if band:
        p_b = jnp.concatenate([t.astype(BF16) for t in ps[n_ctx_tiles:]], axis=1)
        o = o + _dot(p_b, v_ref[pl.ds(start, nband), :])
    o = o / l
    for g in range(groups):
        o_ref[:, g * LANES:(g + 1) * LANES] = o[g * tq:(g + 1) * tq, :].astype(o_ref.dtype)


def _attn_oneshot(sink, qkv, *, q_col, k_col, v_col, groups, kv_heads, q_row0, n_q, tq, band, use_sink):
    b = qkv.shape[0]
    k_rows = S_ROWS if band else CTX_LEN
    k_row0 = 0 if band else SEQ // CTX_LEN
    gw = groups * LANES
    return pl.pallas_call(
        functools.partial(_attn_oneshot_kernel, groups=groups, tq=tq, band=band, use_sink=use_sink),
        grid=(b, kv_heads, n_q),
        in_specs=[
            pl.BlockSpec(memory_space=pltpu.SMEM),
            pl.BlockSpec((None, tq, gw), lambda bi, h, i: (bi, i + q_row0, q_col // groups + h)),
            pl.BlockSpec((None, k_rows, LANES), lambda bi, h, i: (bi, k_row0, k_col + h)),
            pl.BlockSpec((None, k_rows, LANES), lambda bi, h, i: (bi, k_row0, v_col + h)),
        ],
        out_specs=pl.BlockSpec((None, tq, gw), lambda bi, h, i: (bi, i, h)),
        out_shape=jax.ShapeDtypeStruct((b, n_q * tq, kv_heads * gw), BF16),
        compiler_params=_params(("parallel", "parallel", "parallel"), 32),
        name="attn_oneshot",
    )(sink, qkv, qkv, qkv)


def _attn_flash_kernel(lam_ref, subg_ref, q_ref, k_ref, v_ref, o_ref, q_sc, m_sc, l_sc, acc_sc,
                       *, streams, shared_k, tq, tk, diff_scale):
    dv = acc_sc.shape[1]
    for g in range(streams):
        q_sc[g * tq:(g + 1) * tq, :] = q_ref[:, g * LANES:(g + 1) * LANES]
    m_sc[...] = jnp.full(m_sc.shape, NEG_INF, F32)
    l_sc[...] = jnp.zeros(l_sc.shape, F32)
    acc_sc[...] = jnp.zeros(acc_sc.shape, F32)

    def chunk(off, size):
        kc = k_ref[pl.ds(off, size), :]
        if shared_k:
            s = _dot_nt(q_sc[...], kc)
        else:
            s = jnp.concatenate([_dot_nt(q_sc[g * tq:(g + 1) * tq, :], kc[:, g * LANES:(g + 1) * LANES])
                                 for g in range(streams)], axis=0)
        tiles = _lane_tiles(s)
        m_prev = m_sc[...]
        m_new = jnp.maximum(m_prev, jnp.max(functools.reduce(jnp.maximum, tiles), axis=-1, keepdims=True))
        alpha = jnp.exp2(m_prev - m_new)
        ps = [jnp.exp2(t - m_new) for t in tiles]
        m_sc[...] = m_new
        l_sc[...] = alpha * l_sc[...] + functools.reduce(jnp.add, ps)
        p = jnp.concatenate([t.astype(BF16) for t in ps], axis=1)
        pv = _dot(p, v_ref[pl.ds(off, size), :])
        acc_sc[...] = jnp.concatenate([alpha] * (dv // LANES), axis=1) * acc_sc[...] + pv

    chunk(SEQ, CTX_LEN)

    def body(c, carry):
        chunk(pl.multiple_of(c * tk, tk), tk)
        return carry

    lax.fori_loop(0, SEQ // tk, body, 0)
    o = acc_sc[...] / jnp.sum(l_sc[...], axis=-1, keepdims=True)
    if diff_scale is None:
        for g in range(streams):
            o_ref[:, g * LANES:(g + 1) * LANES] = o[g * tq:(g + 1) * tq, :].astype(o_ref.dtype)
    else:
        lambda_init, out_scale = diff_scale
        lp = lam_ref[...]
        lam = (jnp.exp(jnp.sum(lp[0:1, :] * lp[1:2, :], axis=-1, keepdims=True))
               - jnp.exp(jnp.sum(lp[2:3, :] * lp[3:4, :], axis=-1, keepdims=True)) + lambda_init)
        o = o[0:tq, :] - lam * o[tq:2 * tq, :]
        o_ref[...] = (_rms(o) * subg_ref[...] * out_scale).astype(o_ref.dtype)


def _attn_flash(lam, subg, qkv, *, q_col, k_col, v_col, streams, shared_k, dv, heads, tq, tk, diff_scale):
    b = qkv.shape[0]
    qw = streams * LANES
    kw = LANES if shared_k else streams * LANES
    n_q = SEQ // tq
    ow = dv if diff_scale is not None else qw
    rows = streams * tq
    return pl.pallas_call(
        functools.partial(_attn_flash_kernel, streams=streams, shared_k=shared_k, tq=tq, tk=tk,
                          diff_scale=diff_scale),
        grid=(b, heads, n_q),
        in_specs=[
            pl.BlockSpec((4, LANES), lambda bi, h, i: (0, 0)),
            pl.BlockSpec((1, dv), lambda bi, h, i: (0, 0)),
            pl.BlockSpec((None, tq, qw), lambda bi, h, i: (bi, i, q_col + h)),
            pl.BlockSpec((None, S_ROWS, kw), lambda bi, h, i: (bi, 0, k_col + h)),
            pl.BlockSpec((None, S_ROWS, dv), lambda bi, h, i: (bi, 0, v_col + h)),
        ],
        out_specs=pl.BlockSpec((None, tq, ow), lambda bi, h, i: (bi, i, h)),
        out_shape=jax.ShapeDtypeStruct((b, SEQ, heads * ow), BF16),
        scratch_shapes=[pltpu.VMEM((rows, LANES), BF16), pltpu.VMEM((rows, LANES), F32),
                        pltpu.VMEM((rows, LANES), F32), pltpu.VMEM((rows, dv), F32)],
        compiler_params=_params(("parallel", "parallel", "parallel"), 48),
        name="attn_flash",
    )(lam, subg, qkv, qkv, qkv)


def _outproj_kernel(*refs, split):
    if split:
        ya_ref, yb_ref, yac_ref, ybc_ref, xl_ref, xc_ref = refs[:6]
        lat = pl.program_id(1) < _LAT_TILES
        y_in = jnp.concatenate([jnp.where(lat, ya_ref[...], yac_ref[...]),
                                jnp.where(lat, yb_ref[...], ybc_ref[...])], axis=1)
        x_in = jnp.where(lat, xl_ref[...], xc_ref[...])
        refs = refs[6:]
    else:
        y_in, x_in = refs[0][...], refs[1][...]
        refs = refs[2:]
    w_ref, mod_ref, g_ref, wr_ref, xo_ref, hf_ref, aff_ref = refs
    m = mod_ref[...]
    g = g_ref[...]
    y = _dot(y_in, w_ref[...])
    x1 = x_in + m[2:3, :] * (_rms(y) * g[1:2, :])
    xo_ref[...] = x1
    hf = _rms(x1) * g[2:3, :] * (1.0 + m[4:5, :]) + m[3:4, :]
    hf_ref[...] = hf
    logits = _dot(hf.astype(BF16), wr_ref[...]).T[0:N_EXPERTS, :]
    e = jnp.exp(logits - jnp.max(logits, axis=0, keepdims=True))
    aff_ref[...] = e / jnp.sum(e, axis=0, keepdims=True)


def _outproj(ys, xs, w_bf, mod_tab, g4, wr_bf, *, split):
    b, d = xs[0].shape[0], xs[0].shape[2]
    sy = S_ROWS if split else ys[0].shape[1]
    seg = _seg if split else (lambda i: 1)
    row = lambda bi, i: (bi, i, 0)
    if split:
        half = pl.BlockSpec((None, ROW_TILE, d // 2), _lat_row)
        half_c = pl.BlockSpec((None, ROW_TILE, d // 2), _ctx_row)
        data_specs = [half, half, half_c, half_c,
                      pl.BlockSpec((None, ROW_TILE, d), _lat_row), pl.BlockSpec((None, ROW_TILE, d), _ctx_row)]
    else:
        data_specs = [pl.BlockSpec((None, ROW_TILE, d), row), pl.BlockSpec((None, ROW_TILE, d), row)]
    out_f32 = jax.ShapeDtypeStruct((b, sy, d), F32)
    return pl.pallas_call(
        functools.partial(_outproj_kernel, split=split),
        grid=(b, sy // ROW_TILE),
        in_specs=data_specs + [
            pl.BlockSpec((d, d), lambda bi, i: (0, 0)),
            pl.BlockSpec((None, None, 6, d), lambda bi, i: (bi, seg(i), 0, 0)),
            pl.BlockSpec((4, d), lambda bi, i: (0, 0)),
            pl.BlockSpec((d, LANES), lambda bi, i: (0, 0)),
        ],
        out_specs=[pl.BlockSpec((None, ROW_TILE, d), row), pl.BlockSpec((None, ROW_TILE, d), row),
                   pl.BlockSpec((None, N_EXPERTS, ROW_TILE), lambda bi, i: (bi, 0, i))],
        out_shape=[out_f32, out_f32, jax.ShapeDtypeStruct((b, N_EXPERTS, sy), F32)],
        compiler_params=_params(("parallel", "parallel"), 48),
        name="out_proj",
    )(*ys, *xs, w_bf, mod_tab, g4, wr_bf)


def _row_copy(h_hbm, stage_sc, sem, src_row, dst_row, n_rows):
    return pltpu.make_async_copy(h_hbm.at[pl.ds(src_row, n_rows), :], stage_sc.at[pl.ds(dst_row, n_rows), :], sem)


def _moe_kernel(idx_ref, h_hbm, wg_ref, wu_ref, wd_ref, gate_ref, o_ref, stage_sc, x_sc, hmid_sc, sem, *, n_up, n_down):
    e = pl.program_id(0)
    s = pl.program_id(1)
    last = (e == pl.num_programs(0) - 1) & (s == n_up + n_down - 1)
    m = x_sc.shape[0]
    rows_per_step = m // n_up

    def issue_one(expert, row):
        _row_copy(h_hbm, stage_sc, sem, idx_ref[expert, row], row, 1).start()

    @pl.when(s == 0)
    def _():
        @pl.when(e == 0)
        def _():
            def body(i, carry):
                issue_one(0, i)
                return carry

            lax.fori_loop(0, m, body, 0, unroll=8)

        _row_copy(h_hbm, stage_sc, sem, 0, 0, m).wait()
        x_sc[...] = stage_sc[...].astype(BF16)

    @pl.when(s < n_up)
    def _():
        for i in range(rows_per_step):
            issue_one(e + 1, s * rows_per_step + i)
        x = x_sc[...]
        a = _dot(x, wg_ref[...].astype(BF16))
        u = _dot(x, wu_ref[...].astype(BF16))
        hmid_sc[s] = (a * jax.nn.sigmoid(a) * u).astype(BF16)

    @pl.when(s >= n_up)
    def _():
        hmid = jnp.concatenate([hmid_sc[f] for f in range(n_up)], axis=1)
        y = _dot(hmid, wd_ref[...].astype(BF16))
        o_ref[...] = (y * gate_ref[...]).astype(o_ref.dtype)

    @pl.when(last)
    def _():
        _row_copy(h_hbm, stage_sc, sem, 0, 0, m).wait()


def _moe(idx, h, w_g, w_u, w_d, layer, gates):
    e, m = idx.shape
    d = w_g.shape[2]
    f_dim = w_g.shape[-1]
    t_up, t_down = 256, 512
    n_up, n_down = f_dim // t_up, d // t_down
    idx = jnp.concatenate([idx, jnp.zeros((1, m), idx.dtype)], axis=0)
    up_tile = lambda ei, s, idx_ref: (layer, ei, 0, jnp.minimum(s, n_up - 1))
    down_tile = lambda s: jnp.maximum(s - n_up, 0)
    grid_spec = pltpu.PrefetchScalarGridSpec(
        num_scalar_prefetch=1,
        grid=(e, n_up + n_down),
        in_specs=[
            pl.BlockSpec(memory_space=pl.ANY),
            pl.BlockSpec((None, None, d, t_up), up_tile),
            pl.BlockSpec((None, None, d, t_up), up_tile),
            pl.BlockSpec((None, None, f_dim, t_down), lambda ei, s, idx_ref: (layer, ei, 0, down_tile(s))),
            pl.BlockSpec((None, m, 1), lambda ei, s, idx_ref: (ei, 0, 0)),
        ],
        out_specs=pl.BlockSpec((None, m, t_down), lambda ei, s, idx_ref: (ei, 0, down_tile(s))),
        scratch_shapes=[pltpu.VMEM((m, d), F32), pltpu.VMEM((m, d), BF16), pltpu.VMEM((n_up, m, t_up), BF16),
                        pltpu.SemaphoreType.DMA(())],
    )
    return pl.pallas_call(
        functools.partial(_moe_kernel, n_up=n_up, n_down=n_down),
        grid_spec=grid_spec,
        out_shape=jax.ShapeDtypeStruct((e, m, d), BF16),
        compiler_params=_params(("arbitrary", "arbitrary"), 58),
        name="moe_experts",
    )(idx, h, w_g, w_u, w_d, gates)


_BF16_ROWS = 16
_WIN = LANES + _BF16_ROWS


def _combine_kernel(starts_ref, tab_ref, y_hbm, x_ref, mod_ref, g_ref, *rest, cap_t, n_latent, has_next):
    if has_next:
        modn_ref, gn_ref, xo_ref, h_ref, buf, sem = rest
    else:
        xo_ref, buf, sem = rest
    b = pl.program_id(0)
    j = pl.program_id(1)
    nb = pl.num_programs(0)
    n_tiles = pl.num_programs(1)
    t = b * n_tiles + j
    cur = lax.rem(t, 2)

    def window_start(bb, jj, e):
        s = starts_ref[bb * N_EXPERTS + e, jj]
        return jnp.minimum(s // _BF16_ROWS * _BF16_ROWS, cap_t - _WIN)

    def window_copy(row, e, slot, n_rows):
        return pltpu.make_async_copy(y_hbm.at[pl.ds(row, n_rows), :], buf.at[slot, pl.ds(e * _WIN, n_rows), :],
                                     sem.at[slot])

    def fetch(bb, jj, slot):
        for e in range(N_EXPERTS):
            row = (e * nb + bb) * cap_t + window_start(bb, jj, e)
            window_copy(pl.multiple_of(row, _BF16_ROWS), e, slot, _WIN).start()

    @pl.when(t == 0)
    def _():
        fetch(0, 0, 0)

    @pl.when(t + 1 < nb * n_tiles)
    def _():
        t1 = t + 1
        fetch(t1 // n_tiles, lax.rem(t1, n_tiles), 1 - cur)

    window_copy(0, 0, cur, N_EXPERTS * _WIN).wait()

    tab = tab_ref[...]
    lane_f = lax.broadcasted_iota(jnp.int32, (LANES, LANES), 1).astype(F32)
    col_of = []
    for e in range(N_EXPERTS):
        slot_e = tab[:, e:e + 1]
        rel = slot_e - window_start(b, j, e).astype(F32) + float(e * _WIN)
        col_of.append(jnp.broadcast_to(jnp.where(slot_e >= 0.0, rel, -1.0), (LANES, LANES)))
    tiles = []
    for i in range(N_EXPERTS * _WIN // LANES):
        cols = lane_f + float(i * LANES)
        e_lo, e_hi = i * LANES // _WIN, (i * LANES + LANES - 1) // _WIN
        hit = col_of[e_lo] == cols
        if e_hi != e_lo:
            hit = hit | (col_of[e_hi] == cols)
        tiles.append(jnp.where(hit, 1.0, 0.0).astype(BF16))
    moe = _dot(jnp.concatenate(tiles, axis=1), buf[cur])

    is_lat = (j * LANES + lax.broadcasted_iota(jnp.int32, (LANES, 1), 0)) < n_latent

    def mod(ref, kk):
        return jnp.where(is_lat, ref[1, kk:kk + 1, :], ref[0, kk:kk + 1, :])

    x2 = x_ref[...] + mod(mod_ref, 5) * (_rms(moe) * g_ref[3:4, :])
    xo_ref[...] = x2
    if has_next:
        h_ref[...] = (_rms(x2) * gn_ref[0:1, :] * (1.0 + mod(modn_ref, 1)) + mod(modn_ref, 0)).astype(h_ref.dtype)


def _combine(starts, tab, y, x, mod_tab, g4, mod_next, g4_next, *, n_latent):
    b, sx, d = x.shape
    cap_t = y.shape[0] // (b * N_EXPERTS)
    has_next = mod_next is not None
    row = pl.BlockSpec((None, LANES, d), lambda bi, i, s: (bi, i, 0))
    mod_spec = pl.BlockSpec((None, 2, 6, d), lambda bi, i, s: (bi, 0, 0, 0))
    g_spec = pl.BlockSpec((4, d), lambda bi, i, s: (0, 0))
    in_specs = [pl.BlockSpec((None, LANES, LANES), lambda bi, i, s: (bi, i, 0)),
                pl.BlockSpec(memory_space=pl.ANY), row, mod_spec, g_spec]
    ins = [tab, y, x, mod_tab, g4]
    out_specs, out_shape = [row], [jax.ShapeDtypeStruct((b, sx, d), F32)]
    if has_next:
        in_specs += [mod_spec, g_spec]
        ins += [mod_next, g4_next]
        out_specs.append(row)
        out_shape.append(jax.ShapeDtypeStruct((b, sx, d), BF16))
    grid_spec = pltpu.PrefetchScalarGridSpec(
        num_scalar_prefetch=1, grid=(b, sx // LANES), in_specs=in_specs, out_specs=out_specs,
        scratch_shapes=[pltpu.VMEM((2, N_EXPERTS * _WIN, d), BF16), pltpu.SemaphoreType.DMA((2,))])
    return pl.pallas_call(
        functools.partial(_combine_kernel, cap_t=cap_t, n_latent=n_latent, has_next=has_next),
        grid_spec=grid_spec, out_shape=out_shape,
        compiler_params=_params(("arbitrary", "arbitrary"), 48), name="moe_combine",
    )(starts, *ins)


def _prefix_count(mask_tiles, tri):
    off = jnp.zeros((N_EXPERTS, 1), F32)
    out, before = [], []
    for mt in mask_tiles:
        before.append(off)
        w = _dot(jnp.where(mt, 1.0, 0.0).astype(BF16), tri) + off
        out.append(w)
        off = w[:, LANES - 1:LANES]
    return out, before


_MIN_EXP = -160.0
_EXP_STEPS = 8
_VALUE_STEPS = 40


def _route_kernel(aff_ref, tab_ref, starts_ref, slot_ref, atile_ref, *, segments):
    r_i = lax.broadcasted_iota(jnp.int32, (LANES, LANES), 0)
    c_i = lax.broadcasted_iota(jnp.int32, (LANES, LANES), 1)
    tri = jnp.where(r_i <= c_i, 1.0, 0.0).astype(BF16)
    shape = (N_EXPERTS, LANES)
    lane_e = lax.broadcasted_iota(jnp.int32, shape, 1)
    starts = jnp.zeros(shape, F32)
    base = 0
    tile0 = 0
    for t0, n, cap in segments:
        nt = n // LANES
        a_t = [aff_ref[:, t0 + j * LANES:t0 + (j + 1) * LANES] for j in range(nt)]

        def count_ge(v, a_t=a_t):
            v_b = jnp.broadcast_to(v, shape)
            cnt = functools.reduce(jnp.add, [jnp.where(t >= v_b, 1.0, 0.0) for t in a_t])
            return jnp.sum(cnt, axis=1, keepdims=True)

        e_lo = jnp.full((N_EXPERTS, 1), _MIN_EXP, F32)
        e_hi = jnp.full((N_EXPERTS, 1), 1.0, F32)
        lo = jnp.zeros((N_EXPERTS, 1), F32)
        hi = jnp.full((N_EXPERTS, 1), 2.0, F32)
        for _ in range(_EXP_STEPS):
            e_mid = jnp.floor((e_lo + e_hi) * 0.5)
            v = jnp.exp2(e_mid)
            ok = count_ge(v) >= cap
            e_lo, lo = jnp.where(ok, e_mid, e_lo), jnp.where(ok, v, lo)
            e_hi, hi = jnp.where(ok, e_hi, e_mid), jnp.where(ok, hi, v)
        for _ in range(_VALUE_STEPS):
            v = (lo + hi) * 0.5
            ok = count_ge(v) >= cap
            lo, hi = jnp.where(ok, v, lo), jnp.where(ok, hi, v)
        lo_b = jnp.broadcast_to(lo, shape)
        hi_b = jnp.broadcast_to(hi, shape)
        gt = [t >= hi_b for t in a_t]
        eq = [(t >= lo_b) & (t < hi_b) for t in a_t]
        need = jnp.broadcast_to(cap - count_ge(hi), shape)
        eq_rank, _ = _prefix_count(eq, tri)
        sel = [g | (q & (r <= need)) for g, q, r in zip(gt, eq, eq_rank)]
        pos, before = _prefix_count(sel, tri)
        for j in range(nt):
            starts = jnp.where(lane_e == tile0 + j, before[j] + float(base), starts)
            slot = jnp.where(sel[j], pos[j] + (base - 1.0), -1.0)
            slot_ref[tile0 + j] = slot
            atile_ref[tile0 + j] = a_t[j]
            padded =jnp.concatenate([slot, jnp.full((LANES - N_EXPERTS, LANES), -1.0, F32)], axis=0)
            tab_ref[t0 + j * LANES:t0 + (j + 1) * LANES, :] = padded.T
        base += cap
        tile0 += nt
    starts_ref[...] = starts


def _compact_kernel(starts_ref, slot_ref, aff_ref, idx_ref, gate_ref, acc_i, acc_g, *, segments, rows_per_sample):
    bi = pl.program_id(0)
    row0 = (bi * rows_per_sample).astype(F32)
    r_i = lax.broadcasted_iota(jnp.int32, (LANES, LANES), 0)
    lane1 = lax.broadcasted_iota(jnp.int32, (1, LANES), 1)
    base = 0
    tile0 = 0
    for t0, n, cap in segments:
        nt = n // LANES

        def expert_body(e, carry, t0=t0, nt=nt, cap=cap, base=base, tile0=tile0):
            srow = bi * N_EXPERTS + e
            for c in range(-(-cap // LANES)):
                lo = base + c * LANES
                slot_id = (r_i + lo).astype(F32)
                acc_i[...] = jnp.zeros((LANES, LANES), F32)
                acc_g[...] = jnp.zeros((LANES, LANES), F32)

                def tile_body(j, carry2):
                    first = starts_ref[srow, tile0 + j]
                    nxt = jnp.where(j + 1 < nt, starts_ref[srow, tile0 + jnp.minimum(j + 1, nt - 1)], base + cap)

                    @pl.when((first < lo + LANES) & (nxt > lo))
                    def _():
                        pe = jnp.broadcast_to(slot_ref[tile0 + j, pl.ds(e, 1), :], (LANES, LANES))
                        ae = jnp.broadcast_to(aff_ref[tile0 + j, pl.ds(e, 1), :], (LANES, LANES))
                        tok = jnp.broadcast_to((lane1 + (t0 + j * LANES)).astype(F32) + row0, (LANES, LANES))
                        hit = pe == slot_id
                        acc_i[...] += jnp.where(hit, tok, 0.0)
                        acc_g[...] += jnp.where(hit, ae, 0.0)

                    return carry2

                lax.fori_loop(0, nt, tile_body, 0)
                w = min(LANES, cap - c * LANES)
                col_i = jnp.broadcast_to(jnp.sum(acc_i[...], axis=1, keepdims=True), (LANES, LANES))
                col_g = jnp.broadcast_to(jnp.sum(acc_g[...], axis=1, keepdims=True), (LANES, LANES))
                idx_ref[e, :, lo:lo + w] = col_i.T[0:1, 0:w].astype(jnp.int32)
                gate_ref[e, :, lo:lo + w] = col_g.T[0:1, 0:w]
            return carry

        lax.fori_loop(0, N_EXPERTS, expert_body, 0)
        base += cap
        tile0 += nt


def _route(aff_t, segments, rows_per_sample):
    b, e, sy = aff_t.shape
    k = sum(cap for _, _, cap in segments)
    n_tiles = sum(n // LANES for _, n, _ in segments)
    tiles_spec = pl.BlockSpec((None, n_tiles, e, LANES), lambda bi: (bi, 0, 0, 0))
    tiles_shape = jax.ShapeDtypeStruct((b, n_tiles, e, LANES), F32)
    tab, starts, slots, a_tiles = pl.pallas_call(
        functools.partial(_route_kernel, segments=segments),
        grid=(b,),
        in_specs=[pl.BlockSpec((None, e, sy), lambda bi: (bi, 0, 0))],
        out_specs=[pl.BlockSpec((None, sy, LANES), lambda bi: (bi, 0, 0)),
                   pl.BlockSpec((None, e, LANES), lambda bi: (bi, 0, 0)), tiles_spec, tiles_spec],
        out_shape=[jax.ShapeDtypeStruct((b, sy, LANES), F32), jax.ShapeDtypeStruct((b, e, LANES), F32),
                   tiles_shape, tiles_shape],
        compiler_params=_params(("parallel",), 32),
        name="route_topk",
    )(aff_t)
    starts = starts.astype(jnp.int32).reshape(b * e, LANES)
    grid_spec = pltpu.PrefetchScalarGridSpec(
        num_scalar_prefetch=1, grid=(b,),
        in_specs=[pl.BlockSpec((None, n_tiles, e, LANES), lambda bi, s: (bi, 0, 0, 0)),
                  pl.BlockSpec((None, n_tiles, e, LANES), lambda bi, s: (bi, 0, 0, 0))],
        out_specs=[pl.BlockSpec((None, e, 1, k), lambda bi, s: (bi, 0, 0, 0)),
                   pl.BlockSpec((None, e, 1, k), lambda bi, s: (bi, 0, 0, 0))],
        scratch_shapes=[pltpu.VMEM((LANES, LANES), F32), pltpu.VMEM((LANES, LANES), F32)])
    idx, gates = pl.pallas_call(
        functools.partial(_compact_kernel, segments=segments, rows_per_sample=rows_per_sample),
        grid_spec=grid_spec,
        out_shape=[jax.ShapeDtypeStruct((b, e, 1, k), jnp.int32), jax.ShapeDtypeStruct((b, e, 1, k), F32)],
        compiler_params=_params(("parallel",), 32),
        name="route_compact",
    )(starts, slots, a_tiles)
    return tab, idx, gates, starts


def _moe_block(x, hf, aff_t, segments, w_g, w_u, w_d, layer, mod_tab, g4, mod_next, g4_next, *, n_latent):
    b, sy = aff_t.shape[0], aff_t.shape[2]
    tab, idx, gates, starts = _route(aff_t, segments, sy)
    k = idx.shape[-1]
    idx_em = jnp.swapaxes(idx[:, :, 0, :], 0, 1).reshape(N_EXPERTS, b * k)
    gates_em = jnp.swapaxes(gates[:, :, 0, :], 0, 1).reshape(N_EXPERTS, b * k, 1)
    y = _moe(idx_em, hf.reshape(b * sy, hf.shape[-1]), w_g, w_u, w_d, layer, gates_em)
    y2d = y.reshape(N_EXPERTS * b * k, y.shape[-1])
    return _combine(starts, tab, y2d, x, mod_tab, g4, mod_next, g4_next, n_latent=n_latent)


_SCALE = HEAD_DIM ** -0.5 * LOG2E
_PLAN_AB = tuple([(None, _SCALE)] * 8 + [(None, 1.0)] * 2 + [(None, None)] * 2
                 + [("q", _SCALE)] * 8 + [("k", 1.0)] * 2 + [(None, None)] * 2)
_PLAN_C = tuple([(None, _SCALE)] * 16 + [(None, 1.0)] * 16 + [(None, None)] * 16)


def kernel(x, c, ctx, c_ctx, mod_w, mod_b, norm_g, ab_w_in, ab_w_out, ab_sink, ab_q_norm, ab_k_norm,
           dif_w_in, dif_w_out, dif_lambda, dif_subln, router_w, exp_w_gate, exp_w_up, exp_w_down):
    b = x.shape[0]
    d = D_MODEL
    cv = jnp.concatenate([c, c_ctx[None, :], jnp.zeros((8 - b - 1, d), F32)], axis=0)
    mods = _mods(cv, mod_w, mod_b).reshape(DEPTH, 8, 6, d)
    mod_tab = jnp.stack([jnp.broadcast_to(mods[:, b][:, None], (DEPTH, b, 6, d)), mods[:, :b]], axis=2)
    tables = _rope_tables()
    ones = jnp.ones((1, LANES), F32)
    wr = jnp.pad(router_w, ((0, 0), (0, 0), (0, LANES - N_EXPERTS))).astype(BF16)

    h0 = _norm_mod(x, ctx, norm_g[0], mod_tab[0])
    p0 = _matmul(h0.reshape(b * S_ROWS, d), _cast_bf16(ab_w_in[0]), BF16).reshape(b, S_ROWS, AB_IN)
    qkv0 = _prep(p0, tables, ab_q_norm[0][None, :], ab_k_norm[0][None, :], _PLAN_AB)
    sink = ab_sink[0]
    common_a = dict(q_col=0, k_col=8, v_col=10, groups=4, kv_heads=A_KV_HEADS)
    common_b = dict(q_col=12, k_col=20, v_col=22, groups=4, kv_heads=B_KV_HEADS)
    ctx_q = dict(q_row0=SEQ // CTX_LEN, n_q=1, tq=CTX_LEN, band=False)
    ya_l = _attn_oneshot(sink, qkv0, **common_a, q_row0=0, n_q=SEQ // 256, tq=256, band=True, use_sink=True)
    ya_c = _attn_oneshot(sink, qkv0, **common_a, **ctx_q, use_sink=True)
    yb_c = _attn_oneshot(sink, qkv0, **common_b, **ctx_q, use_sink=False)
    yb_l = _attn_flash(jnp.zeros((4, LANES), F32), ones, qkv0, q_col=3, k_col=20, v_col=22, streams=4,
                       shared_k=True, dv=HEAD_DIM, heads=B_KV_HEADS, tq=256, tk=2048, diff_scale=None)
    x1, hf0, aff0 = _outproj((ya_l, yb_l, ya_c, yb_c), (x, ctx), _cast_bf16(ab_w_out[0]), mod_tab[0], norm_g[0],
                             wr[0], split=True)
    cap_l = EC_FACTOR * SEQ // N_EXPERTS
    cap_c = EC_FACTOR * CTX_LEN // N_EXPERTS
    x2, h1 = _moe_block(x1, hf0, aff0, ((0, SEQ, cap_l), (SEQ, CTX_LEN, cap_c)),
                        exp_w_gate, exp_w_up, exp_w_down, 0, mod_tab[0], norm_g[0], mod_tab[1], norm_g[1],
                        n_latent=SEQ)

    p1 = _matmul(h1.reshape(b * S_ROWS, d), _cast_bf16(dif_w_in[0]), BF16).reshape(b, S_ROWS, C_IN)
    qkv1 = _prep(p1, tables, ones, ones, _PLAN_C)
    lambda_init = 0.8 - 0.6 * math.exp(-0.3 * 1)
    y1 = _attn_flash(dif_lambda[0], dif_subln[0][None, :], qkv1, q_col=0, k_col=8, v_col=16, streams=2,
                     shared_k=False, dv=C_V_DIM, heads=C_HEADS, tq=512, tk=2048,
                     diff_scale=(lambda_init, 1.0 - lambda_init))
    x3, hf1, aff1 = _outproj((y1,), (x2,), _cast_bf16(dif_w_out[0]), mod_tab[1], norm_g[1], wr[1], split=False)
    (out,) = _moe_block(x3, hf1, aff1, ((0, SEQ, cap_l),), exp_w_gate, exp_w_up, exp_w_down, 1,
                        mod_tab[1], norm_g[1], None, None, n_latent=SEQ)
    return out
```

```python
import functools
import math

import jax
import jax.numpy as jnp
from jax import lax
from jax.experimental import pallas as pl
from jax.experimental.pallas import tpu as pltpu

D_MODEL = 2048
SEQ = 4096
DEPTH = 2
GRID_W = 64
CTX_LEN = 256
HEAD_DIM = 128
ROPE_BASE = 10000.0
BLOCK = 128
WINDOW = 128
A_Q_HEADS = 8
A_KV_HEADS = 2
B_Q_HEADS = 8
B_KV_HEADS = 2
AB_IN = 3072
C_HEADS = 8
C_V_DIM = 256
C_IN = 6144
N_EXPERTS = 16
EC_FACTOR = 2
EPS = 1e-6
NEG_INF = -1e30

S_ROWS = SEQ + CTX_LEN
ROW_TILE = 256
LANES = 128
MIB = 1024 * 1024
LOG2E = math.log2(math.e)

F32 = jnp.float32
BF16 = jnp.bfloat16


def _params(sem, vmem_mib):
    return pltpu.CompilerParams(dimension_semantics=sem, vmem_limit_bytes=vmem_mib * MIB)


def _rms(x):
    return x * lax.rsqrt(jnp.mean(x * x, axis=-1, keepdims=True) + EPS)


def _dot(a, b):
    return jnp.dot(a, b, preferred_element_type=F32)


def _dot_nt(a, b):
    return lax.dot_general(a, b, (((1,), (1,)), ((), ())), preferred_element_type=F32)


def _mods_kernel(cv_ref, w_ref, b_ref, o_ref):
    cv = cv_ref[...]
    s = cv * jax.nn.sigmoid(cv)
    o_ref[...] = _dot(s.astype(BF16), w_ref[...].astype(BF16)) + b_ref[...]


def _mods(cv, mod_w, mod_b):
    tn = 1024
    n = mod_w.shape[-1]
    return pl.pallas_call(
        _mods_kernel,
        grid=(DEPTH, n // tn),
        in_specs=[
            pl.BlockSpec((8, D_MODEL), lambda l, j: (0, 0)),
            pl.BlockSpec((None, D_MODEL, tn), lambda l, j: (l, 0, j)),
            pl.BlockSpec((None, 1, tn), lambda l, j: (l, 0, j)),
        ],
        out_specs=pl.BlockSpec((None, 8, tn), lambda l, j: (l, 0, j)),
        out_shape=jax.ShapeDtypeStruct((DEPTH, 8, n), F32),
        compiler_params=_params(("parallel", "parallel"), 40),
        name="mods",
    )(cv, mod_w, mod_b.reshape(DEPTH, 1, n))


def _cast_kernel(x_ref, o_ref):
    o_ref[...] = x_ref[...].astype(o_ref.dtype)


def _cast_bf16(w):
    k, n = w.shape
    tk = 512
    return pl.pallas_call(
        _cast_kernel,
        grid=(k // tk,),
        in_specs=[pl.BlockSpec((tk, n), lambda i: (i, 0))],
        out_specs=pl.BlockSpec((tk, n), lambda i: (i, 0)),
        out_shape=jax.ShapeDtypeStruct((k, n), BF16),
        compiler_params=_params(("parallel",), 48),
        name="cast_bf16",
    )(w)


def _seg(i):
    return jnp.where(i < SEQ // ROW_TILE, 1, 0)


_LAT_TILES = SEQ // ROW_TILE


def _lat_row(bi, i):
    return (bi, jnp.minimum(i, _LAT_TILES - 1), 0)


def _ctx_row(bi, i):
    return (bi, 0, 0)


def _norm_mod_kernel(x_ref, ctx_ref, g_ref, mod_ref, h_ref):
    m = mod_ref[...]
    xv = jnp.where(pl.program_id(1) < _LAT_TILES, x_ref[...], ctx_ref[...])
    h = _rms(xv) * g_ref[0:1, :] * (1.0 + m[1:2, :]) + m[0:1, :]
    h_ref[...] = h.astype(h_ref.dtype)


def _norm_mod(x, ctx, g4, mod_tab):
    b, _, d = x.shape
    return pl.pallas_call(
        _norm_mod_kernel,
        grid=(b, S_ROWS // ROW_TILE),
        in_specs=[
            pl.BlockSpec((None, ROW_TILE, d), _lat_row),
            pl.BlockSpec((None, ROW_TILE, d), _ctx_row),
            pl.BlockSpec((4, d), lambda bi, i: (0, 0)),
            pl.BlockSpec((None, None, 6, d), lambda bi, i: (bi, _seg(i), 0, 0)),
        ],
        out_specs=pl.BlockSpec((None, ROW_TILE, d), lambda bi, i: (bi, i, 0)),
        out_shape=jax.ShapeDtypeStruct((b, S_ROWS, d), BF16),
        compiler_params=_params(("parallel", "parallel"), 32),
        name="norm_mod",
    )(x, ctx, g4, mod_tab)


def _mm_kernel(a_ref, w_ref, o_ref):
    o_ref[...] = _dot(a_ref[...], w_ref[...]).astype(o_ref.dtype)


def _matmul(a, w, out_dtype):
    m, k = a.shape
    n = w.shape[1]
    tm, tn = 1088, 1024
    return pl.pallas_call(
        _mm_kernel,
        grid=(n // tn, m // tm),
        in_specs=[
            pl.BlockSpec((tm, k), lambda j, i: (i, 0)),
            pl.BlockSpec((k, tn), lambda j, i: (0, j)),
        ],
        out_specs=pl.BlockSpec((tm, tn), lambda j, i: (i, j)),
        out_shape=jax.ShapeDtypeStruct((m, n), out_dtype),
        compiler_params=_params(("parallel", "parallel"), 48),
        name="in_proj",
    )(a, w)


def _rope_tables():
    n_rows = SEQ // GRID_W
    row = jnp.repeat(jnp.arange(n_rows), GRID_W).astype(F32)
    col = jnp.tile(jnp.arange(GRID_W), n_rows).astype(F32)
    n_freq = HEAD_DIM // 4
    inv = ROPE_BASE ** (-jnp.arange(n_freq, dtype=F32) / n_freq)
    ar = row[:, None] * inv
    ac = col[:, None] * inv
    cr, sr, cc, sc = jnp.cos(ar), jnp.sin(ar), jnp.cos(ac), jnp.sin(ac)
    z = jnp.zeros_like(sr)
    c = jnp.concatenate([cr, cr, cc, cc], axis=-1)
    sa = jnp.concatenate([-sr, z, -sc, z], axis=-1)
    sb = jnp.concatenate([z, sr, z, sc], axis=-1)
    pad = jnp.zeros((CTX_LEN, HEAD_DIM), F32)
    return (jnp.concatenate([c, pad + 1.0], axis=0), jnp.concatenate([sa, pad], axis=0),
            jnp.concatenate([sb, pad], axis=0))


def _prep_kernel(p_ref, c_ref, sa_ref, sb_ref, gq_ref, gk_ref, o_ref, *, plan):
    c, sa, sb = c_ref[...], sa_ref[...], sb_ref[...]
    for s, (norm, scale) in enumerate(plan):
        sl = slice(s * LANES, (s + 1) * LANES)
        if scale is None:
            o_ref[:, sl] = p_ref[:, sl]
            continue
        x = p_ref[:, sl].astype(F32)
        if norm is not None:
            x = _rms(x) * (gq_ref[...] if norm == "q" else gk_ref[...])
        r = x * c + pltpu.roll(x, 96, 1) * sa + pltpu.roll(x, 32, 1) * sb
        if scale != 1.0:
            r = r * scale
        o_ref[:, sl] = r.astype(o_ref.dtype)


def _prep(p, tables, gq, gk, plan):
    b, s, n = p.shape
    tab_spec = pl.BlockSpec((ROW_TILE, LANES), lambda bi, i: (i, 0))
    g_spec = pl.BlockSpec((1, LANES), lambda bi, i: (0, 0))
    return pl.pallas_call(
        functools.partial(_prep_kernel, plan=plan),
        grid=(b, s // ROW_TILE),
        in_specs=[pl.BlockSpec((None, ROW_TILE, n), lambda bi, i: (bi, i, 0)),
                  tab_spec, tab_spec, tab_spec, g_spec, g_spec],
        out_specs=pl.BlockSpec((None, ROW_TILE, n), lambda bi, i: (bi, i, 0)),
        out_shape=jax.ShapeDtypeStruct((b, s, n), BF16),
        compiler_params=_params(("parallel", "parallel"), 32),
        name="qk_prep",
    )(p, *tables, gq, gk)


def _lane_tiles(s):
    return [s[:, j * LANES:(j + 1) * LANES] for j in range(s.shape[1] // LANES)]


def _stack_heads(q_ref, groups):
    return jnp.concatenate([q_ref[:, g * LANES:(g + 1) * LANES] for g in range(groups)], axis=0)


def _attn_oneshot_kernel(sink_ref, q_ref, k_ref, v_ref, o_ref, *, groups, tq, band, use_sink):
    h = pl.program_id(1)
    qi = pl.program_id(2)
    rows = groups * tq
    ctx0 = SEQ if band else 0
    q = _stack_heads(q_ref, groups)
    tiles = _lane_tiles(_dot_nt(q, k_ref[ctx0:ctx0 + CTX_LEN, :]))
    n_ctx_tiles = len(tiles)
    if band:
        nband = tq + 2 * BLOCK
        start = jnp.clip(qi * tq - BLOCK, 0, SEQ - nband)
        start = pl.multiple_of(start, BLOCK)
        row = lax.broadcasted_iota(jnp.int32, (rows, nband), 0)
        q_pos = qi * tq + (row & (tq - 1))
        k_pos = start + lax.broadcasted_iota(jnp.int32, (rows, nband), 1)
        valid = jnp.abs(q_pos - k_pos) <= WINDOW
        tiles += _lane_tiles(jnp.where(valid, _dot_nt(q, k_ref[pl.ds(start, nband), :]), NEG_INF))
    m = jnp.max(functools.reduce(jnp.maximum, tiles), axis=-1, keepdims=True)
    if use_sink:
        row1 = lax.broadcasted_iota(jnp.int32, (rows, 1), 0)
        sink = jnp.zeros((rows, 1), F32)
        for g in range(groups):
            sink = jnp.where(row1 >= g * tq, sink_ref[h * groups + g] * LOG2E, sink)
        m = jnp.maximum(m, sink)
    m_b = jnp.broadcast_to(m, (rows, LANES))
    ps = [jnp.exp2(t - m_b) for t in tiles]
    l = jnp.sum(functools.reduce(jnp.add, ps), axis=-1, keepdims=True)
    if use_sink:
        l = l + jnp.exp2(sink - m)
    p_c = jnp.concatenate([t.astype(BF16) for t in ps[:n_ctx_tiles]], axis=1)
    o = _dot(p_c, v_ref[ctx0:ctx0 + CTX_LEN, :])
    if band:
        p_b = jnp.concatenate([t.astype(BF16) for t in ps[n_ctx_tiles:]], axis=1)
        o = o + _dot(p_b, v_ref[pl.ds(start, nband), :])
    o = o / l
    for g in range(groups):
        o_ref[:, g * LANES:(g + 1) * LANES] = o[g * tq:(g + 1) * tq, :].astype(o_ref.dtype)


def _attn_oneshot(sink, qkv, *, q_col, k_col, v_col, groups, kv_heads, q_row0, n_q, tq, band, use_sink):
    b = qkv.shape[0]
    k_rows = S_ROWS if band else CTX_LEN
    k_row0 = 0 if band else SEQ // CTX_LEN
    gw = groups * LANES
    return pl.pallas_call(
        functools.partial(_attn_oneshot_kernel, groups=groups, tq=tq, band=band, use_sink=use_sink),
        grid=(b, kv_heads, n_q),
        in_specs=[
            pl.BlockSpec(memory_space=pltpu.SMEM),
            pl.BlockSpec((None, tq, gw), lambda bi, h, i: (bi, i + q_row0, q_col // groups + h)),
            pl.BlockSpec((None, k_rows, LANES), lambda bi, h, i: (bi, k_row0, k_col + h)),
            pl.BlockSpec((None, k_rows, LANES), lambda bi, h, i: (bi, k_row0, v_col + h)),
        ],
        out_specs=pl.BlockSpec((None, tq, gw), lambda bi, h, i: (bi, i, h)),
        out_shape=jax.ShapeDtypeStruct((b, n_q * tq, kv_heads * gw), BF16),
        compiler_params=_params(("parallel", "parallel", "parallel"), 32),
        name="attn_oneshot",
    )(sink, qkv, qkv, qkv)


def _attn_flash_kernel(lam_ref, subg_ref, q_ref, k_ref, v_ref, o_ref, q_sc, m_sc, l_sc, acc_sc,
                       *, streams, shared_k, tq, tk, diff_scale):
    dv = acc_sc.shape[1]
    for g in range(streams):
        q_sc[g * tq:(g + 1) * tq, :] = q_ref[:, g * LANES:(g + 1) * LANES]
    m_sc[...] = jnp.full(m_sc.shape, NEG_INF, F32)
    l_sc[...] = jnp.zeros(l_sc.shape, F32)
    acc_sc[...] = jnp.zeros(acc_sc.shape, F32)

    def chunk(off, size):
        kc = k_ref[pl.ds(off, size), :]
        if shared_k:
            s = _dot_nt(q_sc[...], kc)
        else:
            s = jnp.concatenate([_dot_nt(q_sc[g * tq:(g + 1) * tq, :], kc[:, g * LANES:(g + 1) * LANES])
                                 for g in range(streams)], axis=0)
        tiles = _lane_tiles(s)
        m_prev = m_sc[...]
        m_new = jnp.maximum(m_prev, jnp.max(functools.reduce(jnp.maximum, tiles), axis=-1, keepdims=True))
        alpha = jnp.exp2(m_prev - m_new)
        ps = [jnp.exp2(t - m_new) for t in tiles]
        m_sc[...] = m_new
        l_sc[...] = alpha * l_sc[...] + functools.reduce(jnp.add, ps)
        p = jnp.concatenate([t.astype(BF16) for t in ps], axis=1)
        pv = _dot(p, v_ref[pl.ds(off, size), :])
        acc_sc[...] = jnp.concatenate([alpha] * (dv // LANES), axis=1) * acc_sc[...] + pv

    chunk(SEQ, CTX_LEN)

    def body(c, carry):
        chunk(pl.multiple_of(c * tk, tk), tk)
        return carry

    lax.fori_loop(0, SEQ // tk, body, 0)
    o = acc_sc[...] / jnp.sum(l_sc[...], axis=-1, keepdims=True)
    if diff_scale is None:
        for g in range(streams):
            o_ref[:, g * LANES:(g + 1) * LANES] = o[g * tq:(g + 1) * tq, :].astype(o_ref.dtype)
    else:
        lambda_init, out_scale = diff_scale
        lp = lam_ref[...]
        lam = (jnp.exp(jnp.sum(lp[0:1, :] * lp[1:2, :], axis=-1, keepdims=True))
               - jnp.exp(jnp.sum(lp[2:3, :] * lp[3:4, :], axis=-1, keepdims=True)) + lambda_init)
        o = o[0:tq, :] - lam * o[tq:2 * tq, :]
        o_ref[...] = (_rms(o) * subg_ref[...] * out_scale).astype(o_ref.dtype)


def _attn_flash(lam, subg, qkv, *, q_col, k_col, v_col, streams, shared_k, dv, heads, tq, tk, diff_scale):
    b = qkv.shape[0]
    qw = streams * LANES
    kw = LANES if shared_k else streams * LANES
    n_q = SEQ // tq
    ow = dv if diff_scale is not None else qw
    rows = streams * tq
    return pl.pallas_call(
        functools.partial(_attn_flash_kernel, streams=streams, shared_k=shared_k, tq=tq, tk=tk,
                          diff_scale=diff_scale),
        grid=(b, heads, n_q),
        in_specs=[
            pl.BlockSpec((4, LANES), lambda bi, h, i: (0, 0)),
            pl.BlockSpec((1, dv), lambda bi, h, i: (0, 0)),
            pl.BlockSpec((None, tq, qw), lambda bi, h, i: (bi, i, q_col + h)),
            pl.BlockSpec((None, S_ROWS, kw), lambda bi, h, i: (bi, 0, k_col + h)),
            pl.BlockSpec((None, S_ROWS, dv), lambda bi, h, i: (bi, 0, v_col + h)),
        ],
        out_specs=pl.BlockSpec((None, tq, ow), lambda bi, h, i: (bi, i, h)),
        out_shape=jax.ShapeDtypeStruct((b, SEQ, heads * ow), BF16),
        scratch_shapes=[pltpu.VMEM((rows, LANES), BF16), pltpu.VMEM((rows, LANES), F32),
                        pltpu.VMEM((rows, LANES), F32), pltpu.VMEM((rows, dv), F32)],
        compiler_params=_params(("parallel", "parallel", "parallel"), 48),
        name="attn_flash",
    )(lam, subg, qkv, qkv, qkv)


def _outproj_kernel(*refs, split):
    if split:
        ya_ref, yb_ref, yac_ref, ybc_ref, xl_ref, xc_ref = refs[:6]
        lat = pl.program_id(1) < _LAT_TILES
        y_in = jnp.concatenate([jnp.where(lat, ya_ref[...], yac_ref[...]),
                                jnp.where(lat, yb_ref[...], ybc_ref[...])], axis=1)
        x_in = jnp.where(lat, xl_ref[...], xc_ref[...])
        refs = refs[6:]
    else:
        y_in, x_in = refs[0][...], refs[1][...]
        refs = refs[2:]
    w_ref, mod_ref, g_ref, wr_ref, xo_ref, hf_ref, aff_ref = refs
    m = mod_ref[...]
    g = g_ref[...]
    y = _dot(y_in, w_ref[...])
    x1 = x_in + m[2:3, :] * (_rms(y) * g[1:2, :])
    xo_ref[...] = x1
    hf = _rms(x1) * g[2:3, :] * (1.0 + m[4:5, :]) + m[3:4, :]
    hf_ref[...] = hf
    logits = _dot(hf.astype(BF16), wr_ref[...]).T[0:N_EXPERTS, :]
    e = jnp.exp(logits - jnp.max(logits, axis=0, keepdims=True))
    aff_ref[...] = e / jnp.sum(e, axis=0, keepdims=True)


def _outproj(ys, xs, w_bf, mod_tab, g4, wr_bf, *, split):
    b, d = xs[0].shape[0], xs[0].shape[2]
    sy = S_ROWS if split else ys[0].shape[1]
    seg = _seg if split else (lambda i: 1)
    row = lambda bi, i: (bi, i, 0)
    if split:
        half = pl.BlockSpec((None, ROW_TILE, d // 2), _lat_row)
        half_c = pl.BlockSpec((None, ROW_TILE, d // 2), _ctx_row)
        data_specs = [half, half, half_c, half_c,
                      pl.BlockSpec((None, ROW_TILE, d), _lat_row), pl.BlockSpec((None, ROW_TILE, d), _ctx_row)]
    else:
        data_specs = [pl.BlockSpec((None, ROW_TILE, d), row), pl.BlockSpec((None, ROW_TILE, d), row)]
    out_f32 = jax.ShapeDtypeStruct((b, sy, d), F32)
    return pl.pallas_call(
        functools.partial(_outproj_kernel, split=split),
        grid=(b, sy // ROW_TILE),
        in_specs=data_specs + [
            pl.BlockSpec((d, d), lambda bi, i: (0, 0)),
            pl.BlockSpec((None, None, 6, d), lambda bi, i: (bi, seg(i), 0, 0)),
            pl.BlockSpec((4, d), lambda bi, i: (0, 0)),
            pl.BlockSpec((d, LANES), lambda bi, i: (0, 0)),
        ],
        out_specs=[pl.BlockSpec((None, ROW_TILE, d), row), pl.BlockSpec((None, ROW_TILE, d), row),
                   pl.BlockSpec((None, N_EXPERTS, ROW_TILE), lambda bi, i: (bi, 0, i))],
        out_shape=[out_f32, out_f32, jax.ShapeDtypeStruct((b, N_EXPERTS, sy), F32)],
        compiler_params=_params(("parallel", "parallel"), 48),
        name="out_proj",
    )(*ys, *xs, w_bf, mod_tab, g4, wr_bf)


def _row_copy(h_hbm, stage_sc, sem, src_row, dst_row, n_rows):
    return pltpu.make_async_copy(h_hbm.at[pl.ds(src_row, n_rows), :], stage_sc.at[pl.ds(dst_row, n_rows), :], sem)


def _moe_kernel(idx_ref, h_hbm, wg_ref, wu_ref, wd_ref, gate_ref, o_ref, stage_sc, x_sc, hmid_sc, sem, *, n_up, n_down):
    e = pl.program_id(0)
    s = pl.program_id(1)
    last = (e == pl.num_programs(0) - 1) & (s == n_up + n_down - 1)
    m = x_sc.shape[0]
    rows_per_step = m // n_up

    def issue_one(expert, row):
        _row_copy(h_hbm, stage_sc, sem, idx_ref[expert, row], row, 1).start()

    @pl.when(s == 0)
    def _():
        @pl.when(e == 0)
        def _():
            def body(i, carry):
                issue_one(0, i)
                return carry

            lax.fori_loop(0, m, body, 0, unroll=8)

        _row_copy(h_hbm, stage_sc, sem, 0, 0, m).wait()
        x_sc[...] = stage_sc[...].astype(BF16)

    @pl.when(s < n_up)
    def _():
        for i in range(rows_per_step):
            issue_one(e + 1, s * rows_per_step + i)
        x = x_sc[...]
        a = _dot(x, wg_ref[...].astype(BF16))
        u = _dot(x, wu_ref[...].astype(BF16))
        hmid_sc[s] = (a * jax.nn.sigmoid(a) * u).astype(BF16)

    @pl.when(s >= n_up)
    def _():
        hmid = jnp.concatenate([hmid_sc[f] for f in range(n_up)], axis=1)
        y = _dot(hmid, wd_ref[...].astype(BF16))
        o_ref[...] = (y * gate_ref[...]).astype(o_ref.dtype)

    @pl.when(last)
    def _():
        _row_copy(h_hbm, stage_sc, sem, 0, 0, m).wait()


def _moe(idx, h, w_g, w_u, w_d, layer, gates):
    e, m = idx.shape
    d = w_g.shape[2]
    f_dim = w_g.shape[-1]
    t_up, t_down = 256, 512
    n_up, n_down = f_dim // t_up, d // t_down
    idx = jnp.concatenate([idx, jnp.zeros((1, m), idx.dtype)], axis=0)
    up_tile = lambda ei, s, idx_ref: (layer, ei, 0, jnp.minimum(s, n_up - 1))
    down_tile = lambda s: jnp.maximum(s - n_up, 0)
    grid_spec = pltpu.PrefetchScalarGridSpec(
        num_scalar_prefetch=1,
        grid=(e, n_up + n_down),
        in_specs=[
            pl.BlockSpec(memory_space=pl.ANY),
            pl.BlockSpec((None, None, d, t_up), up_tile),
            pl.BlockSpec((None, None, d, t_up), up_tile),
            pl.BlockSpec((None, None, f_dim, t_down), lambda ei, s, idx_ref: (layer, ei, 0, down_tile(s))),
            pl.BlockSpec((None, m, 1), lambda ei, s, idx_ref: (ei, 0, 0)),
        ],
        out_specs=pl.BlockSpec((None, m, t_down), lambda ei, s, idx_ref: (ei, 0, down_tile(s))),
        scratch_shapes=[pltpu.VMEM((m, d), F32), pltpu.VMEM((m, d), BF16), pltpu.VMEM((n_up, m, t_up), BF16),
                        pltpu.SemaphoreType.DMA(())],
    )
    return pl.pallas_call(
        functools.partial(_moe_kernel, n_up=n_up, n_down=n_down),
        grid_spec=grid_spec,
        out_shape=jax.ShapeDtypeStruct((e, m, d), BF16),
        compiler_params=_params(("arbitrary", "arbitrary"), 58),
        name="moe_experts",
    )(idx, h, w_g, w_u, w_d, gates)


_BF16_ROWS = 16
_WIN = LANES + _BF16_ROWS


def _combine_kernel(starts_ref, tab_ref, y_hbm, x_ref, mod_ref, g_ref, *rest, cap_t, n_latent, has_next):
    if has_next:
        modn_ref, gn_ref, xo_ref, h_ref, buf, sem = rest
    else:
        xo_ref, buf, sem = rest
    b = pl.program_id(0)
    j = pl.program_id(1)
    nb = pl.num_programs(0)
    n_tiles = pl.num_programs(1)
    t = b * n_tiles + j
    cur = lax.rem(t, 2)

    def window_start(bb, jj, e):
        s = starts_ref[bb * N_EXPERTS + e, jj]
        return jnp.minimum(s // _BF16_ROWS * _BF16_ROWS, cap_t - _WIN)

    def window_copy(row, e, slot, n_rows):
        return pltpu.make_async_copy(y_hbm.at[pl.ds(row, n_rows), :], buf.at[slot, pl.ds(e * _WIN, n_rows), :],
                                     sem.at[slot])

    def fetch(bb, jj, slot):
        for e in range(N_EXPERTS):
            row = (e * nb + bb) * cap_t + window_start(bb, jj, e)
            window_copy(pl.multiple_of(row, _BF16_ROWS), e, slot, _WIN).start()

    @pl.when(t == 0)
    def _():
        fetch(0, 0, 0)

    @pl.when(t + 1 < nb * n_tiles)
    def _():
        t1 = t + 1
        fetch(t1 // n_tiles, lax.rem(t1, n_tiles), 1 - cur)

    window_copy(0, 0, cur, N_EXPERTS * _WIN).wait()

    tab = tab_ref[...]
    lane_f = lax.broadcasted_iota(jnp.int32, (LANES, LANES), 1).astype(F32)
    col_of = []
    for e in range(N_EXPERTS):
        slot_e = tab[:, e:e + 1]
        rel = slot_e - window_start(b, j, e).astype(F32) + float(e * _WIN)
        col_of.append(jnp.broadcast_to(jnp.where(slot_e >= 0.0, rel, -1.0), (LANES, LANES)))
    tiles = []
    for i in range(N_EXPERTS * _WIN // LANES):
        cols = lane_f + float(i * LANES)
        e_lo, e_hi = i * LANES // _WIN, (i * LANES + LANES - 1) // _WIN
        hit = col_of[e_lo] == cols
        if e_hi != e_lo:
            hit = hit | (col_of[e_hi] == cols)
        tiles.append(jnp.where(hit, 1.0, 0.0).astype(BF16))
    moe = _dot(jnp.concatenate(tiles, axis=1), buf[cur])

    is_lat = (j * LANES + lax.broadcasted_iota(jnp.int32, (LANES, 1), 0)) < n_latent

    def mod(ref, kk):
        return jnp.where(is_lat, ref[1, kk:kk + 1, :], ref[0, kk:kk + 1, :])

    x2 = x_ref[...] + mod(mod_ref, 5) * (_rms(moe) * g_ref[3:4, :])
    xo_ref[...] = x2
    if has_next:
        h_ref[...] = (_rms(x2) * gn_ref[0:1, :] * (1.0 + mod(modn_ref, 1)) + mod(modn_ref, 0)).astype(h_ref.dtype)


def _combine(starts, tab, y, x, mod_tab, g4, mod_next, g4_next, *, n_latent):
    b, sx, d = x.shape
    cap_t = y.shape[0] // (b * N_EXPERTS)
    has_next = mod_next is not None
    row = pl.BlockSpec((None, LANES, d), lambda bi, i, s: (bi, i, 0))
    mod_spec = pl.BlockSpec((None, 2, 6, d), lambda bi, i, s: (bi, 0, 0, 0))
    g_spec = pl.BlockSpec((4, d), lambda bi, i, s: (0, 0))
    in_specs = [pl.BlockSpec((None, LANES, LANES), lambda bi, i, s: (bi, i, 0)),
                pl.BlockSpec(memory_space=pl.ANY), row, mod_spec, g_spec]
    ins = [tab, y, x, mod_tab, g4]
    out_specs, out_shape = [row], [jax.ShapeDtypeStruct((b, sx, d), F32)]
    if has_next:
        in_specs += [mod_spec, g_spec]
        ins += [mod_next, g4_next]
        out_specs.append(row)
        out_shape.append(jax.ShapeDtypeStruct((b, sx, d), BF16))
    grid_spec = pltpu.PrefetchScalarGridSpec(
        num_scalar_prefetch=1, grid=(b, sx // LANES), in_specs=in_specs, out_specs=out_specs,
        scratch_shapes=[pltpu.VMEM((2, N_EXPERTS * _WIN, d), BF16), pltpu.SemaphoreType.DMA((2,))])
    return pl.pallas_call(
        functools.partial(_combine_kernel, cap_t=cap_t, n_latent=n_latent, has_next=has_next),
        grid_spec=grid_spec, out_shape=out_shape,
        compiler_params=_params(("arbitrary", "arbitrary"), 48), name="moe_combine",
    )(starts, *ins)


def _prefix_count(mask_tiles, tri):
    off = jnp.zeros((N_EXPERTS, 1), F32)
    out, before = [], []
    for mt in mask_tiles:
        before.append(off)
        w = _dot(jnp.where(mt, 1.0, 0.0).astype(BF16), tri) + off
        out.append(w)
        off = w[:, LANES - 1:LANES]
    return out, before


_MIN_EXP = -160.0
_EXP_STEPS = 8
_VALUE_STEPS = 40


def _route_kernel(aff_ref, tab_ref, starts_ref, slot_ref, atile_ref, *, segments):
    r_i = lax.broadcasted_iota(jnp.int32, (LANES, LANES), 0)
    c_i = lax.broadcasted_iota(jnp.int32, (LANES, LANES), 1)
    tri = jnp.where(r_i <= c_i, 1.0, 0.0).astype(BF16)
    shape = (N_EXPERTS, LANES)
    lane_e = lax.broadcasted_iota(jnp.int32, shape, 1)
    starts = jnp.zeros(shape, F32)
    base = 0
    tile0 = 0
    for t0, n, cap in segments:
        nt = n // LANES
        a_t = [aff_ref[:, t0 + j * LANES:t0 + (j + 1) * LANES] for j in range(nt)]

        def count_ge(v, a_t=a_t):
            v_b = jnp.broadcast_to(v, shape)
            cnt = functools.reduce(jnp.add, [jnp.where(t >= v_b, 1.0, 0.0) for t in a_t])
            return jnp.sum(cnt, axis=1, keepdims=True)

        e_lo = jnp.full((N_EXPERTS, 1), _MIN_EXP, F32)
        e_hi = jnp.full((N_EXPERTS, 1), 1.0, F32)
        lo = jnp.zeros((N_EXPERTS, 1), F32)
        hi = jnp.full((N_EXPERTS, 1), 2.0, F32)
        for _ in range(_EXP_STEPS):
            e_mid = jnp.floor((e_lo + e_hi) * 0.5)
            v = jnp.exp2(e_mid)
            ok = count_ge(v) >= cap
            e_lo, lo = jnp.where(ok, e_mid, e_lo), jnp.where(ok, v, lo)
            e_hi, hi = jnp.where(ok, e_hi, e_mid), jnp.where(ok, hi, v)
        for _ in range(_VALUE_STEPS):
            v = (lo + hi) * 0.5
            ok = count_ge(v) >= cap
            lo, hi = jnp.where(ok, v, lo), jnp.where(ok, hi, v)
        lo_b = jnp.broadcast_to(lo, shape)
        hi_b = jnp.broadcast_to(hi, shape)
        gt = [t >= hi_b for t in a_t]
        eq = [(t >= lo_b) & (t < hi_b) for t in a_t]
        need = jnp.broadcast_to(cap - count_ge(hi), shape)
        eq_rank, _ = _prefix_count(eq, tri)
        sel = [g | (q & (r <= need)) for g, q, r in zip(gt, eq, eq_rank)]
        pos, before = _prefix_count(sel, tri)
        for j in range(nt):
            starts = jnp.where(lane_e == tile0 + j, before[j] + float(base), starts)
            slot = jnp.where(sel[j], pos[j] + (base - 1.0), -1.0)
            slot_ref[tile0 + j] = slot
            atile_ref[tile0 + j] = a_t[j]
            padded =jnp.concatenate([slot, jnp.full((LANES - N_EXPERTS, LANES), -1.0, F32)], axis=0)
            tab_ref[t0 + j * LANES:t0 + (j + 1) * LANES, :] = padded.T
        base += cap
        tile0 += nt
    starts_ref[...] = starts


def _compact_kernel(starts_ref, slot_ref, aff_ref, idx_ref, gate_ref, acc_i, acc_g, *, segments, rows_per_sample):
    bi = pl.program_id(0)
    row0 = (bi * rows_per_sample).astype(F32)
    r_i = lax.broadcasted_iota(jnp.int32, (LANES, LANES), 0)
    lane1 = lax.broadcasted_iota(jnp.int32, (1, LANES), 1)
    base = 0
    tile0 = 0
    for t0, n, cap in segments:
        nt = n // LANES

        def expert_body(e, carry, t0=t0, nt=nt, cap=cap, base=base, tile0=tile0):
            srow = bi * N_EXPERTS + e
            n_chunks = -(-cap // LANES)

            def scan(j, bounds):
                first = starts_ref[srow, tile0 + j]
                nxt = jnp.where(j + 1 < nt, starts_ref[srow, tile0 + jnp.minimum(j + 1, nt - 1)], base + cap)
                out = []
                for c in range(n_chunks):
                    lo = base + c * LANES
                    out.append(bounds[2 * c] + (nxt <= lo).astype(jnp.int32))
                    out.append(bounds[2 * c + 1] + (first < lo + LANES).astype(jnp.int32))
                return tuple(out)

            bounds = lax.fori_loop(0, nt, scan, (jnp.int32(0),) * (2 * n_chunks), unroll=4)
            for c in range(n_chunks):
                lo = base + c * LANES
                j_lo, j_hi = bounds[2 * c], bounds[2 * c + 1]
                slot_id = (r_i + lo).astype(F32)
                acc_i[...] = jnp.zeros((LANES, LANES), F32)
                acc_g[...] = jnp.zeros((LANES, LANES), F32)

                def tile_body(j, carry2):
                    pe = jnp.broadcast_to(slot_ref[tile0 + j, pl.ds(e, 1), :], (LANES, LANES))
                    ae = jnp.broadcast_to(aff_ref[tile0 + j, pl.ds(e, 1), :], (LANES, LANES))
                    tok = jnp.broadcast_to((lane1 + (t0 + j * LANES)).astype(F32) + row0, (LANES, LANES))
                    hit = pe == slot_id
                    acc_i[...] += jnp.where(hit, tok, 0.0)
                    acc_g[...] += jnp.where(hit, ae, 0.0)
                    return carry2

                lax.fori_loop(j_lo, j_hi, tile_body, 0)
                w = min(LANES, cap - c * LANES)
                col_i = jnp.broadcast_to(jnp.sum(acc_i[...], axis=1, keepdims=True), (LANES, LANES))
                col_g = jnp.broadcast_to(jnp.sum(acc_g[...], axis=1, keepdims=True), (LANES, LANES))
                idx_ref[e, :, lo:lo + w] = col_i.T[0:1, 0:w].astype(jnp.int32)
                gate_ref[e, :, lo:lo + w] = col_g.T[0:1, 0:w]
            return carry

        lax.fori_loop(0, N_EXPERTS, expert_body, 0)
        base += cap
        tile0 += nt


def _route(aff_t, segments, rows_per_sample):
    b, e, sy = aff_t.shape
    k = sum(cap for _, _, cap in segments)
    n_tiles = sum(n // LANES for _, n, _ in segments)
    tiles_spec = pl.BlockSpec((None, n_tiles, e, LANES), lambda bi: (bi, 0, 0, 0))
    tiles_shape = jax.ShapeDtypeStruct((b, n_tiles, e, LANES), F32)
    tab, starts, slots, a_tiles = pl.pallas_call(
        functools.partial(_route_kernel, segments=segments),
        grid=(b,),
        in_specs=[pl.BlockSpec((None, e, sy), lambda bi: (bi, 0, 0))],
        out_specs=[pl.BlockSpec((None, sy, LANES), lambda bi: (bi, 0, 0)),
                   pl.BlockSpec((None, e, LANES), lambda bi: (bi, 0, 0)), tiles_spec, tiles_spec],
        out_shape=[jax.ShapeDtypeStruct((b, sy, LANES), F32), jax.ShapeDtypeStruct((b, e, LANES), F32),
                   tiles_shape, tiles_shape],
        compiler_params=_params(("parallel",), 32),
        name="route_topk",
    )(aff_t)
    starts = starts.astype(jnp.int32).reshape(b * e, LANES)
    grid_spec = pltpu.PrefetchScalarGridSpec(
        num_scalar_prefetch=1, grid=(b,),
        in_specs=[pl.BlockSpec((None, n_tiles, e, LANES), lambda bi, s: (bi, 0, 0, 0)),
                  pl.BlockSpec((None, n_tiles, e, LANES), lambda bi, s: (bi, 0, 0, 0))],
        out_specs=[pl.BlockSpec((None, e, 1, k), lambda bi, s: (bi, 0, 0, 0)),
                   pl.BlockSpec((None, e, 1, k), lambda bi, s: (bi, 0, 0, 0))],
        scratch_shapes=[pltpu.VMEM((LANES, LANES), F32), pltpu.VMEM((LANES, LANES), F32)])
    idx, gates = pl.pallas_call(
        functools.partial(_compact_kernel, segments=segments, rows_per_sample=rows_per_sample),
        grid_spec=grid_spec,
        out_shape=[jax.ShapeDtypeStruct((b, e, 1, k), jnp.int32), jax.ShapeDtypeStruct((b, e, 1, k), F32)],
        compiler_params=_params(("parallel",), 32),
        name="route_compact",
    )(starts, slots, a_tiles)
    return tab, idx, gates, starts


def _moe_block(x, hf, aff_t, segments, w_g, w_u, w_d, layer, mod_tab, g4, mod_next, g4_next, *, n_latent):
    b, sy = aff_t.shape[0], aff_t.shape[2]
    tab, idx, gates, starts = _route(aff_t, segments, sy)
    k = idx.shape[-1]
    idx_em = jnp.swapaxes(idx[:, :, 0, :], 0, 1).reshape(N_EXPERTS, b * k)
    gates_em = jnp.swapaxes(gates[:, :, 0, :], 0, 1).reshape(N_EXPERTS, b * k, 1)
    y = _moe(idx_em, hf.reshape(b * sy, hf.shape[-1]), w_g, w_u, w_d, layer, gates_em)
    y2d = y.reshape(N_EXPERTS * b * k, y.shape[-1])
    return _combine(starts, tab, y2d, x, mod_tab, g4, mod_next, g4_next, n_latent=n_latent)


_SCALE = HEAD_DIM ** -0.5 * LOG2E
_PLAN_AB = tuple([(None, _SCALE)] * 8 + [(None, 1.0)] * 2 + [(None, None)] * 2
                 + [("q", _SCALE)] * 8 + [("k", 1.0)] * 2 + [(None, None)] * 2)
_PLAN_C = tuple([(None, _SCALE)] * 16 + [(None, 1.0)] * 16 + [(None, None)] * 16)


def kernel(x, c, ctx, c_ctx, mod_w, mod_b, norm_g, ab_w_in, ab_w_out, ab_sink, ab_q_norm, ab_k_norm,
           dif_w_in, dif_w_out, dif_lambda, dif_subln, router_w, exp_w_gate, exp_w_up, exp_w_down):
    b = x.shape[0]
    d = D_MODEL
    cv = jnp.concatenate([c, c_ctx[None, :], jnp.zeros((8 - b - 1, d), F32)], axis=0)
    mods = _mods(cv, mod_w, mod_b).reshape(DEPTH, 8, 6, d)
    mod_tab = jnp.stack([jnp.broadcast_to(mods[:, b][:, None], (DEPTH, b, 6, d)), mods[:, :b]], axis=2)
    tables = _rope_tables()
    ones = jnp.ones((1, LANES), F32)
    wr = jnp.pad(router_w, ((0, 0), (0, 0), (0, LANES - N_EXPERTS))).astype(BF16)

    h0 = _norm_mod(x, ctx, norm_g[0], mod_tab[0])
    p0 = _matmul(h0.reshape(b * S_ROWS, d), _cast_bf16(ab_w_in[0]), BF16).reshape(b, S_ROWS, AB_IN)
    qkv0 = _prep(p0, tables, ab_q_norm[0][None, :], ab_k_norm[0][None, :], _PLAN_AB)
    sink = ab_sink[0]
    common_a = dict(q_col=0, k_col=8, v_col=10, groups=4, kv_heads=A_KV_HEADS)
    common_b = dict(q_col=12, k_col=20, v_col=22, groups=4, kv_heads=B_KV_HEADS)
    ctx_q = dict(q_row0=SEQ // CTX_LEN, n_q=1, tq=CTX_LEN, band=False)
    ya_l = _attn_oneshot(sink, qkv0, **common_a, q_row0=0, n_q=SEQ // 256, tq=256, band=True, use_sink=True)
    ya_c = _attn_oneshot(sink, qkv0, **common_a, **ctx_q, use_sink=True)
    yb_c = _attn_oneshot(sink, qkv0, **common_b, **ctx_q, use_sink=False)
    yb_l = _attn_flash(jnp.zeros((4, LANES), F32), ones, qkv0, q_col=3, k_col=20, v_col=22, streams=4,
                       shared_k=True, dv=HEAD_DIM, heads=B_KV_HEADS, tq=256, tk=2048, diff_scale=None)
    x1, hf0, aff0 = _outproj((ya_l, yb_l, ya_c, yb_c), (x, ctx), _cast_bf16(ab_w_out[0]), mod_tab[0], norm_g[0],
                             wr[0], split=True)
    cap_l = EC_FACTOR * SEQ // N_EXPERTS
    cap_c = EC_FACTOR * CTX_LEN // N_EXPERTS
    x2, h1 = _moe_block(x1, hf0, aff0, ((0, SEQ, cap_l), (SEQ, CTX_LEN, cap_c)),
                        exp_w_gate, exp_w_up, exp_w_down, 0, mod_tab[0], norm_g[0], mod_tab[1], norm_g[1],
                        n_latent=SEQ)

    p1 = _matmul(h1.reshape(b * S_ROWS, d), _cast_bf16(dif_w_in[0]), BF16).reshape(b, S_ROWS, C_IN)
    qkv1 = _prep(p1, tables, ones, ones, _PLAN_C)
    lambda_init = 0.8 - 0.6 * math.exp(-0.3 * 1)
    y1 = _attn_flash(dif_lambda[0], dif_subln[0][None, :], qkv1, q_col=0, k_col=8, v_col=16, streams=2,
                     shared_k=False, dv=C_V_DIM, heads=C_HEADS, tq=512, tk=2048,
                     diff_scale=(lambda_init, 1.0 - lambda_init))
    x3, hf1, aff1 = _outproj((y1,), (x2,), _cast_bf16(dif_w_out[0]), mod_tab[1], norm_g[1], wr[1], split=False)
    (out,) = _moe_block(x3, hf1, aff1, ((0, SEQ, cap_l),), exp_w_gate, exp_w_up, exp_w_down, 1,
                        mod_tab[1], norm_g[1], None, None, n_latent=SEQ)
    return out
```

```python
import functools
import math

import jax
import jax.numpy as jnp
from jax import lax
from jax.experimental import pallas as pl
from jax.experimental.pallas import tpu as pltpu

D_MODEL = 2048
SEQ = 4096
DEPTH = 2
GRID_W = 64
CTX_LEN = 256
HEAD_DIM = 128
ROPE_BASE = 10000.0
BLOCK = 128
WINDOW = 128
A_Q_HEADS = 8
A_KV_HEADS = 2
B_Q_HEADS = 8
B_KV_HEADS = 2
AB_IN = 3072
C_HEADS = 8
C_V_DIM = 256
C_IN = 6144
N_EXPERTS = 16
EC_FACTOR = 2
EPS = 1e-6
NEG_INF = -1e30

S_ROWS = SEQ + CTX_LEN
ROW_TILE = 256
LANES = 128
MIB = 1024 * 1024
LOG2E = math.log2(math.e)

F32 = jnp.float32
BF16 = jnp.bfloat16


def _params(sem, vmem_mib):
    return pltpu.CompilerParams(dimension_semantics=sem, vmem_limit_bytes=vmem_mib * MIB)


def _rms(x):
    return x * lax.rsqrt(jnp.mean(x * x, axis=-1, keepdims=True) + EPS)


def _dot(a, b):
    return jnp.dot(a, b, preferred_element_type=F32)


def _dot_nt(a, b):
    return lax.dot_general(a, b, (((1,), (1,)), ((), ())), preferred_element_type=F32)


def _mods_kernel(cv_ref, w_ref, b_ref, o_ref):
    cv = cv_ref[...]
    s = cv * jax.nn.sigmoid(cv)
    o_ref[...] = _dot(s.astype(BF16), w_ref[...].astype(BF16)) + b_ref[...]


def _mods(cv, mod_w, mod_b):
    tn = 1024
    n = mod_w.shape[-1]
    return pl.pallas_call(
        _mods_kernel,
        grid=(DEPTH, n // tn),
        in_specs=[
            pl.BlockSpec((8, D_MODEL), lambda l, j: (0, 0)),
            pl.BlockSpec((None, D_MODEL, tn), lambda l, j: (l, 0, j)),
            pl.BlockSpec((None, 1, tn), lambda l, j: (l, 0, j)),
        ],
        out_specs=pl.BlockSpec((None, 8, tn), lambda l, j: (l, 0, j)),
        out_shape=jax.ShapeDtypeStruct((DEPTH, 8, n), F32),
        compiler_params=_params(("parallel", "parallel"), 40),
        name="mods",
    )(cv, mod_w, mod_b.reshape(DEPTH, 1, n))


def _cast_kernel(x_ref, o_ref):
    o_ref[...] = x_ref[...].astype(o_ref.dtype)


def _cast_bf16(w):
    k, n = w.shape
    tk = 512
    return pl.pallas_call(
        _cast_kernel,
        grid=(k // tk,),
        in_specs=[pl.BlockSpec((tk, n), lambda i: (i, 0))],
        out_specs=pl.BlockSpec((tk, n), lambda i: (i, 0)),
        out_shape=jax.ShapeDtypeStruct((k, n), BF16),
        compiler_params=_params(("parallel",), 48),
        name="cast_bf16",
    )(w)


def _seg(i):
    return jnp.where(i < SEQ // ROW_TILE, 1, 0)


_LAT_TILES = SEQ // ROW_TILE


def _lat_row(bi, i):
    return (bi, jnp.minimum(i, _LAT_TILES - 1), 0)


def _ctx_row(bi, i):
    return (bi, 0, 0)


def _norm_mod_kernel(x_ref, ctx_ref, g_ref, mod_ref, h_ref):
    m = mod_ref[...]
    xv = jnp.where(pl.program_id(1) < _LAT_TILES, x_ref[...], ctx_ref[...])
    h = _rms(xv) * g_ref[0:1, :] * (1.0 + m[1:2, :]) + m[0:1, :]
    h_ref[...] = h.astype(h_ref.dtype)


def _norm_mod(x, ctx, g4, mod_tab):
    b, _, d = x.shape
    return pl.pallas_call(
        _norm_mod_kernel,
        grid=(b, S_ROWS // ROW_TILE),
        in_specs=[
            pl.BlockSpec((None, ROW_TILE, d), _lat_row),
            pl.BlockSpec((None, ROW_TILE, d), _ctx_row),
            pl.BlockSpec((4, d), lambda bi, i: (0, 0)),
            pl.BlockSpec((None, None, 6, d), lambda bi, i: (bi, _seg(i), 0, 0)),
        ],
        out_specs=pl.BlockSpec((None, ROW_TILE, d), lambda bi, i: (bi, i, 0)),
        out_shape=jax.ShapeDtypeStruct((b, S_ROWS, d), BF16),
        compiler_params=_params(("parallel", "parallel"), 32),
        name="norm_mod",
    )(x, ctx, g4, mod_tab)


def _mm_kernel(a_ref, w_ref, o_ref, w_sc):
    @pl.when(pl.program_id(1) == 0)
    def _():
        w_sc[...] = w_ref[...].astype(BF16)

    o_ref[...] = _dot(a_ref[...], w_sc[...]).astype(o_ref.dtype)


def _matmul(a, w, out_dtype):
    m, k = a.shape
    n = w.shape[1]
    tm, tn = 1088, 1024
    return pl.pallas_call(
        _mm_kernel,
        grid=(n // tn, m // tm),
        in_specs=[
            pl.BlockSpec((tm, k), lambda j, i: (i, 0)),
            pl.BlockSpec((k, tn), lambda j, i: (0, j)),
        ],
        out_specs=pl.BlockSpec((tm, tn), lambda j, i: (i, j)),
        out_shape=jax.ShapeDtypeStruct((m, n), out_dtype),
        scratch_shapes=[pltpu.VMEM((k, tn), BF16)],
        compiler_params=_params(("parallel", "arbitrary"), 48),
        name="in_proj",
    )(a, w)


def _rope_tables():
    n_rows = SEQ // GRID_W
    row = jnp.repeat(jnp.arange(n_rows), GRID_W).astype(F32)
    col = jnp.tile(jnp.arange(GRID_W), n_rows).astype(F32)
    n_freq = HEAD_DIM // 4
    inv = ROPE_BASE ** (-jnp.arange(n_freq, dtype=F32) / n_freq)
    ar = row[:, None] * inv
    ac = col[:, None] * inv
    cr, sr, cc, sc = jnp.cos(ar), jnp.sin(ar), jnp.cos(ac), jnp.sin(ac)
    z = jnp.zeros_like(sr)
    c = jnp.concatenate([cr, cr, cc, cc], axis=-1)
    sa = jnp.concatenate([-sr, z, -sc, z], axis=-1)
    sb = jnp.concatenate([z, sr, z, sc], axis=-1)
    pad = jnp.zeros((CTX_LEN, HEAD_DIM), F32)
    return (jnp.concatenate([c, pad + 1.0], axis=0), jnp.concatenate([sa, pad], axis=0),
            jnp.concatenate([sb, pad], axis=0))


def _prep_kernel(p_ref, c_ref, sa_ref, sb_ref, gq_ref, gk_ref, o_ref, *, plan):
    c, sa, sb = c_ref[...], sa_ref[...], sb_ref[...]
    for s, (src, norm, scale) in enumerate(plan):
        x = p_ref[:, src * LANES:(src + 1) * LANES].astype(F32)
        if norm is not None:
            x = _rms(x) * (gq_ref[...] if norm == "q" else gk_ref[...])
        r = x * c + pltpu.roll(x, 96, 1) * sa + pltpu.roll(x, 32, 1) * sb
        if scale != 1.0:
            r = r * scale
        o_ref[:, s * LANES:(s + 1) * LANES] = r.astype(o_ref.dtype)


def _prep(p, tables, gq, gk, plan):
    b, s, n = p.shape
    n_out = len(plan) * LANES
    tab_spec = pl.BlockSpec((ROW_TILE, LANES), lambda bi, i: (i, 0))
    g_spec = pl.BlockSpec((1, LANES), lambda bi, i: (0, 0))
    return pl.pallas_call(
        functools.partial(_prep_kernel, plan=plan),
        grid=(b, s // ROW_TILE),
        in_specs=[pl.BlockSpec((None, ROW_TILE, n), lambda bi, i: (bi, i, 0)),
                  tab_spec, tab_spec, tab_spec, g_spec, g_spec],
        out_specs=pl.BlockSpec((None, ROW_TILE, n_out), lambda bi, i: (bi, i, 0)),
        out_shape=jax.ShapeDtypeStruct((b, s, n_out), BF16),
        compiler_params=_params(("parallel", "parallel"), 32),
        name="qk_prep",
    )(p, *tables, gq, gk)


def _lane_tiles(s):
    return [s[:, j * LANES:(j + 1) * LANES] for j in range(s.shape[1] // LANES)]


def _stack_heads(q_ref, groups):
    return jnp.concatenate([q_ref[:, g * LANES:(g + 1) * LANES] for g in range(groups)], axis=0)


def _attn_oneshot_kernel(sink_ref, q_ref, k_ref, v_ref, o_ref, *, groups, tq, band, use_sink):
    h = pl.program_id(1)
    qi = pl.program_id(2)
    rows = groups * tq
    ctx0 = SEQ if band else 0
    q = _stack_heads(q_ref, groups)
    tiles = _lane_tiles(_dot_nt(q, k_ref[ctx0:ctx0 + CTX_LEN, :]))
    n_ctx_tiles = len(tiles)
    if band:
        nband = tq + 2 * BLOCK
        start = jnp.clip(qi * tq - BLOCK, 0, SEQ - nband)
        start = pl.multiple_of(start, BLOCK)
        row = lax.broadcasted_iota(jnp.int32, (rows, nband), 0)
        q_pos = qi * tq + (row & (tq - 1))
        k_pos = start + lax.broadcasted_iota(jnp.int32, (rows, nband), 1)
        valid = jnp.abs(q_pos - k_pos) <= WINDOW
        tiles += _lane_tiles(jnp.where(valid, _dot_nt(q, k_ref[pl.ds(start, nband), :]), NEG_INF))
    m = jnp.max(functools.reduce(jnp.maximum, tiles), axis=-1, keepdims=True)
    if use_sink:
        row1 = lax.broadcasted_iota(jnp.int32, (rows, 1), 0)
        sink = jnp.zeros((rows, 1), F32)
        for g in range(groups):
            sink = jnp.where(row1 >= g * tq, sink_ref[h * groups + g] * LOG2E, sink)
        m = jnp.maximum(m, sink)
    m_b = jnp.broadcast_to(m, (rows, LANES))
    ps = [jnp.exp2(t - m_b) for t in tiles]
    l = jnp.sum(functools.reduce(jnp.add, ps), axis=-1, keepdims=True)
    if use_sink:
        l = l + jnp.exp2(sink - m)
    p_c = jnp.concatenate([t.astype(BF16) for t in ps[:n_ctx_tiles]], axis=1)
    o = _dot(p_c, v_ref[ctx0:ctx0 + CTX_LEN, :])
    if band:
        p_b = jnp.concatenate([t.astype(BF16) for t in ps[n_ctx_tiles:]], axis=1)
        o = o + _dot(p_b, v_ref[pl.ds(start, nband), :])
    o = o / l
    for g in range(groups):
        o_ref[:, g * LANES:(g + 1) * LANES] = o[g * tq:(g + 1) * tq, :].astype(o_ref.dtype)


def _attn_oneshot(sink, qk, v, *, q_col, k_col, v_col, groups, kv_heads, q_row0, n_q, tq, band, use_sink):
    b = qk.shape[0]
    k_rows = S_ROWS if band else CTX_LEN
    k_row0 = 0 if band else SEQ // CTX_LEN
    gw = groups * LANES
    return pl.pallas_call(
        functools.partial(_attn_oneshot_kernel, groups=groups, tq=tq, band=band, use_sink=use_sink),
        grid=(b, kv_heads, n_q),
        in_specs=[
            pl.BlockSpec(memory_space=pltpu.SMEM),
            pl.BlockSpec((None, tq, gw), lambda bi, h, i: (bi, i + q_row0, q_col // groups + h)),
            pl.BlockSpec((None, k_rows, LANES), lambda bi, h, i: (bi, k_row0, k_col + h)),
            pl.BlockSpec((None, k_rows, LANES), lambda bi, h, i: (bi, k_row0, v_col + h)),
        ],
        out_specs=pl.BlockSpec((None, tq, gw), lambda bi, h, i: (bi, i, h)),
        out_shape=jax.ShapeDtypeStruct((b, n_q * tq, kv_heads * gw), BF16),
        compiler_params=_params(("parallel", "parallel", "parallel"), 32),
        name="attn_oneshot",
    )(sink, qk, qk, v)


def _attn_flash_kernel(lam_ref, subg_ref, q_ref, k_ref, v_ref, o_ref, q_sc, m_sc, l_sc, acc_sc,
                       *, streams, shared_k, tq, tk, diff_scale):
    dv = acc_sc.shape[1]
    for g in range(streams):
        q_sc[g * tq:(g + 1) * tq, :] = q_ref[:, g * LANES:(g + 1) * LANES]
    m_sc[...] = jnp.full(m_sc.shape, NEG_INF, F32)
    l_sc[...] = jnp.zeros(l_sc.shape, F32)
    acc_sc[...] = jnp.zeros(acc_sc.shape, F32)

    def chunk(off, size):
        kc = k_ref[pl.ds(off, size), :]
        if shared_k:
            s = _dot_nt(q_sc[...], kc)
        else:
            s = jnp.concatenate([_dot_nt(q_sc[g * tq:(g + 1) * tq, :], kc[:, g * LANES:(g + 1) * LANES])
                                 for g in range(streams)], axis=0)
        tiles = _lane_tiles(s)
        m_prev = m_sc[...]
        m_new = jnp.maximum(m_prev, jnp.max(functools.reduce(jnp.maximum, tiles), axis=-1, keepdims=True))
        alpha = jnp.exp2(m_prev - m_new)
        ps = [jnp.exp2(t - m_new) for t in tiles]
        m_sc[...] = m_new
        l_sc[...] = alpha * l_sc[...] + functools.reduce(jnp.add, ps)
        p = jnp.concatenate([t.astype(BF16) for t in ps], axis=1)
        pv = _dot(p, v_ref[pl.ds(off, size), :])
        acc_sc[...] = jnp.concatenate([alpha] * (dv // LANES), axis=1) * acc_sc[...] + pv

    def body(c, carry):
        chunk(pl.multiple_of(c * tk, LANES), tk)
        return carry

    if S_ROWS % tk == 0:
        lax.fori_loop(0, S_ROWS // tk, body, 0)
    else:
        chunk(SEQ, CTX_LEN)
        lax.fori_loop(0, SEQ // tk, body, 0)
    o = acc_sc[...] / jnp.sum(l_sc[...], axis=-1, keepdims=True)
    if diff_scale is None:
        for g in range(streams):
            o_ref[:, g * LANES:(g + 1) * LANES] = o[g * tq:(g + 1) * tq, :].astype(o_ref.dtype)
    else:
        lambda_init, out_scale = diff_scale
        lp = lam_ref[...]
        lam = (jnp.exp(jnp.sum(lp[0:1, :] * lp[1:2, :], axis=-1, keepdims=True))
               - jnp.exp(jnp.sum(lp[2:3, :] * lp[3:4, :], axis=-1, keepdims=True)) + lambda_init)
        o = o[0:tq, :] - lam * o[tq:2 * tq, :]
        o_ref[...] = (_rms(o) * subg_ref[...] * out_scale).astype(o_ref.dtype)


def _attn_flash(lam, subg, qk, v, *, q_col, k_col, v_col, streams, shared_k, dv, heads, tq, tk, diff_scale):
    b = qk.shape[0]
    qw = streams * LANES
    kw = LANES if shared_k else streams * LANES
    n_q = SEQ // tq
    ow = dv if diff_scale is not None else qw
    rows = streams * tq
    return pl.pallas_call(
        functools.partial(_attn_flash_kernel, streams=streams, shared_k=shared_k, tq=tq, tk=tk,
                          diff_scale=diff_scale),
        grid=(b, heads, n_q),
        in_specs=[
            pl.BlockSpec((4, LANES), lambda bi, h, i: (0, 0)),
            pl.BlockSpec((1, dv), lambda bi, h, i: (0, 0)),
            pl.BlockSpec((None, tq, qw), lambda bi, h, i: (bi, i, q_col + h)),
            pl.BlockSpec((None, S_ROWS, kw), lambda bi, h, i: (bi, 0, k_col + h)),
            pl.BlockSpec((None, S_ROWS, dv), lambda bi, h, i: (bi, 0, v_col + h)),
        ],
        out_specs=pl.BlockSpec((None, tq, ow), lambda bi, h, i: (bi, i, h)),
        out_shape=jax.ShapeDtypeStruct((b, SEQ, heads * ow), BF16),
        scratch_shapes=[pltpu.VMEM((rows, LANES), BF16), pltpu.VMEM((rows, LANES), F32),
                        pltpu.VMEM((rows, LANES), F32), pltpu.VMEM((rows, dv), F32)],
        compiler_params=_params(("parallel", "parallel", "parallel"), 48),
        name="attn_flash",
    )(lam, subg, qk, qk, v)


def _outproj_kernel(*refs, split):
    if split:
        ya_ref, yb_ref, yac_ref, ybc_ref, xl_ref, xc_ref = refs[:6]
        lat = pl.program_id(1) < _LAT_TILES
        y_in = jnp.concatenate([jnp.where(lat, ya_ref[...], yac_ref[...]),
                                jnp.where(lat, yb_ref[...], ybc_ref[...])], axis=1)
        x_in = jnp.where(lat, xl_ref[...], xc_ref[...])
        refs = refs[6:]
    else:
        y_in, x_in = refs[0][...], refs[1][...]
        refs = refs[2:]
    w_ref, mod_ref, g_ref, wr_ref, xo_ref, hf_ref, aff_ref = refs
    m = mod_ref[...]
    g = g_ref[...]
    y = _dot(y_in, w_ref[...])
    x1 = x_in + m[2:3, :] * (_rms(y) * g[1:2, :])
    xo_ref[...] = x1
    hf = _rms(x1) * g[2:3, :] * (1.0 + m[4:5, :]) + m[3:4, :]
    hf_ref[...] = hf
    logits = _dot(hf.astype(BF16), wr_ref[...]).T[0:N_EXPERTS, :]
    e = jnp.exp(logits - jnp.max(logits, axis=0, keepdims=True))
    aff_ref[...] = e / jnp.sum(e, axis=0, keepdims=True)


def _outproj(ys, xs, w_bf, mod_tab, g4, wr_bf, *, split):
    b, d = xs[0].shape[0], xs[0].shape[2]
    sy = S_ROWS if split else ys[0].shape[1]
    seg = _seg if split else (lambda i: 1)
    row = lambda bi, i: (bi, i, 0)
    if split:
        half = pl.BlockSpec((None, ROW_TILE, d // 2), _lat_row)
        half_c = pl.BlockSpec((None, ROW_TILE, d // 2), _ctx_row)
        data_specs = [half, half, half_c, half_c,
                      pl.BlockSpec((None, ROW_TILE, d), _lat_row), pl.BlockSpec((None, ROW_TILE, d), _ctx_row)]
    else:
        data_specs = [pl.BlockSpec((None, ROW_TILE, d), row), pl.BlockSpec((None, ROW_TILE, d), row)]
    out_f32 = jax.ShapeDtypeStruct((b, sy, d), F32)
    return pl.pallas_call(
        functools.partial(_outproj_kernel, split=split),
        grid=(b, sy // ROW_TILE),
        in_specs=data_specs + [
            pl.BlockSpec((d, d), lambda bi, i: (0, 0)),
            pl.BlockSpec((None, None, 6, d), lambda bi, i: (bi, seg(i), 0, 0)),
            pl.BlockSpec((4, d), lambda bi, i: (0, 0)),
            pl.BlockSpec((d, LANES), lambda bi, i: (0, 0)),
        ],
        out_specs=[pl.BlockSpec((None, ROW_TILE, d), row), pl.BlockSpec((None, ROW_TILE, d), row),
                   pl.BlockSpec((None, N_EXPERTS, ROW_TILE), lambda bi, i: (bi, 0, i))],
        out_shape=[out_f32, out_f32, jax.ShapeDtypeStruct((b, N_EXPERTS, sy), F32)],
        compiler_params=_params(("parallel", "parallel"), 48),
        name="out_proj",
    )(*ys, *xs, w_bf, mod_tab, g4, wr_bf)


def _row_copy(h_hbm, stage_sc, sem, src_row, dst_row, n_rows):
    return pltpu.make_async_copy(h_hbm.at[pl.ds(src_row, n_rows), :], stage_sc.at[pl.ds(dst_row, n_rows), :], sem)


def _moe_kernel(idx_ref, h_hbm, wg_ref, wu_ref, wd_ref, gate_ref, o_ref, stage_sc, x_sc, hmid_sc, sem, *, n_up, n_down):
    e = pl.program_id(0)
    s = pl.program_id(1)
    last = (e == pl.num_programs(0) - 1) & (s == n_up + n_down - 1)
    m = x_sc.shape[0]
    rows_per_step = m // n_up

    def issue_one(expert, row):
        _row_copy(h_hbm, stage_sc, sem, idx_ref[expert, row], row, 1).start()

    @pl.when(s == 0)
    def _():
        @pl.when(e == 0)
        def _():
            def body(i, carry):
                issue_one(0, i)
                return carry

            lax.fori_loop(0, m, body, 0, unroll=8)

        _row_copy(h_hbm, stage_sc, sem, 0, 0, m).wait()
        x_sc[...] = stage_sc[...].astype(BF16)

    @pl.when(s < n_up)
    def _():
        for i in range(rows_per_step):
            issue_one(e + 1, s * rows_per_step + i)
        x = x_sc[...]
        a = _dot(x, wg_ref[...].astype(BF16))
        u = _dot(x, wu_ref[...].astype(BF16))
        hmid_sc[s] = (a * jax.nn.sigmoid(a) * u).astype(BF16)

    @pl.when(s >= n_up)
    def _():
        hmid = jnp.concatenate([hmid_sc[f] for f in range(n_up)], axis=1)
        y = _dot(hmid, wd_ref[...].astype(BF16))
        o_ref[...] = (y * gate_ref[...]).astype(o_ref.dtype)

    @pl.when(last)
    def _():
        _row_copy(h_hbm, stage_sc, sem, 0, 0, m).wait()


def _moe(idx, h, w_g, w_u, w_d, layer, gates):
    e, m = idx.shape
    d = w_g.shape[2]
    f_dim = w_g.shape[-1]
    t_up, t_down = 256, 1024
    n_up, n_down = f_dim // t_up, d // t_down
    idx = jnp.concatenate([idx, jnp.zeros((1, m), idx.dtype)], axis=0)
    up_tile = lambda ei, s, idx_ref: (layer, ei, 0, jnp.minimum(s, n_up - 1))
    down_tile = lambda s: jnp.maximum(s - n_up, 0)
    grid_spec = pltpu.PrefetchScalarGridSpec(
        num_scalar_prefetch=1,
        grid=(e, n_up + n_down),
        in_specs=[
            pl.BlockSpec(memory_space=pl.ANY),
            pl.BlockSpec((None, None, d, t_up), up_tile),
            pl.BlockSpec((None, None, d, t_up), up_tile),
            pl.BlockSpec((None, None, f_dim, t_down), lambda ei, s, idx_ref: (layer, ei, 0, down_tile(s))),
            pl.BlockSpec((None, m, 1), lambda ei, s, idx_ref: (ei, 0, 0)),
        ],
        out_specs=pl.BlockSpec((None, m, t_down), lambda ei, s, idx_ref: (ei, 0, down_tile(s))),
        scratch_shapes=[pltpu.VMEM((m, d), F32), pltpu.VMEM((m, d), BF16), pltpu.VMEM((n_up, m, t_up), BF16),
                        pltpu.SemaphoreType.DMA(())],
    )
    return pl.pallas_call(
        functools.partial(_moe_kernel, n_up=n_up, n_down=n_down),
        grid_spec=grid_spec,
        out_shape=jax.ShapeDtypeStruct((e, m, d), BF16),
        compiler_params=_params(("arbitrary", "arbitrary"), 58),
        name="moe_experts",
    )(idx, h, w_g, w_u, w_d, gates)


_BF16_ROWS = 16
_WIN = LANES + _BF16_ROWS


def _combine_kernel(starts_ref, tab_ref, y_hbm, x_ref, mod_ref, g_ref, *rest, cap_t, n_latent, has_next):
    if has_next:
        modn_ref, gn_ref, xo_ref, h_ref, buf, sem = rest
    else:
        xo_ref, buf, sem = rest
    b = pl.program_id(0)
    j = pl.program_id(1)
    nb = pl.num_programs(0)
    n_tiles = pl.num_programs(1)
    t = b * n_tiles + j
    cur = lax.rem(t, 2)

    def window_start(bb, jj, e):
        s = starts_ref[bb * N_EXPERTS + e, jj]
        return jnp.minimum(s // _BF16_ROWS * _BF16_ROWS, cap_t - _WIN)

    def window_copy(row, e, slot, n_rows):
        return pltpu.make_async_copy(y_hbm.at[pl.ds(row, n_rows), :], buf.at[slot, pl.ds(e * _WIN, n_rows), :],
                                     sem.at[slot])

    def fetch(bb, jj, slot):
        for e in range(N_EXPERTS):
            row = (e * nb + bb) * cap_t + window_start(bb, jj, e)
            window_copy(pl.multiple_of(row, _BF16_ROWS), e, slot, _WIN).start()

    @pl.when(t == 0)
    def _():
        fetch(0, 0, 0)

    @pl.when(t + 1 < nb * n_tiles)
    def _():
        t1 = t + 1
        fetch(t1 // n_tiles, lax.rem(t1, n_tiles), 1 - cur)

    window_copy(0, 0, cur, N_EXPERTS * _WIN).wait()

    tab = tab_ref[...]
    lane_f = lax.broadcasted_iota(jnp.int32, (LANES, LANES), 1).astype(F32)
    col_of = []
    for e in range(N_EXPERTS):
        slot_e = tab[:, e:e + 1]
        rel = slot_e - window_start(b, j, e).astype(F32) + float(e * _WIN)
        col_of.append(jnp.broadcast_to(jnp.where(slot_e >= 0.0, rel, -1.0), (LANES, LANES)))
    tiles = []
    for i in range(N_EXPERTS * _WIN // LANES):
        cols = lane_f + float(i * LANES)
        e_lo, e_hi = i * LANES // _WIN, (i * LANES + LANES - 1) // _WIN
        hit = col_of[e_lo] == cols
        if e_hi != e_lo:
            hit = hit | (col_of[e_hi] == cols)
        tiles.append(jnp.where(hit, 1.0, 0.0).astype(BF16))
    moe = _dot(jnp.concatenate(tiles, axis=1), buf[cur])

    is_lat = (j * LANES + lax.broadcasted_iota(jnp.int32, (LANES, 1), 0)) < n_latent

    def mod(ref, kk):
        return jnp.where(is_lat, ref[1, kk:kk + 1, :], ref[0, kk:kk + 1, :])

    x2 = x_ref[...] + mod(mod_ref, 5) * (_rms(moe) * g_ref[3:4, :])
    xo_ref[...] = x2
    if has_next:
        h_ref[...] = (_rms(x2) * gn_ref[0:1, :] * (1.0 + mod(modn_ref, 1)) + mod(modn_ref, 0)).astype(h_ref.dtype)


def _combine(starts, tab, y, x, mod_tab, g4, mod_next, g4_next, *, n_latent):
    b, sx, d = x.shape
    cap_t = y.shape[0] // (b * N_EXPERTS)
    has_next = mod_next is not None
    row = pl.BlockSpec((None, LANES, d), lambda bi, i, s: (bi, i, 0))
    mod_spec = pl.BlockSpec((None, 2, 6, d), lambda bi, i, s: (bi, 0, 0, 0))
    g_spec = pl.BlockSpec((4, d), lambda bi, i, s: (0, 0))
    in_specs = [pl.BlockSpec((None, LANES, LANES), lambda bi, i, s: (bi, i, 0)),
                pl.BlockSpec(memory_space=pl.ANY), row, mod_spec, g_spec]
    ins = [tab, y, x, mod_tab, g4]
    out_specs, out_shape = [row], [jax.ShapeDtypeStruct((b, sx, d), F32)]
    if has_next:
        in_specs += [mod_spec, g_spec]
        ins += [mod_next, g4_next]
        out_specs.append(row)
        out_shape.append(jax.ShapeDtypeStruct((b, sx, d), BF16))
    grid_spec = pltpu.PrefetchScalarGridSpec(
        num_scalar_prefetch=1, grid=(b, sx // LANES), in_specs=in_specs, out_specs=out_specs,
        scratch_shapes=[pltpu.VMEM((2, N_EXPERTS * _WIN, d), BF16), pltpu.SemaphoreType.DMA((2,))])
    return pl.pallas_call(
        functools.partial(_combine_kernel, cap_t=cap_t, n_latent=n_latent, has_next=has_next),
        grid_spec=grid_spec, out_shape=out_shape,
        compiler_params=_params(("arbitrary", "arbitrary"), 48), name="moe_combine",
    )(starts, *ins)


def _prefix_count(mask_tiles, tri):
    off = jnp.zeros((N_EXPERTS, 1), F32)
    out, before = [], []
    for mt in mask_tiles:
        before.append(off)
        w = _dot(jnp.where(mt, 1.0, 0.0).astype(BF16), tri) + off
        out.append(w)
        off = w[:, LANES - 1:LANES]
    return out, before


_MIN_EXP = -160.0
_EXP_STEPS = 8
_VALUE_STEPS = 40


def _route_kernel(aff_ref, tab_ref, starts_ref, slot_ref, atile_ref, *, segments):
    r_i = lax.broadcasted_iota(jnp.int32, (LANES, LANES), 0)
    c_i = lax.broadcasted_iota(jnp.int32, (LANES, LANES), 1)
    tri = jnp.where(r_i <= c_i, 1.0, 0.0).astype(BF16)
    shape = (N_EXPERTS, LANES)
    lane_e = lax.broadcasted_iota(jnp.int32, shape, 1)
    starts = jnp.zeros(shape, F32)
    base = 0
    tile0 = 0
    for t0, n, cap in segments:
        nt = n // LANES
        a_t = [aff_ref[:, t0 + j * LANES:t0 + (j + 1) * LANES] for j in range(nt)]

        def count_ge(v, a_t=a_t):
            v_b = jnp.broadcast_to(v, shape)
            cnt = functools.reduce(jnp.add, [jnp.where(t >= v_b, 1.0, 0.0) for t in a_t])
            return jnp.sum(cnt, axis=1, keepdims=True)

        e_lo = jnp.full((N_EXPERTS, 1), _MIN_EXP, F32)
        e_hi = jnp.full((N_EXPERTS, 1), 1.0, F32)
        lo = jnp.zeros((N_EXPERTS, 1), F32)
        hi = jnp.full((N_EXPERTS, 1), 2.0, F32)
        for _ in range(_EXP_STEPS):
            e_mid = jnp.floor((e_lo + e_hi) * 0.5)
            v = jnp.exp2(e_mid)
            ok = count_ge(v) >= cap
            e_lo, lo = jnp.where(ok, e_mid, e_lo), jnp.where(ok, v, lo)
            e_hi, hi = jnp.where(ok, e_hi, e_mid), jnp.where(ok, hi, v)
        for _ in range(_VALUE_STEPS):
            v = (lo + hi) * 0.5
            ok = count_ge(v) >= cap
            lo, hi = jnp.where(ok, v, lo), jnp.where(ok, hi, v)
        lo_b = jnp.broadcast_to(lo, shape)
        hi_b = jnp.broadcast_to(hi, shape)
        gt = [t >= hi_b for t in a_t]
        eq = [(t >= lo_b) & (t < hi_b) for t in a_t]
        need = jnp.broadcast_to(cap - count_ge(hi), shape)
        eq_rank, _ = _prefix_count(eq, tri)
        sel = [g | (q & (r <= need)) for g, q, r in zip(gt, eq, eq_rank)]
        pos, before = _prefix_count(sel, tri)
        for j in range(nt):
            starts = jnp.where(lane_e == tile0 + j, before[j] + float(base), starts)
            slot = jnp.where(sel[j], pos[j] + (base - 1.0), -1.0)
            slot_ref[tile0 + j] = slot
            atile_ref[tile0 + j] = a_t[j]
            padded =jnp.concatenate([slot, jnp.full((LANES - N_EXPERTS, LANES), -1.0, F32)], axis=0)
            tab_ref[t0 + j * LANES:t0 + (j + 1) * LANES, :] = padded.T
        base += cap
        tile0 += nt
    starts_ref[...] = starts


def _compact_kernel(starts_ref, slot_ref, aff_ref, idx_ref, gate_ref, acc_i, acc_g, *, segments, rows_per_sample):
    bi = pl.program_id(0)
    row0 = (bi * rows_per_sample).astype(F32)
    r_i = lax.broadcasted_iota(jnp.int32, (LANES, LANES), 0)
    lane1 = lax.broadcasted_iota(jnp.int32, (1, LANES), 1)
    base = 0
    tile0 = 0
    for t0, n, cap in segments:
        nt = n // LANES

        def expert_body(e, carry, t0=t0, nt=nt, cap=cap, base=base, tile0=tile0):
            srow = bi * N_EXPERTS + e
            n_chunks = -(-cap // LANES)

            def scan(j, bounds):
                first = starts_ref[srow, tile0 + j]
                nxt = jnp.where(j + 1 < nt, starts_ref[srow, tile0 + jnp.minimum(j + 1, nt - 1)], base + cap)
                out = []
                for c in range(n_chunks):
                    lo = base + c * LANES
                    out.append(bounds[2 * c] + (nxt <= lo).astype(jnp.int32))
                    out.append(bounds[2 * c + 1] + (first < lo + LANES).astype(jnp.int32))
                return tuple(out)

            bounds = lax.fori_loop(0, nt, scan, (jnp.int32(0),) * (2 * n_chunks), unroll=4)
            for c in range(n_chunks):
                lo = base + c * LANES
                j_lo, j_hi = bounds[2 * c], bounds[2 * c + 1]
                slot_id = (r_i + lo).astype(F32)
                acc_i[...] = jnp.zeros((LANES, LANES), F32)
                acc_g[...] = jnp.zeros((LANES, LANES), F32)

                def tile_body(j, carry2):
                    pe = jnp.broadcast_to(slot_ref[tile0 + j, pl.ds(e, 1), :], (LANES, LANES))
                    ae = jnp.broadcast_to(aff_ref[tile0 + j, pl.ds(e, 1), :], (LANES, LANES))
                    tok = jnp.broadcast_to((lane1 + (t0 + j * LANES)).astype(F32) + row0, (LANES, LANES))
                    hit = pe == slot_id
                    acc_i[...] += jnp.where(hit, tok, 0.0)
                    acc_g[...] += jnp.where(hit, ae, 0.0)
                    return carry2

                lax.fori_loop(j_lo, j_hi, tile_body, 0)
                w = min(LANES, cap - c * LANES)
                col_i = jnp.broadcast_to(jnp.sum(acc_i[...], axis=1, keepdims=True), (LANES, LANES))
                col_g = jnp.broadcast_to(jnp.sum(acc_g[...], axis=1, keepdims=True), (LANES, LANES))
                idx_ref[e, :, lo:lo + w] = col_i.T[0:1, 0:w].astype(jnp.int32)
                gate_ref[e, :, lo:lo + w] = col_g.T[0:1, 0:w]
            return carry

        lax.fori_loop(0, N_EXPERTS, expert_body, 0)
        base += cap
        tile0 += nt


def _route(aff_t, segments, rows_per_sample):
    b, e, sy = aff_t.shape
    k = sum(cap for _, _, cap in segments)
    n_tiles = sum(n // LANES for _, n, _ in segments)
    tiles_spec = pl.BlockSpec((None, n_tiles, e, LANES), lambda bi: (bi, 0, 0, 0))
    tiles_shape = jax.ShapeDtypeStruct((b, n_tiles, e, LANES), F32)
    tab, starts, slots, a_tiles = pl.pallas_call(
        functools.partial(_route_kernel, segments=segments),
        grid=(b,),
        in_specs=[pl.BlockSpec((None, e, sy), lambda bi: (bi, 0, 0))],
        out_specs=[pl.BlockSpec((None, sy, LANES), lambda bi: (bi, 0, 0)),
                   pl.BlockSpec((None, e, LANES), lambda bi: (bi, 0, 0)), tiles_spec, tiles_spec],
        out_shape=[jax.ShapeDtypeStruct((b, sy, LANES), F32), jax.ShapeDtypeStruct((b, e, LANES), F32),
                   tiles_shape, tiles_shape],
        compiler_params=_params(("parallel",), 32),
        name="route_topk",
    )(aff_t)
    starts = starts.astype(jnp.int32).reshape(b * e, LANES)
    grid_spec = pltpu.PrefetchScalarGridSpec(
        num_scalar_prefetch=1, grid=(b,),
        in_specs=[pl.BlockSpec((None, n_tiles, e, LANES), lambda bi, s: (bi, 0, 0, 0)),
                  pl.BlockSpec((None, n_tiles, e, LANES), lambda bi, s: (bi, 0, 0, 0))],
        out_specs=[pl.BlockSpec((None, e, 1, k), lambda bi, s: (bi, 0, 0, 0)),
                   pl.BlockSpec((None, e, 1, k), lambda bi, s: (bi, 0, 0, 0))],
        scratch_shapes=[pltpu.VMEM((LANES, LANES), F32), pltpu.VMEM((LANES, LANES), F32)])
    idx, gates = pl.pallas_call(
        functools.partial(_compact_kernel, segments=segments, rows_per_sample=rows_per_sample),
        grid_spec=grid_spec,
        out_shape=[jax.ShapeDtypeStruct((b, e, 1, k), jnp.int32), jax.ShapeDtypeStruct((b, e, 1, k), F32)],
        compiler_params=_params(("parallel",), 32),
        name="route_compact",
    )(starts, slots, a_tiles)
    return tab, idx, gates, starts


def _moe_block(x, hf, aff_t, segments, w_g, w_u, w_d, layer, mod_tab, g4, mod_next, g4_next, *, n_latent):
    b, sy = aff_t.shape[0], aff_t.shape[2]
    tab, idx, gates, starts = _route(aff_t, segments, sy)
    k = idx.shape[-1]
    idx_em = jnp.swapaxes(idx[:, :, 0, :], 0, 1).reshape(N_EXPERTS, b * k)
    gates_em = jnp.swapaxes(gates[:, :, 0, :], 0, 1).reshape(N_EXPERTS, b * k, 1)
    y = _moe(idx_em, hf.reshape(b * sy, hf.shape[-1]), w_g, w_u, w_d, layer, gates_em)
    y2d = y.reshape(N_EXPERTS * b * k, y.shape[-1])
    return _combine(starts, tab, y2d, x, mod_tab, g4, mod_next, g4_next, n_latent=n_latent)


_SCALE = HEAD_DIM ** -0.5 * LOG2E
_PLAN_AB = tuple([(s, None, _SCALE) for s in range(0, 8)] + [(s, "q", _SCALE) for s in range(12, 20)]
                 + [(s, None, 1.0) for s in range(8, 10)] + [(s, "k", 1.0) for s in range(20, 22)])
_PLAN_C = tuple([(s, None, _SCALE) for s in range(0, 16)] + [(s, None, 1.0) for s in range(16, 32)])


def kernel(x, c, ctx, c_ctx, mod_w, mod_b, norm_g, ab_w_in, ab_w_out, ab_sink, ab_q_norm, ab_k_norm,
           dif_w_in, dif_w_out, dif_lambda, dif_subln, router_w, exp_w_gate, exp_w_up, exp_w_down):
    b = x.shape[0]
    d = D_MODEL
    cv = jnp.concatenate([c, c_ctx[None, :], jnp.zeros((8 - b - 1, d), F32)], axis=0)
    mods = _mods(cv, mod_w, mod_b).reshape(DEPTH, 8, 6, d)
    mod_tab = jnp.stack([jnp.broadcast_to(mods[:, b][:, None], (DEPTH, b, 6, d)), mods[:, :b]], axis=2)
    tables = _rope_tables()
    ones = jnp.ones((1, LANES), F32)
    wr = jnp.pad(router_w, ((0, 0), (0, 0), (0, LANES - N_EXPERTS))).astype(BF16)

    h0 = _norm_mod(x, ctx, norm_g[0], mod_tab[0])
    p0 = _matmul(h0.reshape(b * S_ROWS, d), ab_w_in[0], BF16).reshape(b, S_ROWS, AB_IN)
    qk0 = _prep(p0, tables, ab_q_norm[0][None, :], ab_k_norm[0][None, :], _PLAN_AB)
    sink = ab_sink[0]
    common_a = dict(q_col=0, k_col=16, v_col=10, groups=4, kv_heads=A_KV_HEADS)
    common_b = dict(q_col=8, k_col=18, v_col=22, groups=4, kv_heads=B_KV_HEADS)
    ctx_q = dict(q_row0=SEQ // CTX_LEN, n_q=1, tq=CTX_LEN, band=False)
    ya_l = _attn_oneshot(sink, qk0, p0, **common_a, q_row0=0, n_q=SEQ // 256, tq=256, band=True, use_sink=True)
    ya_c = _attn_oneshot(sink, qk0, p0, **common_a, **ctx_q, use_sink=True)
    yb_c = _attn_oneshot(sink, qk0, p0, **common_b, **ctx_q, use_sink=False)
    yb_l = _attn_flash(jnp.zeros((4, LANES), F32), ones, qk0, p0, q_col=2, k_col=18, v_col=22, streams=4,
                       shared_k=True, dv=HEAD_DIM, heads=B_KV_HEADS, tq=256, tk=2048, diff_scale=None)
    x1, hf0, aff0 = _outproj((ya_l, yb_l, ya_c, yb_c), (x, ctx), _cast_bf16(ab_w_out[0]), mod_tab[0], norm_g[0],
                             wr[0], split=True)
    cap_l = EC_FACTOR * SEQ // N_EXPERTS
    cap_c = EC_FACTOR * CTX_LEN // N_EXPERTS
    x2, h1 = _moe_block(x1, hf0, aff0, ((0, SEQ, cap_l), (SEQ, CTX_LEN, cap_c)),
                        exp_w_gate, exp_w_up, exp_w_down, 0, mod_tab[0], norm_g[0], mod_tab[1], norm_g[1],
                        n_latent=SEQ)

    p1 = _matmul(h1.reshape(b * S_ROWS, d), dif_w_in[0], BF16).reshape(b, S_ROWS, C_IN)
    qk1 = _prep(p1, tables, ones, ones, _PLAN_C)
    lambda_init = 0.8 - 0.6 * math.exp(-0.3 * 1)
    y1 = _attn_flash(dif_lambda[0], dif_subln[0][None, :], qk1, p1, q_col=0, k_col=8, v_col=16, streams=2,
                     shared_k=False, dv=C_V_DIM, heads=C_HEADS, tq=512, tk=2048,
                     diff_scale=(lambda_init, 1.0 - lambda_init))
    x3, hf1, aff1 = _outproj((y1,), (x2,), _cast_bf16(dif_w_out[0]), mod_tab[1], norm_g[1], wr[1], split=False)
    (out,) = _moe_block(x3, hf1, aff1, ((0, SEQ, cap_l),), exp_w_gate, exp_w_up, exp_w_down, 1,
                        mod_tab[1], norm_g[1], None, None, n_latent=SEQ)
    return out
```

```python
import functools
import math

import jax
import jax.numpy as jnp
from jax import lax
from jax.experimental import pallas as pl
from jax.experimental.pallas import tpu as pltpu

D_MODEL = 2048
SEQ = 4096
DEPTH = 2
GRID_W = 64
CTX_LEN = 256
HEAD_DIM = 128
ROPE_BASE = 10000.0
BLOCK = 128
WINDOW = 128
A_Q_HEADS = 8
A_KV_HEADS = 2
B_Q_HEADS = 8
B_KV_HEADS = 2
AB_IN = 3072
C_HEADS = 8
C_V_DIM = 256
C_IN = 6144
N_EXPERTS = 16
EC_FACTOR = 2
EPS = 1e-6
NEG_INF = -1e30

S_ROWS = SEQ + CTX_LEN
ROW_TILE = 256
LANES = 128
MIB = 1024 * 1024
LOG2E = math.log2(math.e)

F32 = jnp.float32
BF16 = jnp.bfloat16


def _params(sem, vmem_mib):
    return pltpu.CompilerParams(dimension_semantics=sem, vmem_limit_bytes=vmem_mib * MIB)


def _rms(x):
    return x * lax.rsqrt(jnp.mean(x * x, axis=-1, keepdims=True) + EPS)


def _dot(a, b):
    return jnp.dot(a, b, preferred_element_type=F32)


def _dot_nt(a, b):
    return lax.dot_general(a, b, (((1,), (1,)), ((), ())), preferred_element_type=F32)


def _mods_kernel(cv_ref, w_ref, b_ref, o_ref):
    cv = cv_ref[...]
    s = cv * jax.nn.sigmoid(cv)
    o_ref[...] = _dot(s.astype(BF16), w_ref[...].astype(BF16)) + b_ref[...]


def _mods(cv, mod_w, mod_b):
    tn = 1024
    n = mod_w.shape[-1]
    return pl.pallas_call(
        _mods_kernel,
        grid=(DEPTH, n // tn),
        in_specs=[
            pl.BlockSpec((8, D_MODEL), lambda l, j: (0, 0)),
            pl.BlockSpec((None, D_MODEL, tn), lambda l, j: (l, 0, j)),
            pl.BlockSpec((None, 1, tn), lambda l, j: (l, 0, j)),
        ],
        out_specs=pl.BlockSpec((None, 8, tn), lambda l, j: (l, 0, j)),
        out_shape=jax.ShapeDtypeStruct((DEPTH, 8, n), F32),
        compiler_params=_params(("parallel", "parallel"), 40),
        name="mods",
    )(cv, mod_w, mod_b.reshape(DEPTH, 1, n))


def _cast_kernel(x_ref, o_ref):
    o_ref[...] = x_ref[...].astype(o_ref.dtype)


def _cast_bf16(w):
    k, n = w.shape
    tk = 512
    return pl.pallas_call(
        _cast_kernel,
        grid=(k // tk,),
        in_specs=[pl.BlockSpec((tk, n), lambda i: (i, 0))],
        out_specs=pl.BlockSpec((tk, n), lambda i: (i, 0)),
        out_shape=jax.ShapeDtypeStruct((k, n), BF16),
        compiler_params=_params(("parallel",), 48),
        name="cast_bf16",
    )(w)


def _seg(i):
    return jnp.where(i < SEQ // ROW_TILE, 1, 0)


_LAT_TILES = SEQ // ROW_TILE


def _lat_row(bi, i):
    return (bi, jnp.minimum(i, _LAT_TILES - 1), 0)


def _ctx_row(bi, i):
    return (bi, 0, 0)


def _norm_mod_kernel(x_ref, ctx_ref, g_ref, mod_ref, h_ref):
    m = mod_ref[...]
    xv = jnp.where(pl.program_id(1) < _LAT_TILES, x_ref[...], ctx_ref[...])
    h = _rms(xv) * g_ref[0:1, :] * (1.0 + m[1:2, :]) + m[0:1, :]
    h_ref[...] = h.astype(h_ref.dtype)


def _norm_mod(x, ctx, g4, mod_tab):
    b, _, d = x.shape
    return pl.pallas_call(
        _norm_mod_kernel,
        grid=(b, S_ROWS // ROW_TILE),
        in_specs=[
            pl.BlockSpec((None, ROW_TILE, d), _lat_row),
            pl.BlockSpec((None, ROW_TILE, d), _ctx_row),
            pl.BlockSpec((4, d), lambda bi, i: (0, 0)),
            pl.BlockSpec((None, None, 6, d), lambda bi, i: (bi, _seg(i), 0, 0)),
        ],
        out_specs=pl.BlockSpec((None, ROW_TILE, d), lambda bi, i: (bi, i, 0)),
        out_shape=jax.ShapeDtypeStruct((b, S_ROWS, d), BF16),
        compiler_params=_params(("parallel", "parallel"), 32),
        name="norm_mod",
    )(x, ctx, g4, mod_tab)


def _mm_kernel(a_ref, w_ref, o_ref, w_sc):
    @pl.when(pl.program_id(1) == 0)
    def _():
        w_sc[...] = w_ref[...].astype(BF16)

    o_ref[...] = _dot(a_ref[...], w_sc[...]).astype(o_ref.dtype)


def _matmul(a, w, out_dtype):
    m, k = a.shape
    n = w.shape[1]
    tm, tn = 1088, 1024
    return pl.pallas_call(
        _mm_kernel,
        grid=(n // tn, m // tm),
        in_specs=[
            pl.BlockSpec((tm, k), lambda j, i: (i, 0)),
            pl.BlockSpec((k, tn), lambda j, i: (0, j)),
        ],
        out_specs=pl.BlockSpec((tm, tn), lambda j, i: (i, j)),
        out_shape=jax.ShapeDtypeStruct((m, n), out_dtype),
        scratch_shapes=[pltpu.VMEM((k, tn), BF16)],
        compiler_params=_params(("parallel", "arbitrary"), 48),
        name="in_proj",
    )(a, w)


def _rope_tables():
    n_rows = SEQ // GRID_W
    row = jnp.repeat(jnp.arange(n_rows), GRID_W).astype(F32)
    col = jnp.tile(jnp.arange(GRID_W), n_rows).astype(F32)
    n_freq = HEAD_DIM // 4
    inv = ROPE_BASE ** (-jnp.arange(n_freq, dtype=F32) / n_freq)
    ar = row[:, None] * inv
    ac = col[:, None] * inv
    cr, sr, cc, sc = jnp.cos(ar), jnp.sin(ar), jnp.cos(ac), jnp.sin(ac)
    z = jnp.zeros_like(sr)
    c = jnp.concatenate([cr, cr, cc, cc], axis=-1)
    sa = jnp.concatenate([-sr, z, -sc, z], axis=-1)
    sb = jnp.concatenate([z, sr, z, sc], axis=-1)
    pad = jnp.zeros((CTX_LEN, HEAD_DIM), F32)
    return (jnp.concatenate([c, pad + 1.0], axis=0), jnp.concatenate([sa, pad], axis=0),
            jnp.concatenate([sb, pad], axis=0))


def _prep_kernel(p_ref, c_ref, sa_ref, sb_ref, gq_ref, gk_ref, o_ref, *, plan):
    c, sa, sb = c_ref[...], sa_ref[...], sb_ref[...]
    for s, (src, norm, scale) in enumerate(plan):
        x = p_ref[:, src * LANES:(src + 1) * LANES].astype(F32)
        if norm is not None:
            x = _rms(x) * (gq_ref[...] if norm == "q" else gk_ref[...])
        r = x * c + pltpu.roll(x, 96, 1) * sa + pltpu.roll(x, 32, 1) * sb
        if scale != 1.0:
            r = r * scale
        o_ref[:, s * LANES:(s + 1) * LANES] = r.astype(o_ref.dtype)


def _prep(p, tables, gq, gk, plan):
    b, s, n = p.shape
    n_out = len(plan) * LANES
    tab_spec = pl.BlockSpec((ROW_TILE, LANES), lambda bi, i: (i, 0))
    g_spec = pl.BlockSpec((1, LANES), lambda bi, i: (0, 0))
    return pl.pallas_call(
        functools.partial(_prep_kernel, plan=plan),
        grid=(b, s // ROW_TILE),
        in_specs=[pl.BlockSpec((None, ROW_TILE, n), lambda bi, i: (bi, i, 0)),
                  tab_spec, tab_spec, tab_spec, g_spec, g_spec],
        out_specs=pl.BlockSpec((None, ROW_TILE, n_out), lambda bi, i: (bi, i, 0)),
        out_shape=jax.ShapeDtypeStruct((b, s, n_out), BF16),
        compiler_params=_params(("parallel", "parallel"), 32),
        name="qk_prep",
    )(p, *tables, gq, gk)


def _lane_tiles(s):
    return [s[:, j * LANES:(j + 1) * LANES] for j in range(s.shape[1] // LANES)]


def _stack_heads(q_ref, groups):
    return jnp.concatenate([q_ref[:, g * LANES:(g + 1) * LANES] for g in range(groups)], axis=0)


def _attn_oneshot_kernel(sink_ref, q_ref, k_ref, v_ref, o_ref, *, groups, tq, band, use_sink):
    h = pl.program_id(1)
    qi = pl.program_id(2)
    rows = groups * tq
    ctx0 = SEQ if band else 0
    q = _stack_heads(q_ref, groups)
    tiles = _lane_tiles(_dot_nt(q, k_ref[ctx0:ctx0 + CTX_LEN, :]))
    n_ctx_tiles = len(tiles)
    if band:
        nband = tq + 2 * BLOCK
        start = jnp.clip(qi * tq - BLOCK, 0, SEQ - nband)
        start = pl.multiple_of(start, BLOCK)
        row = lax.broadcasted_iota(jnp.int32, (rows, nband), 0)
        q_pos = qi * tq + (row & (tq - 1))
        k_pos = start + lax.broadcasted_iota(jnp.int32, (rows, nband), 1)
        valid = jnp.abs(q_pos - k_pos) <= WINDOW
        tiles += _lane_tiles(jnp.where(valid, _dot_nt(q, k_ref[pl.ds(start, nband), :]), NEG_INF))
    m = jnp.max(functools.reduce(jnp.maximum, tiles), axis=-1, keepdims=True)
    if use_sink:
        row1 = lax.broadcasted_iota(jnp.int32, (rows, 1), 0)
        sink = jnp.zeros((rows, 1), F32)
        for g in range(groups):
            sink = jnp.where(row1 >= g * tq, sink_ref[h * groups + g] * LOG2E, sink)
        m = jnp.maximum(m, sink)
    m_b = jnp.broadcast_to(m, (rows, LANES))
    ps = [jnp.exp2(t - m_b) for t in tiles]
    l = jnp.sum(functools.reduce(jnp.add, ps), axis=-1, keepdims=True)
    if use_sink:
        l = l + jnp.exp2(sink - m)
    p_c = jnp.concatenate([t.astype(BF16) for t in ps[:n_ctx_tiles]], axis=1)
    o = _dot(p_c, v_ref[ctx0:ctx0 + CTX_LEN, :])
    if band:
        p_b = jnp.concatenate([t.astype(BF16) for t in ps[n_ctx_tiles:]], axis=1)
        o = o + _dot(p_b, v_ref[pl.ds(start, nband), :])
    o = o / l
    for g in range(groups):
        o_ref[:, g * LANES:(g + 1) * LANES] = o[g * tq:(g + 1) * tq, :].astype(o_ref.dtype)


def _attn_oneshot(sink, qk, v, *, q_col, k_col, v_col, groups, kv_heads, q_row0, n_q, tq, band, use_sink):
    b = qk.shape[0]
    k_rows = S_ROWS if band else CTX_LEN
    k_row0 = 0 if band else SEQ // CTX_LEN
    gw = groups * LANES
    return pl.pallas_call(
        functools.partial(_attn_oneshot_kernel, groups=groups, tq=tq, band=band, use_sink=use_sink),
        grid=(b, kv_heads, n_q),
        in_specs=[
            pl.BlockSpec(memory_space=pltpu.SMEM),
            pl.BlockSpec((None, tq, gw), lambda bi, h, i: (bi, i + q_row0, q_col // groups + h)),
            pl.BlockSpec((None, k_rows, LANES), lambda bi, h, i: (bi, k_row0, k_col + h)),
            pl.BlockSpec((None, k_rows, LANES), lambda bi, h, i: (bi, k_row0, v_col + h)),
        ],
        out_specs=pl.BlockSpec((None, tq, gw), lambda bi, h, i: (bi, i, h)),
        out_shape=jax.ShapeDtypeStruct((b, n_q * tq, kv_heads * gw), BF16),
        compiler_params=_params(("parallel", "parallel", "parallel"), 32),
        name="attn_oneshot",
    )(sink, qk, qk, v)


def _attn_flash_kernel(lam_ref, subg_ref, q_ref, k_ref, v_ref, o_ref, q_sc, m_sc, l_sc, acc_sc,
                       *, streams, shared_k, tq, tk, diff_scale):
    dv = acc_sc.shape[1]
    for g in range(streams):
        q_sc[g * tq:(g + 1) * tq, :] = q_ref[:, g * LANES:(g + 1) * LANES]
    m_sc[...] = jnp.full(m_sc.shape, NEG_INF, F32)
    l_sc[...] = jnp.zeros(l_sc.shape, F32)
    acc_sc[...] = jnp.zeros(acc_sc.shape, F32)

    def chunk(off, size):
        kc = k_ref[pl.ds(off, size), :]
        if shared_k:
            s = _dot_nt(q_sc[...], kc)
        else:
            s = jnp.concatenate([_dot_nt(q_sc[g * tq:(g + 1) * tq, :], kc[:, g * LANES:(g + 1) * LANES])
                                 for g in range(streams)], axis=0)
        tiles = _lane_tiles(s)
        m_prev = m_sc[...]
        m_new = jnp.maximum(m_prev, jnp.max(functools.reduce(jnp.maximum, tiles), axis=-1, keepdims=True))
        alpha = jnp.exp2(m_prev - m_new)
        ps = [jnp.exp2(t - m_new) for t in tiles]
        m_sc[...] = m_new
        l_sc[...] = alpha * l_sc[...] + functools.reduce(jnp.add, ps)
        p = jnp.concatenate([t.astype(BF16) for t in ps], axis=1)
        pv = _dot(p, v_ref[pl.ds(off, size), :])
        acc_sc[...] = jnp.concatenate([alpha] * (dv // LANES), axis=1) * acc_sc[...] + pv

    def body(c, carry):
        chunk(pl.multiple_of(c * tk, LANES), tk)
        return carry

    if S_ROWS % tk == 0:
        lax.fori_loop(0, S_ROWS // tk, body, 0)
    else:
        chunk(SEQ, CTX_LEN)
        lax.fori_loop(0, SEQ // tk, body, 0)
    o = acc_sc[...] / jnp.sum(l_sc[...], axis=-1, keepdims=True)
    if diff_scale is None:
        for g in range(streams):
            o_ref[:, g * LANES:(g + 1) * LANES] = o[g * tq:(g + 1) * tq, :].astype(o_ref.dtype)
    else:
        lambda_init, out_scale = diff_scale
        lp = lam_ref[...]
        lam = (jnp.exp(jnp.sum(lp[0:1, :] * lp[1:2, :], axis=-1, keepdims=True))
               - jnp.exp(jnp.sum(lp[2:3, :] * lp[3:4, :], axis=-1, keepdims=True)) + lambda_init)
        o = o[0:tq, :] - lam * o[tq:2 * tq, :]
        o_ref[...] = (_rms(o) * subg_ref[...] * out_scale).astype(o_ref.dtype)


def _attn_flash(lam, subg, qk, v, *, q_col, k_col, v_col, streams, shared_k, dv, heads, tq, tk, diff_scale):
    b = qk.shape[0]
    qw = streams * LANES
    kw = LANES if shared_k else streams * LANES
    n_q = SEQ // tq
    ow = dv if diff_scale is not None else qw
    rows = streams * tq
    return pl.pallas_call(
        functools.partial(_attn_flash_kernel, streams=streams, shared_k=shared_k, tq=tq, tk=tk,
                          diff_scale=diff_scale),
        grid=(b, heads, n_q),
        in_specs=[
            pl.BlockSpec((4, LANES), lambda bi, h, i: (0, 0)),
            pl.BlockSpec((1, dv), lambda bi, h, i: (0, 0)),
            pl.BlockSpec((None, tq, qw), lambda bi, h, i: (bi, i, q_col + h)),
            pl.BlockSpec((None, S_ROWS, kw), lambda bi, h, i: (bi, 0, k_col + h)),
            pl.BlockSpec((None, S_ROWS, dv), lambda bi, h, i: (bi, 0, v_col + h)),
        ],
        out_specs=pl.BlockSpec((None, tq, ow), lambda bi, h, i: (bi, i, h)),
        out_shape=jax.ShapeDtypeStruct((b, SEQ, heads * ow), BF16),
        scratch_shapes=[pltpu.VMEM((rows, LANES), BF16), pltpu.VMEM((rows, LANES), F32),
                        pltpu.VMEM((rows, LANES), F32), pltpu.VMEM((rows, dv), F32)],
        compiler_params=_params(("parallel", "parallel", "parallel"), 48),
        name="attn_flash",
    )(lam, subg, qk, qk, v)


def _outproj_kernel(*refs, split):
    if split:
        ya_ref, yb_ref, yac_ref, ybc_ref, xl_ref, xc_ref = refs[:6]
        lat = pl.program_id(1) < _LAT_TILES
        y_in = jnp.concatenate([jnp.where(lat, ya_ref[...], yac_ref[...]),
                                jnp.where(lat, yb_ref[...], ybc_ref[...])], axis=1)
        x_in = jnp.where(lat, xl_ref[...], xc_ref[...])
        refs = refs[6:]
    else:
        y_in, x_in = refs[0][...], refs[1][...]
        refs = refs[2:]
    w_ref, mod_ref, g_ref, wr_ref, xo_ref, hf_ref, aff_ref = refs
    m = mod_ref[...]
    g = g_ref[...]
    y = _dot(y_in, w_ref[...])
    x1 = x_in + m[2:3, :] * (_rms(y) * g[1:2, :])
    xo_ref[...] = x1
    hf = _rms(x1) * g[2:3, :] * (1.0 + m[4:5, :]) + m[3:4, :]
    hf_ref[...] = hf
    logits = _dot(hf.astype(BF16), wr_ref[...]).T[0:N_EXPERTS, :]
    e = jnp.exp(logits - jnp.max(logits, axis=0, keepdims=True))
    aff_ref[...] = e / jnp.sum(e, axis=0, keepdims=True)


def _outproj(ys, xs, w_bf, mod_tab, g4, wr_bf, *, split):
    b, d = xs[0].shape[0], xs[0].shape[2]
    sy = S_ROWS if split else ys[0].shape[1]
    seg = _seg if split else (lambda i: 1)
    row = lambda bi, i: (bi, i, 0)
    if split:
        half = pl.BlockSpec((None, ROW_TILE, d // 2), _lat_row)
        half_c = pl.BlockSpec((None, ROW_TILE, d // 2), _ctx_row)
        data_specs = [half, half, half_c, half_c,
                      pl.BlockSpec((None, ROW_TILE, d), _lat_row), pl.BlockSpec((None, ROW_TILE, d), _ctx_row)]
    else:
        data_specs = [pl.BlockSpec((None, ROW_TILE, d), row), pl.BlockSpec((None, ROW_TILE, d), row)]
    out_f32 = jax.ShapeDtypeStruct((b, sy, d), F32)
    return pl.pallas_call(
        functools.partial(_outproj_kernel, split=split),
        grid=(b, sy // ROW_TILE),
        in_specs=data_specs + [
            pl.BlockSpec((d, d), lambda bi, i: (0, 0)),
            pl.BlockSpec((None, None, 6, d), lambda bi, i: (bi, seg(i), 0, 0)),
            pl.BlockSpec((4, d), lambda bi, i: (0, 0)),
            pl.BlockSpec((d, LANES), lambda bi, i: (0, 0)),
        ],
        out_specs=[pl.BlockSpec((None, ROW_TILE, d), row), pl.BlockSpec((None, ROW_TILE, d), row),
                   pl.BlockSpec((None, N_EXPERTS, ROW_TILE), lambda bi, i: (bi, 0, i))],
        out_shape=[out_f32, out_f32, jax.ShapeDtypeStruct((b, N_EXPERTS, sy), F32)],
        compiler_params=_params(("parallel", "parallel"), 48),
        name="out_proj",
    )(*ys, *xs, w_bf, mod_tab, g4, wr_bf)


def _row_copy(h_hbm, stage_sc, sem, src_row, dst_row, n_rows):
    return pltpu.make_async_copy(h_hbm.at[pl.ds(src_row, n_rows), :], stage_sc.at[pl.ds(dst_row, n_rows), :], sem)


def _moe_kernel(idx_ref, h_hbm, wg_ref, wu_ref, wd_ref, gate_ref, o_ref, stage_sc, x_sc, hmid_sc, sem, *, n_up, n_down):
    e = pl.program_id(0)
    s = pl.program_id(1)
    last = (e == pl.num_programs(0) - 1) & (s == n_up + n_down - 1)
    m = x_sc.shape[0]
    rows_per_step = m // n_up

    def issue_one(expert, row):
        _row_copy(h_hbm, stage_sc, sem, idx_ref[expert, row], row, 1).start()

    @pl.when(s == 0)
    def _():
        @pl.when(e == 0)
        def _():
            def body(i, carry):
                issue_one(0, i)
                return carry

            lax.fori_loop(0, m, body, 0, unroll=8)

        _row_copy(h_hbm, stage_sc, sem, 0, 0, m).wait()
        x_sc[...] = stage_sc[...].astype(BF16)

    @pl.when(s < n_up)
    def _():
        for i in range(rows_per_step):
            issue_one(e + 1, s * rows_per_step + i)
        x = x_sc[...]
        a = _dot(x, wg_ref[...].astype(BF16))
        u = _dot(x, wu_ref[...].astype(BF16))
        hmid_sc[s] = (a * jax.nn.sigmoid(a) * u).astype(BF16)

    @pl.when(s >= n_up)
    def _():
        hmid = jnp.concatenate([hmid_sc[f] for f in range(n_up)], axis=1)
        y = _dot(hmid, wd_ref[...].astype(BF16))
        o_ref[...] = (y * gate_ref[...]).astype(o_ref.dtype)

    @pl.when(last)
    def _():
        _row_copy(h_hbm, stage_sc, sem, 0, 0, m).wait()


def _moe(idx, h, w_g, w_u, w_d, layer, gates):
    e, m = idx.shape
    d = w_g.shape[2]
    f_dim = w_g.shape[-1]
    t_up, t_down = 256, 1024
    n_up, n_down = f_dim // t_up, d // t_down
    idx = jnp.concatenate([idx, jnp.zeros((1, m), idx.dtype)], axis=0)
    up_tile = lambda ei, s, idx_ref: (layer, ei, 0, jnp.minimum(s, n_up - 1))
    down_tile = lambda s: jnp.maximum(s - n_up, 0)
    grid_spec = pltpu.PrefetchScalarGridSpec(
        num_scalar_prefetch=1,
        grid=(e, n_up + n_down),
        in_specs=[
            pl.BlockSpec(memory_space=pl.ANY),
            pl.BlockSpec((None, None, d, t_up), up_tile),
            pl.BlockSpec((None, None, d, t_up), up_tile),
            pl.BlockSpec((None, None, f_dim, t_down), lambda ei, s, idx_ref: (layer, ei, 0, down_tile(s))),
            pl.BlockSpec((None, m, 1), lambda ei, s, idx_ref: (ei, 0, 0)),
        ],
        out_specs=pl.BlockSpec((None, m, t_down), lambda ei, s, idx_ref: (ei, 0, down_tile(s))),
        scratch_shapes=[pltpu.VMEM((m, d), F32), pltpu.VMEM((m, d), BF16), pltpu.VMEM((n_up, m, t_up), BF16),
                        pltpu.SemaphoreType.DMA(())],
    )
    return pl.pallas_call(
        functools.partial(_moe_kernel, n_up=n_up, n_down=n_down),
        grid_spec=grid_spec,
        out_shape=jax.ShapeDtypeStruct((e, m, d), BF16),
        compiler_params=_params(("arbitrary", "arbitrary"), 58),
        name="moe_experts",
    )(idx, h, w_g, w_u, w_d, gates)


_BF16_ROWS = 16
_WIN = LANES + _BF16_ROWS
_WIN_HEAD = 3 * _BF16_ROWS


def _combine_kernel(starts_ref, tab_ref, y_hbm, x_ref, mod_ref, g_ref, *rest, cap_t, n_latent, has_next):
    if has_next:
        modn_ref, gn_ref, xo_ref, h_ref, buf, sem = rest
    else:
        xo_ref, buf, sem = rest
    b = pl.program_id(0)
    j = pl.program_id(1)
    nb = pl.num_programs(0)
    n_tiles = pl.num_programs(1)
    t = b * n_tiles + j
    cur = lax.rem(t, 2)

    def window_start(bb, jj, e):
        s = starts_ref[bb * N_EXPERTS + e, jj]
        return jnp.minimum(s // _BF16_ROWS * _BF16_ROWS, cap_t - _WIN)

    def needs_tail(bb, jj, e):
        return starts_ref[bb * N_EXPERTS + e, jj + 1] > window_start(bb, jj, e) + _WIN_HEAD

    def head_copy(row, e, slot, n_rows):
        return pltpu.make_async_copy(y_hbm.at[pl.ds(row, n_rows), :], buf.at[slot, pl.ds(e * _WIN, n_rows), :],
                                     sem.at[0, slot])

    def tail_copy(row, e, slot):
        return pltpu.make_async_copy(y_hbm.at[pl.ds(row + _WIN_HEAD, _WIN - _WIN_HEAD), :],
                                     buf.at[slot, pl.ds(e * _WIN + _WIN_HEAD, _WIN - _WIN_HEAD), :], sem.at[1, slot])

    def window_row(bb, jj, e):
        return pl.multiple_of((e * nb + bb) * cap_t + window_start(bb, jj, e), _BF16_ROWS)

    def fetch(bb, jj, slot):
        for e in range(N_EXPERTS):
            row = window_row(bb, jj, e)
            head_copy(row, e, slot, _WIN_HEAD).start()

            @pl.when(needs_tail(bb, jj, e))
            def _():
                tail_copy(row, e, slot).start()

    @pl.when(t == 0)
    def _():
        buf[...] = jnp.zeros(buf.shape, buf.dtype)
        fetch(0, 0, 0)

    @pl.when(t + 1 < nb * n_tiles)
    def _():
        t1 = t + 1
        fetch(t1 // n_tiles, lax.rem(t1, n_tiles), 1 - cur)

    for e in range(N_EXPERTS):
        head_copy(0, e, cur, _WIN_HEAD).wait()

        @pl.when(needs_tail(b, j, e))
        def _():
            tail_copy(window_row(b, j, e), e, cur).wait()

    tab = tab_ref[...]
    lane_f = lax.broadcasted_iota(jnp.int32, (LANES, LANES), 1).astype(F32)
    col_of = []
    for e in range(N_EXPERTS):
        slot_e = tab[:, e:e + 1]
        rel = slot_e - window_start(b, j, e).astype(F32) + float(e * _WIN)
        col_of.append(jnp.broadcast_to(jnp.where(slot_e >= 0.0, rel, -1.0), (LANES, LANES)))
    tiles = []
    for i in range(N_EXPERTS * _WIN // LANES):
        cols = lane_f + float(i * LANES)
        e_lo, e_hi = i * LANES // _WIN, (i * LANES + LANES - 1) // _WIN
        hit = col_of[e_lo] == cols
        if e_hi != e_lo:
            hit = hit | (col_of[e_hi] == cols)
        tiles.append(jnp.where(hit, 1.0, 0.0).astype(BF16))
    moe = _dot(jnp.concatenate(tiles, axis=1), buf[cur])

    is_lat = (j * LANES + lax.broadcasted_iota(jnp.int32, (LANES, 1), 0)) < n_latent

    def mod(ref, kk):
        return jnp.where(is_lat, ref[1, kk:kk + 1, :], ref[0, kk:kk + 1, :])

    x2 = x_ref[...] + mod(mod_ref, 5) * (_rms(moe) * g_ref[3:4, :])
    xo_ref[...] = x2
    if has_next:
        h_ref[...] = (_rms(x2) * gn_ref[0:1, :] * (1.0 + mod(modn_ref, 1)) + mod(modn_ref, 0)).astype(h_ref.dtype)


def _combine(starts, tab, y, x, mod_tab, g4, mod_next, g4_next, *, n_latent):
    b, sx, d = x.shape
    cap_t = y.shape[0] // (b * N_EXPERTS)
    has_next = mod_next is not None
    row = pl.BlockSpec((None, LANES, d), lambda bi, i, s: (bi, i, 0))
    mod_spec = pl.BlockSpec((None, 2, 6, d), lambda bi, i, s: (bi, 0, 0, 0))
    g_spec = pl.BlockSpec((4, d), lambda bi, i, s: (0, 0))
    in_specs = [pl.BlockSpec((None, LANES, LANES), lambda bi, i, s: (bi, i, 0)),
                pl.BlockSpec(memory_space=pl.ANY), row, mod_spec, g_spec]
    ins = [tab, y, x, mod_tab, g4]
    out_specs, out_shape = [row], [jax.ShapeDtypeStruct((b, sx, d), F32)]
    if has_next:
        in_specs += [mod_spec, g_spec]
        ins += [mod_next, g4_next]
        out_specs.append(row)
        out_shape.append(jax.ShapeDtypeStruct((b, sx, d), BF16))
    grid_spec = pltpu.PrefetchScalarGridSpec(
        num_scalar_prefetch=1, grid=(b, sx // LANES), in_specs=in_specs, out_specs=out_specs,
        scratch_shapes=[pltpu.VMEM((2, N_EXPERTS * _WIN, d), BF16), pltpu.SemaphoreType.DMA((2, 2))])
    return pl.pallas_call(
        functools.partial(_combine_kernel, cap_t=cap_t, n_latent=n_latent, has_next=has_next),
        grid_spec=grid_spec, out_shape=out_shape,
        compiler_params=_params(("arbitrary", "arbitrary"), 48), name="moe_combine",
    )(starts, *ins)


def _prefix_count(mask_tiles, tri):
    off = jnp.zeros((N_EXPERTS, 1), F32)
    out, before = [], []
    for mt in mask_tiles:
        before.append(off)
        w = _dot(jnp.where(mt, 1.0, 0.0).astype(BF16), tri) + off
        out.append(w)
        off = w[:, LANES - 1:LANES]
    return out, before


_MIN_EXP = -160.0
_EXP_STEPS = 8
_VALUE_STEPS = 40


def _route_kernel(aff_ref, tab_ref, starts_ref, slot_ref, atile_ref, *, segments):
    r_i = lax.broadcasted_iota(jnp.int32, (LANES, LANES), 0)
    c_i = lax.broadcasted_iota(jnp.int32, (LANES, LANES), 1)
    tri = jnp.where(r_i <= c_i, 1.0, 0.0).astype(BF16)
    shape = (N_EXPERTS, LANES)
    lane_e = lax.broadcasted_iota(jnp.int32, shape, 1)
    starts = jnp.zeros(shape, F32)
    base = 0
    tile0 = 0
    for t0, n, cap in segments:
        nt = n // LANES
        a_t = [aff_ref[:, t0 + j * LANES:t0 + (j + 1) * LANES] for j in range(nt)]

        def count_ge(v, a_t=a_t):
            v_b = jnp.broadcast_to(v, shape)
            cnt = functools.reduce(jnp.add, [jnp.where(t >= v_b, 1.0, 0.0) for t in a_t])
            return jnp.sum(cnt, axis=1, keepdims=True)

        e_lo = jnp.full((N_EXPERTS, 1), _MIN_EXP, F32)
        e_hi = jnp.full((N_EXPERTS, 1), 1.0, F32)
        lo = jnp.zeros((N_EXPERTS, 1), F32)
        hi = jnp.full((N_EXPERTS, 1), 2.0, F32)
        for _ in range(_EXP_STEPS):
            e_mid = jnp.floor((e_lo + e_hi) * 0.5)
            v = jnp.exp2(e_mid)
            ok = count_ge(v) >= cap
            e_lo, lo = jnp.where(ok, e_mid, e_lo), jnp.where(ok, v, lo)
            e_hi, hi = jnp.where(ok, e_hi, e_mid), jnp.where(ok, hi, v)
        for _ in range(_VALUE_STEPS):
            v = (lo + hi) * 0.5
            ok = count_ge(v) >= cap
            lo, hi = jnp.where(ok, v, lo), jnp.where(ok, hi, v)
        lo_b = jnp.broadcast_to(lo, shape)
        hi_b = jnp.broadcast_to(hi, shape)
        gt = [t >= hi_b for t in a_t]
        eq = [(t >= lo_b) & (t < hi_b) for t in a_t]
        need = jnp.broadcast_to(cap - count_ge(hi), shape)
        eq_rank, _ = _prefix_count(eq, tri)
        sel = [g | (q & (r <= need)) for g, q, r in zip(gt, eq, eq_rank)]
        pos, before = _prefix_count(sel, tri)
        for j in range(nt):
            starts = jnp.where(lane_e == tile0 + j, before[j] + float(base), starts)
            slot = jnp.where(sel[j], pos[j] + (base - 1.0), -1.0)
            slot_ref[tile0 + j] = slot
            atile_ref[tile0 + j] = a_t[j]
            padded =jnp.concatenate([slot, jnp.full((LANES - N_EXPERTS, LANES), -1.0, F32)], axis=0)
            tab_ref[t0 + j * LANES:t0 + (j + 1) * LANES, :] = padded.T
        base += cap
        tile0 += nt
    starts_ref[...] = jnp.where(lane_e == tile0, float(base), starts)


def _compact_kernel(starts_ref, slot_ref, aff_ref, idx_ref, gate_ref, acc_i, acc_g, *, segments, rows_per_sample):
    bi = pl.program_id(0)
    row0 = (bi * rows_per_sample).astype(F32)
    r_i = lax.broadcasted_iota(jnp.int32, (LANES, LANES), 0)
    lane1 = lax.broadcasted_iota(jnp.int32, (1, LANES), 1)
    base = 0
    tile0 = 0
    for t0, n, cap in segments:
        nt = n // LANES

        def expert_body(e, carry, t0=t0, nt=nt, cap=cap, base=base, tile0=tile0):
            srow = bi * N_EXPERTS + e
            n_chunks = -(-cap // LANES)

            def scan(j, bounds):
                first = starts_ref[srow, tile0 + j]
                nxt = jnp.where(j + 1 < nt, starts_ref[srow, tile0 + jnp.minimum(j + 1, nt - 1)], base + cap)
                out = []
                for c in range(n_chunks):
                    lo = base + c * LANES
                    out.append(bounds[2 * c] + (nxt <= lo).astype(jnp.int32))
                    out.append(bounds[2 * c + 1] + (first < lo + LANES).astype(jnp.int32))
                return tuple(out)

            bounds = lax.fori_loop(0, nt, scan, (jnp.int32(0),) * (2 * n_chunks), unroll=4)
            for c in range(n_chunks):
                lo = base + c * LANES
                j_lo, j_hi = bounds[2 * c], bounds[2 * c + 1]
                slot_id = (r_i + lo).astype(F32)
                acc_i[...] = jnp.zeros((LANES, LANES), F32)
                acc_g[...] = jnp.zeros((LANES, LANES), F32)

                def tile_body(j, carry2):
                    pe = jnp.broadcast_to(slot_ref[tile0 + j, pl.ds(e, 1), :], (LANES, LANES))
                    ae = jnp.broadcast_to(aff_ref[tile0 + j, pl.ds(e, 1), :], (LANES, LANES))
                    tok = jnp.broadcast_to((lane1 + (t0 + j * LANES)).astype(F32) + row0, (LANES, LANES))
                    hit = pe == slot_id
                    acc_i[...] += jnp.where(hit, tok, 0.0)
                    acc_g[...] += jnp.where(hit, ae, 0.0)
                    return carry2

                lax.fori_loop(j_lo, j_hi, tile_body, 0)
                w = min(LANES, cap - c * LANES)
                col_i = jnp.broadcast_to(jnp.sum(acc_i[...], axis=1, keepdims=True), (LANES, LANES))
                col_g = jnp.broadcast_to(jnp.sum(acc_g[...], axis=1, keepdims=True), (LANES, LANES))
                idx_ref[e, :, lo:lo + w] = col_i.T[0:1, 0:w].astype(jnp.int32)
                gate_ref[e, :, lo:lo + w] = col_g.T[0:1, 0:w]
            return carry

        lax.fori_loop(0, N_EXPERTS, expert_body, 0)
        base += cap
        tile0 += nt


def _route(aff_t, segments, rows_per_sample):
    b, e, sy = aff_t.shape
    k = sum(cap for _, _, cap in segments)
    n_tiles = sum(n // LANES for _, n, _ in segments)
    tiles_spec = pl.BlockSpec((None, n_tiles, e, LANES), lambda bi: (bi, 0, 0, 0))
    tiles_shape = jax.ShapeDtypeStruct((b, n_tiles, e, LANES), F32)
    tab, starts, slots, a_tiles = pl.pallas_call(
        functools.partial(_route_kernel, segments=segments),
        grid=(b,),
        in_specs=[pl.BlockSpec((None, e, sy), lambda bi: (bi, 0, 0))],
        out_specs=[pl.BlockSpec((None, sy, LANES), lambda bi: (bi, 0, 0)),
                   pl.BlockSpec((None, e, LANES), lambda bi: (bi, 0, 0)), tiles_spec, tiles_spec],
        out_shape=[jax.ShapeDtypeStruct((b, sy, LANES), F32), jax.ShapeDtypeStruct((b, e, LANES), F32),
                   tiles_shape, tiles_shape],
        compiler_params=_params(("parallel",), 32),
        name="route_topk",
    )(aff_t)
    starts = starts.astype(jnp.int32).reshape(b * e, LANES)
    grid_spec = pltpu.PrefetchScalarGridSpec(
        num_scalar_prefetch=1, grid=(b,),
        in_specs=[pl.BlockSpec((None, n_tiles, e, LANES), lambda bi, s: (bi, 0, 0, 0)),
                  pl.BlockSpec((None, n_tiles, e, LANES), lambda bi, s: (bi, 0, 0, 0))],
        out_specs=[pl.BlockSpec((None, e, 1, k), lambda bi, s: (bi, 0, 0, 0)),
                   pl.BlockSpec((None, e, 1, k), lambda bi, s: (bi, 0, 0, 0))],
        scratch_shapes=[pltpu.VMEM((LANES, LANES), F32), pltpu.VMEM((LANES, LANES), F32)])
    idx, gates = pl.pallas_call(
        functools.partial(_compact_kernel, segments=segments, rows_per_sample=rows_per_sample),
        grid_spec=grid_spec,
        out_shape=[jax.ShapeDtypeStruct((b, e, 1, k), jnp.int32), jax.ShapeDtypeStruct((b, e, 1, k), F32)],
        compiler_params=_params(("parallel",), 32),
        name="route_compact",
    )(starts, slots, a_tiles)
    return tab, idx, gates, starts


def _moe_block(x, hf, aff_t, segments, w_g, w_u, w_d, layer, mod_tab, g4, mod_next, g4_next, *, n_latent):
    b, sy = aff_t.shape[0], aff_t.shape[2]
    tab, idx, gates, starts = _route(aff_t, segments, sy)
    k = idx.shape[-1]
    idx_em = jnp.swapaxes(idx[:, :, 0, :], 0, 1).reshape(N_EXPERTS, b * k)
    gates_em = jnp.swapaxes(gates[:, :, 0, :], 0, 1).reshape(N_EXPERTS, b * k, 1)
    y = _moe(idx_em, hf.reshape(b * sy, hf.shape[-1]), w_g, w_u, w_d, layer, gates_em)
    y2d = y.reshape(N_EXPERTS * b * k, y.shape[-1])
    return _combine(starts, tab, y2d, x, mod_tab, g4, mod_next, g4_next, n_latent=n_latent)


_SCALE = HEAD_DIM ** -0.5 * LOG2E
_PLAN_AB = tuple([(s, None, _SCALE) for s in range(0, 8)] + [(s, "q", _SCALE) for s in range(12, 20)]
                 + [(s, None, 1.0) for s in range(8, 10)] + [(s, "k", 1.0) for s in range(20, 22)])
_PLAN_C = tuple([(s, None, _SCALE) for s in range(0, 16)] + [(s, None, 1.0) for s in range(16, 32)])


def kernel(x, c, ctx, c_ctx, mod_w, mod_b, norm_g, ab_w_in, ab_w_out, ab_sink, ab_q_norm, ab_k_norm,
           dif_w_in, dif_w_out, dif_lambda, dif_subln, router_w, exp_w_gate, exp_w_up, exp_w_down):
    b = x.shape[0]
    d = D_MODEL
    cv = jnp.concatenate([c, c_ctx[None, :], jnp.zeros((8 - b - 1, d), F32)], axis=0)
    mods = _mods(cv, mod_w, mod_b).reshape(DEPTH, 8, 6, d)
    mod_tab = jnp.stack([jnp.broadcast_to(mods[:, b][:, None], (DEPTH, b, 6, d)), mods[:, :b]], axis=2)
    tables = _rope_tables()
    ones = jnp.ones((1, LANES), F32)
    wr = jnp.pad(router_w, ((0, 0), (0, 0), (0, LANES - N_EXPERTS))).astype(BF16)

    h0 = _norm_mod(x, ctx, norm_g[0], mod_tab[0])
    p0 = _matmul(h0.reshape(b * S_ROWS, d), ab_w_in[0], BF16).reshape(b, S_ROWS, AB_IN)
    qk0 = _prep(p0, tables, ab_q_norm[0][None, :], ab_k_norm[0][None, :], _PLAN_AB)
    sink = ab_sink[0]
    common_a = dict(q_col=0, k_col=16, v_col=10, groups=4, kv_heads=A_KV_HEADS)
    common_b = dict(q_col=8, k_col=18, v_col=22, groups=4, kv_heads=B_KV_HEADS)
    ctx_q = dict(q_row0=SEQ // CTX_LEN, n_q=1, tq=CTX_LEN, band=False)
    ya_l = _attn_oneshot(sink, qk0, p0, **common_a, q_row0=0, n_q=SEQ // 256, tq=256, band=True, use_sink=True)
    ya_c = _attn_oneshot(sink, qk0, p0, **common_a, **ctx_q, use_sink=True)
    yb_c = _attn_oneshot(sink, qk0, p0, **common_b, **ctx_q, use_sink=False)
    yb_l = _attn_flash(jnp.zeros((4, LANES), F32), ones, qk0, p0, q_col=2, k_col=18, v_col=22, streams=4,
                       shared_k=True, dv=HEAD_DIM, heads=B_KV_HEADS, tq=256, tk=2048, diff_scale=None)
    x1, hf0, aff0 = _outproj((ya_l, yb_l, ya_c, yb_c), (x, ctx), _cast_bf16(ab_w_out[0]), mod_tab[0], norm_g[0],
                             wr[0], split=True)
    cap_l = EC_FACTOR * SEQ // N_EXPERTS
    cap_c = EC_FACTOR * CTX_LEN // N_EXPERTS
    x2, h1 = _moe_block(x1, hf0, aff0, ((0, SEQ, cap_l), (SEQ, CTX_LEN, cap_c)),
                        exp_w_gate, exp_w_up, exp_w_down, 0, mod_tab[0], norm_g[0], mod_tab[1], norm_g[1],
                        n_latent=SEQ)

    p1 = _matmul(h1.reshape(b * S_ROWS, d), dif_w_in[0], BF16).reshape(b, S_ROWS, C_IN)
    qk1 = _prep(p1, tables, ones, ones, _PLAN_C)
    lambda_init = 0.8 - 0.6 * math.exp(-0.3 * 1)
    y1 = _attn_flash(dif_lambda[0], dif_subln[0][None, :], qk1, p1, q_col=0, k_col=8, v_col=16, streams=2,
                     shared_k=False, dv=C_V_DIM, heads=C_HEADS, tq=512, tk=2048,
                     diff_scale=(lambda_init, 1.0 - lambda_init))
    x3, hf1, aff1 = _outproj((y1,), (x2,), _cast_bf16(dif_w_out[0]), mod_tab[1], norm_g[1], wr[1], split=False)
    (out,) = _moe_block(x3, hf1, aff1, ((0, SEQ, cap_l),), exp_w_gate, exp_w_up, exp_w_down, 1,
                        mod_tab[1], norm_g[1], None, None, n_latent=SEQ)
    return out
```

```python
import functools
import math

import jax
import jax.numpy as jnp
from jax import lax
from jax.experimental import pallas as pl
from jax.experimental.pallas import tpu as pltpu

D_MODEL = 2048
SEQ = 4096
DEPTH = 2
GRID_W = 64
CTX_LEN = 256
HEAD_DIM = 128
ROPE_BASE = 10000.0
BLOCK = 128
WINDOW = 128
A_Q_HEADS = 8
A_KV_HEADS = 2
B_Q_HEADS = 8
B_KV_HEADS = 2
AB_IN = 3072
C_HEADS = 8
C_V_DIM = 256
C_IN = 6144
N_EXPERTS = 16
EC_FACTOR = 2
EPS = 1e-6
NEG_INF = -1e30

S_ROWS = SEQ + CTX_LEN
ROW_TILE = 256
LANES = 128
MIB = 1024 * 1024
LOG2E = math.log2(math.e)

F32 = jnp.float32
BF16 = jnp.bfloat16


def _params(sem, vmem_mib):
    return pltpu.CompilerParams(dimension_semantics=sem, vmem_limit_bytes=vmem_mib * MIB)


def _rms(x):
    return x * lax.rsqrt(jnp.mean(x * x, axis=-1, keepdims=True) + EPS)


def _dot(a, b):
    return jnp.dot(a, b, preferred_element_type=F32)


def _dot_nt(a, b):
    return lax.dot_general(a, b, (((1,), (1,)), ((), ())), preferred_element_type=F32)


def _mods_kernel(cv_ref, w_ref, b_ref, o_ref):
    cv = cv_ref[...]
    s = cv * jax.nn.sigmoid(cv)
    o_ref[...] = _dot(s.astype(BF16), w_ref[...].astype(BF16)) + b_ref[...]


def _mods(cv, mod_w, mod_b):
    tn = 1024
    n = mod_w.shape[-1]
    return pl.pallas_call(
        _mods_kernel,
        grid=(DEPTH, n // tn),
        in_specs=[
            pl.BlockSpec((8, D_MODEL), lambda l, j: (0, 0)),
            pl.BlockSpec((None, D_MODEL, tn), lambda l, j: (l, 0, j)),
            pl.BlockSpec((None, 1, tn), lambda l, j: (l, 0, j)),
        ],
        out_specs=pl.BlockSpec((None, 8, tn), lambda l, j: (l, 0, j)),
        out_shape=jax.ShapeDtypeStruct((DEPTH, 8, n), F32),
        compiler_params=_params(("parallel", "parallel"), 40),
        name="mods",
    )(cv, mod_w, mod_b.reshape(DEPTH, 1, n))


def _cast_kernel(x_ref, o_ref):
    o_ref[...] = x_ref[...].astype(o_ref.dtype)


def _cast_bf16(w):
    k, n = w.shape
    tk = 512
    return pl.pallas_call(
        _cast_kernel,
        grid=(k // tk,),
        in_specs=[pl.BlockSpec((tk, n), lambda i: (i, 0))],
        out_specs=pl.BlockSpec((tk, n), lambda i: (i, 0)),
        out_shape=jax.ShapeDtypeStruct((k, n), BF16),
        compiler_params=_params(("parallel",), 48),
        name="cast_bf16",
    )(w)


def _seg(i):
    return jnp.where(i < SEQ // ROW_TILE, 1, 0)


_LAT_TILES = SEQ // ROW_TILE


def _lat_row(bi, i):
    return (bi, jnp.minimum(i, _LAT_TILES - 1), 0)


def _ctx_row(bi, i):
    return (bi, 0, 0)


def _norm_mod_kernel(x_ref, ctx_ref, g_ref, mod_ref, h_ref):
    m = mod_ref[...]
    xv = jnp.where(pl.program_id(1) < _LAT_TILES, x_ref[...], ctx_ref[...])
    h = _rms(xv) * g_ref[0:1, :] * (1.0 + m[1:2, :]) + m[0:1, :]
    h_ref[...] = h.astype(h_ref.dtype)


def _norm_mod(x, ctx, g4, mod_tab):
    b, _, d = x.shape
    return pl.pallas_call(
        _norm_mod_kernel,
        grid=(b, S_ROWS // ROW_TILE),
        in_specs=[
            pl.BlockSpec((None, ROW_TILE, d), _lat_row),
            pl.BlockSpec((None, ROW_TILE, d), _ctx_row),
            pl.BlockSpec((4, d), lambda bi, i: (0, 0)),
            pl.BlockSpec((None, None, 6, d), lambda bi, i: (bi, _seg(i), 0, 0)),
        ],
        out_specs=pl.BlockSpec((None, ROW_TILE, d), lambda bi, i: (bi, i, 0)),
        out_shape=jax.ShapeDtypeStruct((b, S_ROWS, d), BF16),
        compiler_params=_params(("parallel", "parallel"), 32),
        name="norm_mod",
    )(x, ctx, g4, mod_tab)


def _mm_kernel(a_ref, w_ref, o_ref, w_sc):
    @pl.when(pl.program_id(1) == 0)
    def _():
        w_sc[...] = w_ref[...].astype(BF16)

    o_ref[...] = _dot(a_ref[...], w_sc[...]).astype(o_ref.dtype)


def _matmul(a, w, out_dtype):
    m, k = a.shape
    n = w.shape[1]
    tm, tn = 1088, 1024
    return pl.pallas_call(
        _mm_kernel,
        grid=(n // tn, m // tm),
        in_specs=[
            pl.BlockSpec((tm, k), lambda j, i: (i, 0)),
            pl.BlockSpec((k, tn), lambda j, i: (0, j)),
        ],
        out_specs=pl.BlockSpec((tm, tn), lambda j, i: (i, j)),
        out_shape=jax.ShapeDtypeStruct((m, n), out_dtype),
        scratch_shapes=[pltpu.VMEM((k, tn), BF16)],
        compiler_params=_params(("parallel", "arbitrary"), 48),
        name="in_proj",
    )(a, w)


def _rope_tables(q_scale):
    n_rows = SEQ // GRID_W
    row = jnp.repeat(jnp.arange(n_rows), GRID_W).astype(F32)
    col = jnp.tile(jnp.arange(GRID_W), n_rows).astype(F32)
    n_freq = HEAD_DIM // 4
    inv = ROPE_BASE ** (-jnp.arange(n_freq, dtype=F32) / n_freq)
    ar = row[:, None] * inv
    ac = col[:, None] * inv
    cr, sr, cc, sc = jnp.cos(ar), jnp.sin(ar), jnp.cos(ac), jnp.sin(ac)
    c = jnp.concatenate([cr, cr, cc, cc], axis=-1)
    s = jnp.concatenate([-sr, sr, -sc, sc], axis=-1)
    pad = jnp.zeros((CTX_LEN, HEAD_DIM), F32)
    c = jnp.concatenate([c, pad + 1.0], axis=0)
    s = jnp.concatenate([s, pad], axis=0)
    return c, s, c * q_scale, s * q_scale


def _prep_kernel(p_ref, c_ref, s_ref, cq_ref, sq_ref, gq_ref, gk_ref, o_ref, *, plan):
    lane = lax.broadcasted_iota(jnp.int32, (1, LANES), 1)
    first = (lane & (HEAD_DIM // 4)) == 0
    for s, (src, norm, scaled) in enumerate(plan):
        x = p_ref[:, src * LANES:(src + 1) * LANES].astype(F32)
        if norm is not None:
            x = _rms(x) * (gq_ref[...] if norm == "q" else gk_ref[...])
        partner = jnp.where(first, pltpu.roll(x, 96, 1), pltpu.roll(x, 32, 1))
        c, sn = (cq_ref[...], sq_ref[...]) if scaled else (c_ref[...], s_ref[...])
        o_ref[:, s * LANES:(s + 1) * LANES] = (x * c + partner * sn).astype(o_ref.dtype)


def _prep(p, tables, gq, gk, plan):
    b, s, n = p.shape
    n_out = len(plan) * LANES
    tab_spec = pl.BlockSpec((ROW_TILE, LANES), lambda bi, i: (i, 0))
    g_spec = pl.BlockSpec((1, LANES), lambda bi, i: (0, 0))
    return pl.pallas_call(
        functools.partial(_prep_kernel, plan=plan),
        grid=(b, s // ROW_TILE),
        in_specs=[pl.BlockSpec((None, ROW_TILE, n), lambda bi, i: (bi, i, 0)),
                  tab_spec, tab_spec, tab_spec, tab_spec, g_spec, g_spec],
        out_specs=pl.BlockSpec((None, ROW_TILE, n_out), lambda bi, i: (bi, i, 0)),
        out_shape=jax.ShapeDtypeStruct((b, s, n_out), BF16),
        compiler_params=_params(("parallel", "parallel"), 32),
        name="qk_prep",
    )(p, *tables, gq, gk)


def _lane_tiles(s):
    return [s[:, j * LANES:(j + 1) * LANES] for j in range(s.shape[1] // LANES)]


def _stack_heads(q_ref, groups):
    return jnp.concatenate([q_ref[:, g * LANES:(g + 1) * LANES] for g in range(groups)], axis=0)


def _attn_oneshot_kernel(sink_ref, q_ref, k_ref, v_ref, o_ref, *, groups, tq, band, use_sink):
    h = pl.program_id(1)
    qi = pl.program_id(2)
    rows = groups * tq
    ctx0 = SEQ if band else 0
    q = _stack_heads(q_ref, groups)
    tiles = _lane_tiles(_dot_nt(q, k_ref[ctx0:ctx0 + CTX_LEN, :]))
    n_ctx_tiles = len(tiles)
    if band:
        nband = tq + 2 * BLOCK
        start = jnp.clip(qi * tq - BLOCK, 0, SEQ - nband)
        start = pl.multiple_of(start, BLOCK)
        row = lax.broadcasted_iota(jnp.int32, (rows, nband), 0)
        q_pos = qi * tq + (row & (tq - 1))
        k_pos = start + lax.broadcasted_iota(jnp.int32, (rows, nband), 1)
        valid = jnp.abs(q_pos - k_pos) <= WINDOW
        tiles += _lane_tiles(jnp.where(valid, _dot_nt(q, k_ref[pl.ds(start, nband), :]), NEG_INF))
    m = jnp.max(functools.reduce(jnp.maximum, tiles), axis=-1, keepdims=True)
    if use_sink:
        row1 = lax.broadcasted_iota(jnp.int32, (rows, 1), 0)
        sink = jnp.zeros((rows, 1), F32)
        for g in range(groups):
            sink = jnp.where(row1 >= g * tq, sink_ref[h * groups + g] * LOG2E, sink)
        m = jnp.maximum(m, sink)
    m_b = jnp.broadcast_to(m, (rows, LANES))
    ps = [jnp.exp2(t - m_b) for t in tiles]
    l = jnp.sum(functools.reduce(jnp.add, ps), axis=-1, keepdims=True)
    if use_sink:
        l = l + jnp.exp2(sink - m)
    p_c = jnp.concatenate([t.astype(BF16) for t in ps[:n_ctx_tiles]], axis=1)
    o = _dot(p_c, v_ref[ctx0:ctx0 + CTX_LEN, :])
    if band:
        p_b = jnp.concatenate([t.astype(BF16) for t in ps[n_ctx_tiles:]], axis=1)
        o = o + _dot(p_b, v_ref[pl.ds(start, nband), :])
    o = o / l
    for g in range(groups):
        o_ref[:, g * LANES:(g + 1) * LANES] = o[g * tq:(g + 1) * tq, :].astype(o_ref.dtype)


def _attn_oneshot(sink, qk, v, *, q_col, k_col, v_col, groups, kv_heads, q_row0, n_q, tq, band, use_sink):
    b = qk.shape[0]
    k_rows = S_ROWS if band else CTX_LEN
    k_row0 = 0 if band else SEQ // CTX_LEN
    gw = groups * LANES
    return pl.pallas_call(
        functools.partial(_attn_oneshot_kernel, groups=groups, tq=tq, band=band, use_sink=use_sink),
        grid=(b, kv_heads, n_q),
        in_specs=[
            pl.BlockSpec(memory_space=pltpu.SMEM),
            pl.BlockSpec((None, tq, gw), lambda bi, h, i: (bi, i + q_row0, q_col // groups + h)),
            pl.BlockSpec((None, k_rows, LANES), lambda bi, h, i: (bi, k_row0, k_col + h)),
            pl.BlockSpec((None, k_rows, LANES), lambda bi, h, i: (bi, k_row0, v_col + h)),
        ],
        out_specs=pl.BlockSpec((None, tq, gw), lambda bi, h, i: (bi, i, h)),
        out_shape=jax.ShapeDtypeStruct((b, n_q * tq, kv_heads * gw), BF16),
        compiler_params=_params(("parallel", "parallel", "parallel"), 32),
        name="attn_oneshot",
    )(sink, qk, qk, v)


def _attn_flash_kernel(lam_ref, subg_ref, q_ref, k_ref, v_ref, o_ref, q_sc, m_sc, l_sc, acc_sc,
                       *, streams, shared_k, tq, tk, diff_scale):
    dv = acc_sc.shape[1]
    for g in range(streams):
        q_sc[g * tq:(g + 1) * tq, :] = q_ref[:, g * LANES:(g + 1) * LANES]
    m_sc[...] = jnp.full(m_sc.shape, NEG_INF, F32)
    l_sc[...] = jnp.zeros(l_sc.shape, F32)
    acc_sc[...] = jnp.zeros(acc_sc.shape, F32)

    def chunk(off, size):
        kc = k_ref[pl.ds(off, size), :]
        if shared_k:
            s = _dot_nt(q_sc[...], kc)
        else:
            s = jnp.concatenate([_dot_nt(q_sc[g * tq:(g + 1) * tq, :], kc[:, g * LANES:(g + 1) * LANES])
                                 for g in range(streams)], axis=0)
        tiles = _lane_tiles(s)
        m_prev = m_sc[...]
        m_new = jnp.maximum(m_prev, jnp.max(functools.reduce(jnp.maximum, tiles), axis=-1, keepdims=True))
        alpha = jnp.exp2(m_prev - m_new)
        ps = [jnp.exp2(t - m_new) for t in tiles]
        m_sc[...] = m_new
        l_sc[...] = alpha * l_sc[...] + functools.reduce(jnp.add, ps)
        p = jnp.concatenate([t.astype(BF16) for t in ps], axis=1)
        pv = _dot(p, v_ref[pl.ds(off, size), :])
        acc_sc[...] = jnp.concatenate([alpha] * (dv // LANES), axis=1) * acc_sc[...] + pv

    def body(c, carry):
        chunk(pl.multiple_of(c * tk, LANES), tk)
        return carry

    if S_ROWS % tk == 0:
        lax.fori_loop(0, S_ROWS // tk, body, 0)
    else:
        chunk(SEQ, CTX_LEN)
        lax.fori_loop(0, SEQ // tk, body, 0)
    o = acc_sc[...] / jnp.sum(l_sc[...], axis=-1, keepdims=True)
    if diff_scale is None:
        for g in range(streams):
            o_ref[:, g * LANES:(g + 1) * LANES] = o[g * tq:(g + 1) * tq, :].astype(o_ref.dtype)
    else:
        lambda_init, out_scale = diff_scale
        lp = lam_ref[...]
        lam = (jnp.exp(jnp.sum(lp[0:1, :] * lp[1:2, :], axis=-1, keepdims=True))
               - jnp.exp(jnp.sum(lp[2:3, :] * lp[3:4, :], axis=-1, keepdims=True)) + lambda_init)
        o = o[0:tq, :] - lam * o[tq:2 * tq, :]
        o_ref[...] = (_rms(o) * subg_ref[...] * out_scale).astype(o_ref.dtype)


def _attn_flash(lam, subg, qk, v, *, q_col, k_col, v_col, streams, shared_k, dv, heads, tq, tk, diff_scale):
    b = qk.shape[0]
    qw = streams * LANES
    kw = LANES if shared_k else streams * LANES
    n_q = SEQ // tq
    ow = dv if diff_scale is not None else qw
    rows = streams * tq
    return pl.pallas_call(
        functools.partial(_attn_flash_kernel, streams=streams, shared_k=shared_k, tq=tq, tk=tk,
                          diff_scale=diff_scale),
        grid=(b, heads, n_q),
        in_specs=[
            pl.BlockSpec((4, LANES), lambda bi, h, i: (0, 0)),
            pl.BlockSpec((1, dv), lambda bi, h, i: (0, 0)),
            pl.BlockSpec((None, tq, qw), lambda bi, h, i: (bi, i, q_col + h)),
            pl.BlockSpec((None, S_ROWS, kw), lambda bi, h, i: (bi, 0, k_col + h)),
            pl.BlockSpec((None, S_ROWS, dv), lambda bi, h, i: (bi, 0, v_col + h)),
        ],
        out_specs=pl.BlockSpec((None, tq, ow), lambda bi, h, i: (bi, i, h)),
        out_shape=jax.ShapeDtypeStruct((b, SEQ, heads * ow), BF16),
        scratch_shapes=[pltpu.VMEM((rows, LANES), BF16), pltpu.VMEM((rows, LANES), F32),
                        pltpu.VMEM((rows, LANES), F32), pltpu.VMEM((rows, dv), F32)],
        compiler_params=_params(("parallel", "parallel", "parallel"), 48),
        name="attn_flash",
    )(lam, subg, qk, qk, v)


def _outproj_kernel(*refs, split):
    if split:
        ya_ref, yb_ref, yac_ref, ybc_ref, xl_ref, xc_ref = refs[:6]
        lat = pl.program_id(1) < _LAT_TILES
        y_in = jnp.concatenate([jnp.where(lat, ya_ref[...], yac_ref[...]),
                                jnp.where(lat, yb_ref[...], ybc_ref[...])], axis=1)
        x_in = jnp.where(lat, xl_ref[...], xc_ref[...])
        refs = refs[6:]
    else:
        y_in, x_in = refs[0][...], refs[1][...]
        refs = refs[2:]
    w_ref, mod_ref, g_ref, wr_ref, xo_ref, hf_ref, aff_ref = refs
    m = mod_ref[...]
    g = g_ref[...]
    y = _dot(y_in, w_ref[...])
    x1 = x_in + m[2:3, :] * (_rms(y) * g[1:2, :])
    xo_ref[...] = x1
    hf = _rms(x1) * g[2:3, :] * (1.0 + m[4:5, :]) + m[3:4, :]
    hf_ref[...] = hf
    logits = _dot(hf.astype(BF16), wr_ref[...]).T[0:N_EXPERTS, :]
    e = jnp.exp(logits - jnp.max(logits, axis=0, keepdims=True))
    aff_ref[...] = e / jnp.sum(e, axis=0, keepdims=True)


def _outproj(ys, xs, w_bf, mod_tab, g4, wr_bf, *, split):
    b, d = xs[0].shape[0], xs[0].shape[2]
    sy = S_ROWS if split else ys[0].shape[1]
    seg = _seg if split else (lambda i: 1)
    row = lambda bi, i: (bi, i, 0)
    if split:
        half = pl.BlockSpec((None, ROW_TILE, d // 2), _lat_row)
        half_c = pl.BlockSpec((None, ROW_TILE, d // 2), _ctx_row)
        data_specs = [half, half, half_c, half_c,
                      pl.BlockSpec((None, ROW_TILE, d), _lat_row), pl.BlockSpec((None, ROW_TILE, d), _ctx_row)]
    else:
        data_specs = [pl.BlockSpec((None, ROW_TILE, d), row), pl.BlockSpec((None, ROW_TILE, d), row)]
    out_f32 = jax.ShapeDtypeStruct((b, sy, d), F32)
    return pl.pallas_call(
        functools.partial(_outproj_kernel, split=split),
        grid=(b, sy // ROW_TILE),
        in_specs=data_specs + [
            pl.BlockSpec((d, d), lambda bi, i: (0, 0)),
            pl.BlockSpec((None, None, 6, d), lambda bi, i: (bi, seg(i), 0, 0)),
            pl.BlockSpec((4, d), lambda bi, i: (0, 0)),
            pl.BlockSpec((d, LANES), lambda bi, i: (0, 0)),
        ],
        out_specs=[pl.BlockSpec((None, ROW_TILE, d), row), pl.BlockSpec((None, ROW_TILE, d), row),
                   pl.BlockSpec((None, N_EXPERTS, ROW_TILE), lambda bi, i: (bi, 0, i))],
        out_shape=[out_f32, out_f32, jax.ShapeDtypeStruct((b, N_EXPERTS, sy), F32)],
        compiler_params=_params(("parallel", "parallel"), 48),
        name="out_proj",
    )(*ys, *xs, w_bf, mod_tab, g4, wr_bf)


def _row_copy(h_hbm, stage_sc, sem, src_row, dst_row, n_rows):
    return pltpu.make_async_copy(h_hbm.at[pl.ds(src_row, n_rows), :], stage_sc.at[pl.ds(dst_row, n_rows), :], sem)


def _moe_kernel(idx_ref, h_hbm, wg_ref, wu_ref, wd_ref, gate_ref, o_ref, stage_sc, x_sc, hmid_sc, sem, *, n_up, n_down):
    e = pl.program_id(0)
    s = pl.program_id(1)
    last = (e == pl.num_programs(0) - 1) & (s == n_up + n_down - 1)
    m = x_sc.shape[0]
    rows_per_step = m // n_up

    def issue_one(expert, row):
        _row_copy(h_hbm, stage_sc, sem, idx_ref[expert, row], row, 1).start()

    @pl.when(s == 0)
    def _():
        @pl.when(e == 0)
        def _():
            def body(i, carry):
                issue_one(0, i)
                return carry

            lax.fori_loop(0, m, body, 0, unroll=8)

        _row_copy(h_hbm, stage_sc, sem, 0, 0, m).wait()
        x_sc[...] = stage_sc[...].astype(BF16)

    @pl.when(s < n_up)
    def _():
        for i in range(rows_per_step):
            issue_one(e + 1, s * rows_per_step + i)
        x = x_sc[...]
        a = _dot(x, wg_ref[...].astype(BF16))
        u = _dot(x, wu_ref[...].astype(BF16))
        hmid_sc[s] = (a * jax.nn.sigmoid(a) * u).astype(BF16)

    @pl.when(s >= n_up)
    def _():
        hmid = jnp.concatenate([hmid_sc[f] for f in range(n_up)], axis=1)
        y = _dot(hmid, wd_ref[...].astype(BF16))
        o_ref[...] = (y * gate_ref[...]).astype(o_ref.dtype)

    @pl.when(last)
    def _():
        _row_copy(h_hbm, stage_sc, sem, 0, 0, m).wait()


def _moe(idx, h, w_g, w_u, w_d, layer, gates):
    e, m = idx.shape
    d = w_g.shape[2]
    f_dim = w_g.shape[-1]
    t_up, t_down = 256, 1024
    n_up, n_down = f_dim // t_up, d // t_down
    idx = jnp.concatenate([idx, jnp.zeros((1, m), idx.dtype)], axis=0)
    up_tile = lambda ei, s, idx_ref: (layer, ei, 0, jnp.minimum(s, n_up - 1))
    down_tile = lambda s: jnp.maximum(s - n_up, 0)
    grid_spec = pltpu.PrefetchScalarGridSpec(
        num_scalar_prefetch=1,
        grid=(e, n_up + n_down),
        in_specs=[
            pl.BlockSpec(memory_space=pl.ANY),
            pl.BlockSpec((None, None, d, t_up), up_tile),
            pl.BlockSpec((None, None, d, t_up), up_tile),
            pl.BlockSpec((None, None, f_dim, t_down), lambda ei, s, idx_ref: (layer, ei, 0, down_tile(s))),
            pl.BlockSpec((None, m, 1), lambda ei, s, idx_ref: (ei, 0, 0)),
        ],
        out_specs=pl.BlockSpec((None, m, t_down), lambda ei, s, idx_ref: (ei, 0, down_tile(s))),
        scratch_shapes=[pltpu.VMEM((m, d), F32), pltpu.VMEM((m, d), BF16), pltpu.VMEM((n_up, m, t_up), BF16),
                        pltpu.SemaphoreType.DMA(())],
    )
    return pl.pallas_call(
        functools.partial(_moe_kernel, n_up=n_up, n_down=n_down),
        grid_spec=grid_spec,
        out_shape=jax.ShapeDtypeStruct((e, m, d), BF16),
        compiler_params=_params(("arbitrary", "arbitrary"), 58),
        name="moe_experts",
    )(idx, h, w_g, w_u, w_d, gates)


_BF16_ROWS = 16
_WIN = LANES + _BF16_ROWS
_WIN_HEAD = 3 * _BF16_ROWS


def _combine_kernel(starts_ref, tab_ref, y_hbm, x_ref, mod_ref, g_ref, *rest, cap_t, n_latent, has_next):
    if has_next:
        modn_ref, gn_ref, xo_ref, h_ref, buf, sem = rest
    else:
        xo_ref, buf, sem = rest
    b = pl.program_id(0)
    j = pl.program_id(1)
    nb = pl.num_programs(0)
    n_tiles = pl.num_programs(1)
    t = b * n_tiles + j
    cur = lax.rem(t, 2)

    def window_start(bb, jj, e):
        s = starts_ref[bb * N_EXPERTS + e, jj]
        return jnp.minimum(s // _BF16_ROWS * _BF16_ROWS, cap_t - _WIN)

    def needs_tail(bb, jj, e):
        return starts_ref[bb * N_EXPERTS + e, jj + 1] > window_start(bb, jj, e) + _WIN_HEAD

    def head_copy(row, e, slot, n_rows):
        return pltpu.make_async_copy(y_hbm.at[pl.ds(row, n_rows), :], buf.at[slot, pl.ds(e * _WIN, n_rows), :],
                                     sem.at[0, slot])

    def tail_copy(row, e, slot):
        return pltpu.make_async_copy(y_hbm.at[pl.ds(row + _WIN_HEAD, _WIN - _WIN_HEAD), :],
                                     buf.at[slot, pl.ds(e * _WIN + _WIN_HEAD, _WIN - _WIN_HEAD), :], sem.at[1, slot])

    def window_row(bb, jj, e):
        return pl.multiple_of((e * nb + bb) * cap_t + window_start(bb, jj, e), _BF16_ROWS)

    def fetch(bb, jj, slot):
        for e in range(N_EXPERTS):
            row = window_row(bb, jj, e)
            head_copy(row, e, slot, _WIN_HEAD).start()

            @pl.when(needs_tail(bb, jj, e))
            def _():
                tail_copy(row, e, slot).start()

    @pl.when(t == 0)
    def _():
        buf[...] = jnp.zeros(buf.shape, buf.dtype)
        fetch(0, 0, 0)

    @pl.when(t + 1 < nb * n_tiles)
    def _():
        t1 = t + 1
        fetch(t1 // n_tiles, lax.rem(t1, n_tiles), 1 - cur)

    for e in range(N_EXPERTS):
        head_copy(0, e, cur, _WIN_HEAD).wait()

        @pl.when(needs_tail(b, j, e))
        def _():
            tail_copy(window_row(b, j, e), e, cur).wait()

    tab = tab_ref[...]
    lane_f = lax.broadcasted_iota(jnp.int32, (LANES, LANES), 1).astype(F32)
    col_of = []
    for e in range(N_EXPERTS):
        slot_e = tab[:, e:e + 1]
        rel = slot_e - window_start(b, j, e).astype(F32) + float(e * _WIN)
        col_of.append(jnp.broadcast_to(jnp.where(slot_e >= 0.0, rel, -1.0), (LANES, LANES)))
    tiles = []
    for i in range(N_EXPERTS * _WIN // LANES):
        cols = lane_f + float(i * LANES)
        e_lo, e_hi = i * LANES // _WIN, (i * LANES + LANES - 1) // _WIN
        hit = col_of[e_lo] == cols
        if e_hi != e_lo:
            hit = hit | (col_of[e_hi] == cols)
        tiles.append(jnp.where(hit, 1.0, 0.0).astype(BF16))
    moe = _dot(jnp.concatenate(tiles, axis=1), buf[cur])

    is_lat = (j * LANES + lax.broadcasted_iota(jnp.int32, (LANES, 1), 0)) < n_latent

    def mod(ref, kk):
        return jnp.where(is_lat, ref[1, kk:kk + 1, :], ref[0, kk:kk + 1, :])

    x2 = x_ref[...] + mod(mod_ref, 5) * (_rms(moe) * g_ref[3:4, :])
    xo_ref[...] = x2
    if has_next:
        h_ref[...] = (_rms(x2) * gn_ref[0:1, :] * (1.0 + mod(modn_ref, 1)) + mod(modn_ref, 0)).astype(h_ref.dtype)


def _combine(starts, tab, y, x, mod_tab, g4, mod_next, g4_next, *, n_latent):
    b, sx, d = x.shape
    cap_t = y.shape[0] // (b * N_EXPERTS)
    has_next = mod_next is not None
    row = pl.BlockSpec((None, LANES, d), lambda bi, i, s: (bi, i, 0))
    mod_spec = pl.BlockSpec((None, 2, 6, d), lambda bi, i, s: (bi, 0, 0, 0))
    g_spec = pl.BlockSpec((4, d), lambda bi, i, s: (0, 0))
    in_specs = [pl.BlockSpec((None, LANES, LANES), lambda bi, i, s: (bi, i, 0)),
                pl.BlockSpec(memory_space=pl.ANY), row, mod_spec, g_spec]
    ins = [tab, y, x, mod_tab, g4]
    out_specs, out_shape = [row], [jax.ShapeDtypeStruct((b, sx, d), F32)]
    if has_next:
        in_specs += [mod_spec, g_spec]
        ins += [mod_next, g4_next]
        out_specs.append(row)
        out_shape.append(jax.ShapeDtypeStruct((b, sx, d), BF16))
    grid_spec = pltpu.PrefetchScalarGridSpec(
        num_scalar_prefetch=1, grid=(b, sx // LANES), in_specs=in_specs, out_specs=out_specs,
        scratch_shapes=[pltpu.VMEM((2, N_EXPERTS * _WIN, d), BF16), pltpu.SemaphoreType.DMA((2, 2))])
    return pl.pallas_call(
        functools.partial(_combine_kernel, cap_t=cap_t, n_latent=n_latent, has_next=has_next),
        grid_spec=grid_spec, out_shape=out_shape,
        compiler_params=_params(("arbitrary", "arbitrary"), 48), name="moe_combine",
    )(starts, *ins)


def _prefix_count(mask_tiles, tri):
    off = jnp.zeros((N_EXPERTS, 1), F32)
    out, before = [], []
    for mt in mask_tiles:
        before.append(off)
        w = _dot(jnp.where(mt, 1.0, 0.0).astype(BF16), tri) + off
        out.append(w)
        off = w[:, LANES - 1:LANES]
    return out, before


_MIN_EXP = -160.0
_EXP_STEPS = 8
_VALUE_STEPS = 40


def _route_kernel(aff_ref, tab_ref, starts_ref, slot_ref, atile_ref, *, segments):
    r_i = lax.broadcasted_iota(jnp.int32, (LANES, LANES), 0)
    c_i = lax.broadcasted_iota(jnp.int32, (LANES, LANES), 1)
    tri = jnp.where(r_i <= c_i, 1.0, 0.0).astype(BF16)
    shape = (N_EXPERTS, LANES)
    lane_e = lax.broadcasted_iota(jnp.int32, shape, 1)
    starts = jnp.zeros(shape, F32)
    base = 0
    tile0 = 0
    for t0, n, cap in segments:
        nt = n // LANES
        a_t = [aff_ref[:, t0 + j * LANES:t0 + (j + 1) * LANES] for j in range(nt)]

        def count_ge(v, a_t=a_t):
            v_b = jnp.broadcast_to(v, shape)
            cnt = functools.reduce(jnp.add, [jnp.where(t >= v_b, 1.0, 0.0) for t in a_t])
            return jnp.sum(cnt, axis=1, keepdims=True)

        e_lo = jnp.full((N_EXPERTS, 1), _MIN_EXP, F32)
        e_hi = jnp.full((N_EXPERTS, 1), 1.0, F32)
        lo = jnp.zeros((N_EXPERTS, 1), F32)
        hi = jnp.full((N_EXPERTS, 1), 2.0, F32)
        for _ in range(_EXP_STEPS):
            e_mid = jnp.floor((e_lo + e_hi) * 0.5)
            v = jnp.exp2(e_mid)
            ok = count_ge(v) >= cap
            e_lo, lo = jnp.where(ok, e_mid, e_lo), jnp.where(ok, v, lo)
            e_hi, hi = jnp.where(ok, e_hi, e_mid), jnp.where(ok, hi, v)
        for _ in range(_VALUE_STEPS):
            v = (lo + hi) * 0.5
            ok = count_ge(v) >= cap
            lo, hi = jnp.where(ok, v, lo), jnp.where(ok, hi, v)
        lo_b = jnp.broadcast_to(lo, shape)
        hi_b = jnp.broadcast_to(hi, shape)
        gt = [t >= hi_b for t in a_t]
        eq = [(t >= lo_b) & (t < hi_b) for t in a_t]
        need = jnp.broadcast_to(cap - count_ge(hi), shape)
        eq_rank, _ = _prefix_count(eq, tri)
        sel = [g | (q & (r <= need)) for g, q, r in zip(gt, eq, eq_rank)]
        pos, before = _prefix_count(sel, tri)
        for j in range(nt):
            starts = jnp.where(lane_e == tile0 + j, before[j] + float(base), starts)
            slot = jnp.where(sel[j], pos[j] + (base - 1.0), -1.0)
            slot_ref[tile0 + j] = slot
            atile_ref[tile0 + j] = a_t[j]
            padded =jnp.concatenate([slot, jnp.full((LANES - N_EXPERTS, LANES), -1.0, F32)], axis=0)
            tab_ref[t0 + j * LANES:t0 + (j + 1) * LANES, :] = padded.T
        base += cap
        tile0 += nt
    starts_ref[...] = jnp.where(lane_e == tile0, float(base), starts)


def _compact_kernel(starts_ref, slot_ref, aff_ref, idx_ref, gate_ref, acc_i, acc_g, *, segments, rows_per_sample):
    bi = pl.program_id(0)
    row0 = (bi * rows_per_sample).astype(F32)
    r_i = lax.broadcasted_iota(jnp.int32, (LANES, LANES), 0)
    lane1 = lax.broadcasted_iota(jnp.int32, (1, LANES), 1)
    base = 0
    tile0 = 0
    for t0, n, cap in segments:
        nt = n // LANES

        def expert_body(e, carry, t0=t0, nt=nt, cap=cap, base=base, tile0=tile0):
            srow = bi * N_EXPERTS + e
            n_chunks = -(-cap // LANES)

            def scan(j, bounds):
                first = starts_ref[srow, tile0 + j]
                nxt = jnp.where(j + 1 < nt, starts_ref[srow, tile0 + jnp.minimum(j + 1, nt - 1)], base + cap)
                out = []
                for c in range(n_chunks):
                    lo = base + c * LANES
                    out.append(bounds[2 * c] + (nxt <= lo).astype(jnp.int32))
                    out.append(bounds[2 * c + 1] + (first < lo + LANES).astype(jnp.int32))
                return tuple(out)

            bounds = lax.fori_loop(0, nt, scan, (jnp.int32(0),) * (2 * n_chunks), unroll=4)
            for c in range(n_chunks):
                lo = base + c * LANES
                j_lo, j_hi = bounds[2 * c], bounds[2 * c + 1]
                slot_id = (r_i + lo).astype(F32)
                acc_i[...] = jnp.zeros((LANES, LANES), F32)
                acc_g[...] = jnp.zeros((LANES, LANES), F32)

                def tile_body(j, carry2):
                    pe = jnp.broadcast_to(slot_ref[tile0 + j, pl.ds(e, 1), :], (LANES, LANES))
                    ae = jnp.broadcast_to(aff_ref[tile0 + j, pl.ds(e, 1), :], (LANES, LANES))
                    tok = jnp.broadcast_to((lane1 + (t0 + j * LANES)).astype(F32) + row0, (LANES, LANES))
                    hit = pe == slot_id
                    acc_i[...] += jnp.where(hit, tok, 0.0)
                    acc_g[...] += jnp.where(hit, ae, 0.0)
                    return carry2

                lax.fori_loop(j_lo, j_hi, tile_body, 0)
                w = min(LANES, cap - c * LANES)
                col_i = jnp.broadcast_to(jnp.sum(acc_i[...], axis=1, keepdims=True), (LANES, LANES))
                col_g = jnp.broadcast_to(jnp.sum(acc_g[...], axis=1, keepdims=True), (LANES, LANES))
                idx_ref[e, :, lo:lo + w] = col_i.T[0:1, 0:w].astype(jnp.int32)
                gate_ref[e, :, lo:lo + w] = col_g.T[0:1, 0:w]
            return carry

        lax.fori_loop(0, N_EXPERTS, expert_body, 0)
        base += cap
        tile0 += nt


def _route(aff_t, segments, rows_per_sample):
    b, e, sy = aff_t.shape
    k = sum(cap for _, _, cap in segments)
    n_tiles = sum(n // LANES for _, n, _ in segments)
    tiles_spec = pl.BlockSpec((None, n_tiles, e, LANES), lambda bi: (bi, 0, 0, 0))
    tiles_shape = jax.ShapeDtypeStruct((b, n_tiles, e, LANES), F32)
    tab, starts, slots, a_tiles = pl.pallas_call(
        functools.partial(_route_kernel, segments=segments),
        grid=(b,),
        in_specs=[pl.BlockSpec((None, e, sy), lambda bi: (bi, 0, 0))],
        out_specs=[pl.BlockSpec((None, sy, LANES), lambda bi: (bi, 0, 0)),
                   pl.BlockSpec((None, e, LANES), lambda bi: (bi, 0, 0)), tiles_spec, tiles_spec],
        out_shape=[jax.ShapeDtypeStruct((b, sy, LANES), F32), jax.ShapeDtypeStruct((b, e, LANES), F32),
                   tiles_shape, tiles_shape],
        compiler_params=_params(("parallel",), 32),
        name="route_topk",
    )(aff_t)
    starts = starts.astype(jnp.int32).reshape(b * e, LANES)
    grid_spec = pltpu.PrefetchScalarGridSpec(
        num_scalar_prefetch=1, grid=(b,),
        in_specs=[pl.BlockSpec((None, n_tiles, e, LANES), lambda bi, s: (bi, 0, 0, 0)),
                  pl.BlockSpec((None, n_tiles, e, LANES), lambda bi, s: (bi, 0, 0, 0))],
        out_specs=[pl.BlockSpec((None, e, 1, k), lambda bi, s: (bi, 0, 0, 0)),
                   pl.BlockSpec((None, e, 1, k), lambda bi, s: (bi, 0, 0, 0))],
        scratch_shapes=[pltpu.VMEM((LANES, LANES), F32), pltpu.VMEM((LANES, LANES), F32)])
    idx, gates = pl.pallas_call(
        functools.partial(_compact_kernel, segments=segments, rows_per_sample=rows_per_sample),
        grid_spec=grid_spec,
        out_shape=[jax.ShapeDtypeStruct((b, e, 1, k), jnp.int32), jax.ShapeDtypeStruct((b, e, 1, k), F32)],
        compiler_params=_params(("parallel",), 32),
        name="route_compact",
    )(starts, slots, a_tiles)
    return tab, idx, gates, starts


def _moe_block(x, hf, aff_t, segments, w_g, w_u, w_d, layer, mod_tab, g4, mod_next, g4_next, *, n_latent):
    b, sy = aff_t.shape[0], aff_t.shape[2]
    tab, idx, gates, starts = _route(aff_t, segments, sy)
    k = idx.shape[-1]
    idx_em = jnp.swapaxes(idx[:, :, 0, :], 0, 1).reshape(N_EXPERTS, b * k)
    gates_em = jnp.swapaxes(gates[:, :, 0, :], 0, 1).reshape(N_EXPERTS, b * k, 1)
    y = _moe(idx_em, hf.reshape(b * sy, hf.shape[-1]), w_g, w_u, w_d, layer, gates_em)
    y2d = y.reshape(N_EXPERTS * b * k, y.shape[-1])
    return _combine(starts, tab, y2d, x, mod_tab, g4, mod_next, g4_next, n_latent=n_latent)


_SCALE = HEAD_DIM ** -0.5 * LOG2E
_PLAN_AB = tuple([(s, None, True) for s in range(0, 8)] + [(s, "q", True) for s in range(12, 20)]
                 + [(s, None, False) for s in range(8, 10)] + [(s, "k", False) for s in range(20, 22)])
_PLAN_C = tuple([(s, None, True) for s in range(0, 16)] + [(s, None, False) for s in range(16, 32)])


def kernel(x, c, ctx, c_ctx, mod_w, mod_b, norm_g, ab_w_in, ab_w_out, ab_sink, ab_q_norm, ab_k_norm,
           dif_w_in, dif_w_out, dif_lambda, dif_subln, router_w, exp_w_gate, exp_w_up, exp_w_down):
    b = x.shape[0]
    d = D_MODEL
    cv = jnp.concatenate([c, c_ctx[None, :], jnp.zeros((8 - b - 1, d), F32)], axis=0)
    mods = _mods(cv, mod_w, mod_b).reshape(DEPTH, 8, 6, d)
    mod_tab = jnp.stack([jnp.broadcast_to(mods[:, b][:, None], (DEPTH, b, 6, d)), mods[:, :b]], axis=2)
    tables = _rope_tables(_SCALE)
    ones = jnp.ones((1, LANES), F32)
    wr = jnp.pad(router_w, ((0, 0), (0, 0), (0, LANES - N_EXPERTS))).astype(BF16)

    h0 = _norm_mod(x, ctx, norm_g[0], mod_tab[0])
    p0 = _matmul(h0.reshape(b * S_ROWS, d), ab_w_in[0], BF16).reshape(b, S_ROWS, AB_IN)
    qk0 = _prep(p0, tables, ab_q_norm[0][None, :], ab_k_norm[0][None, :], _PLAN_AB)
    sink = ab_sink[0]
    common_a = dict(q_col=0, k_col=16, v_col=10, groups=4, kv_heads=A_KV_HEADS)
    common_b = dict(q_col=8, k_col=18, v_col=22, groups=4, kv_heads=B_KV_HEADS)
    ctx_q = dict(q_row0=SEQ // CTX_LEN, n_q=1, tq=CTX_LEN, band=False)
    ya_l = _attn_oneshot(sink, qk0, p0, **common_a, q_row0=0, n_q=SEQ // 256, tq=256, band=True, use_sink=True)
    ya_c = _attn_oneshot(sink, qk0, p0, **common_a, **ctx_q, use_sink=True)
    yb_c = _attn_oneshot(sink, qk0, p0, **common_b, **ctx_q, use_sink=False)
    yb_l = _attn_flash(jnp.zeros((4, LANES), F32), ones, qk0, p0, q_col=2, k_col=18, v_col=22, streams=4,
                       shared_k=True, dv=HEAD_DIM, heads=B_KV_HEADS, tq=256, tk=2048, diff_scale=None)
    x1, hf0, aff0 = _outproj((ya_l, yb_l, ya_c, yb_c), (x, ctx), _cast_bf16(ab_w_out[0]), mod_tab[0], norm_g[0],
                             wr[0], split=True)
    cap_l = EC_FACTOR * SEQ // N_EXPERTS
    cap_c = EC_FACTOR * CTX_LEN // N_EXPERTS
    x2, h1 = _moe_block(x1, hf0, aff0, ((0, SEQ, cap_l), (SEQ, CTX_LEN, cap_c)),
                        exp_w_gate, exp_w_up, exp_w_down, 0, mod_tab[0], norm_g[0], mod_tab[1], norm_g[1],
                        n_latent=SEQ)

    p1 = _matmul(h1.reshape(b * S_ROWS, d), dif_w_in[0], BF16).reshape(b, S_ROWS, C_IN)
    qk1 = _prep(p1, tables, ones, ones, _PLAN_C)
    lambda_init = 0.8 - 0.6 * math.exp(-0.3 * 1)
    y1 = _attn_flash(dif_lambda[0], dif_subln[0][None, :], qk1, p1, q_col=0, k_col=8, v_col=16, streams=2,
                     shared_k=False, dv=C_V_DIM, heads=C_HEADS, tq=512, tk=2048,
                     diff_scale=(lambda_init, 1.0 - lambda_init))
    x3, hf1, aff1 = _outproj((y1,), (x2,), _cast_bf16(dif_w_out[0]), mod_tab[1], norm_g[1], wr[1], split=False)
    (out,) = _moe_block(x3, hf1, aff1, ((0, SEQ, cap_l),), exp_w_gate, exp_w_up, exp_w_down, 1,
                        mod_tab[1], norm_g[1], None, None, n_latent=SEQ)
    return out
```

```python
import functools
import math

import jax
import jax.numpy as jnp
from jax import lax
from jax.experimental import pallas as pl
from jax.experimental.pallas import tpu as pltpu

D_MODEL = 2048
SEQ = 4096
DEPTH = 2
GRID_W = 64
CTX_LEN = 256
HEAD_DIM = 128
ROPE_BASE = 10000.0
BLOCK = 128
WINDOW = 128
A_Q_HEADS = 8
A_KV_HEADS = 2
B_Q_HEADS = 8
B_KV_HEADS = 2
AB_IN = 3072
C_HEADS = 8
C_V_DIM = 256
C_IN = 6144
N_EXPERTS = 16
EC_FACTOR = 2
EPS = 1e-6
NEG_INF = -1e30

S_ROWS = SEQ + CTX_LEN
ROW_TILE = 256
LANES = 128
MIB = 1024 * 1024
LOG2E = math.log2(math.e)

F32 = jnp.float32
BF16 = jnp.bfloat16


def _params(sem, vmem_mib):
    return pltpu.CompilerParams(dimension_semantics=sem, vmem_limit_bytes=vmem_mib * MIB)


def _rms(x):
    return x * lax.rsqrt(jnp.mean(x * x, axis=-1, keepdims=True) + EPS)


def _dot(a, b):
    return jnp.dot(a, b, preferred_element_type=F32)


def _dot_nt(a, b):
    return lax.dot_general(a, b, (((1,), (1,)), ((), ())), preferred_element_type=F32)


def _mods_kernel(cv_ref, w_ref, b_ref, o_ref):
    cv = cv_ref[...]
    s = cv * jax.nn.sigmoid(cv)
    o_ref[...] = _dot(s.astype(BF16), w_ref[...].astype(BF16)) + b_ref[...]


def _mods(cv, mod_w, mod_b):
    tn = 1024
    n = mod_w.shape[-1]
    return pl.pallas_call(
        _mods_kernel,
        grid=(DEPTH, n // tn),
        in_specs=[
            pl.BlockSpec((8, D_MODEL), lambda l, j: (0, 0)),
            pl.BlockSpec((None, D_MODEL, tn), lambda l, j: (l, 0, j)),
            pl.BlockSpec((None, 1, tn), lambda l, j: (l, 0, j)),
        ],
        out_specs=pl.BlockSpec((None, 8, tn), lambda l, j: (l, 0, j)),
        out_shape=jax.ShapeDtypeStruct((DEPTH, 8, n), F32),
        compiler_params=_params(("parallel", "parallel"), 40),
        name="mods",
    )(cv, mod_w, mod_b.reshape(DEPTH, 1, n))


def _cast_kernel(x_ref, o_ref):
    o_ref[...] = x_ref[...].astype(o_ref.dtype)


def _cast_bf16(w):
    k, n = w.shape
    tk = 512
    return pl.pallas_call(
        _cast_kernel,
        grid=(k // tk,),
        in_specs=[pl.BlockSpec((tk, n), lambda i: (i, 0))],
        out_specs=pl.BlockSpec((tk, n), lambda i: (i, 0)),
        out_shape=jax.ShapeDtypeStruct((k, n), BF16),
        compiler_params=_params(("parallel",), 48),
        name="cast_bf16",
    )(w)


def _seg(i):
    return jnp.where(i < SEQ // ROW_TILE, 1, 0)


_LAT_TILES = SEQ // ROW_TILE


def _lat_row(bi, i):
    return (bi, jnp.minimum(i, _LAT_TILES - 1), 0)


def _ctx_row(bi, i):
    return (bi, 0, 0)


def _norm_mod_kernel(x_ref, ctx_ref, g_ref, mod_ref, h_ref):
    m = mod_ref[...]
    xv = jnp.where(pl.program_id(1) < _LAT_TILES, x_ref[...], ctx_ref[...])
    h = _rms(xv) * g_ref[0:1, :] * (1.0 + m[1:2, :]) + m[0:1, :]
    h_ref[...] = h.astype(h_ref.dtype)


def _norm_mod(x, ctx, g4, mod_tab):
    b, _, d = x.shape
    return pl.pallas_call(
        _norm_mod_kernel,
        grid=(b, S_ROWS // ROW_TILE),
        in_specs=[
            pl.BlockSpec((None, ROW_TILE, d), _lat_row),
            pl.BlockSpec((None, ROW_TILE, d), _ctx_row),
            pl.BlockSpec((4, d), lambda bi, i: (0, 0)),
            pl.BlockSpec((None, None, 6, d), lambda bi, i: (bi, _seg(i), 0, 0)),
        ],
        out_specs=pl.BlockSpec((None, ROW_TILE, d), lambda bi, i: (bi, i, 0)),
        out_shape=jax.ShapeDtypeStruct((b, S_ROWS, d), BF16),
        compiler_params=_params(("parallel", "parallel"), 32),
        name="norm_mod",
    )(x, ctx, g4, mod_tab)


def _mm_kernel(a_ref, w_ref, o_ref, w_sc):
    @pl.when(pl.program_id(1) == 0)
    def _():
        w_sc[...] = w_ref[...].astype(BF16)

    o_ref[...] = _dot(a_ref[...], w_sc[...]).astype(o_ref.dtype)


def _matmul(a, w, out_dtype):
    m, k = a.shape
    n = w.shape[1]
    tm, tn = 1088, 1024
    return pl.pallas_call(
        _mm_kernel,
        grid=(n // tn, m // tm),
        in_specs=[
            pl.BlockSpec((tm, k), lambda j, i: (i, 0)),
            pl.BlockSpec((k, tn), lambda j, i: (0, j)),
        ],
        out_specs=pl.BlockSpec((tm, tn), lambda j, i: (i, j)),
        out_shape=jax.ShapeDtypeStruct((m, n), out_dtype),
        scratch_shapes=[pltpu.VMEM((k, tn), BF16)],
        compiler_params=_params(("parallel", "arbitrary"), 48),
        name="in_proj",
    )(a, w)


def _rope_tables(q_scale):
    n_rows = SEQ // GRID_W
    row = jnp.repeat(jnp.arange(n_rows), GRID_W).astype(F32)
    col = jnp.tile(jnp.arange(GRID_W), n_rows).astype(F32)
    n_freq = HEAD_DIM // 4
    inv = ROPE_BASE ** (-jnp.arange(n_freq, dtype=F32) / n_freq)
    ar = row[:, None] * inv
    ac = col[:, None] * inv
    cr, sr, cc, sc = jnp.cos(ar), jnp.sin(ar), jnp.cos(ac), jnp.sin(ac)
    c = jnp.concatenate([cr, cr, cc, cc], axis=-1)
    s = jnp.concatenate([-sr, sr, -sc, sc], axis=-1)
    pad = jnp.zeros((CTX_LEN, HEAD_DIM), F32)
    c = jnp.concatenate([c, pad + 1.0], axis=0)
    s = jnp.concatenate([s, pad], axis=0)
    return c, s, c * q_scale, s * q_scale


def _prep_kernel(p_ref, c_ref, s_ref, cq_ref, sq_ref, gq_ref, gk_ref, o_ref, *, plan):
    lane = lax.broadcasted_iota(jnp.int32, (1, LANES), 1)
    first = (lane & (HEAD_DIM // 4)) == 0
    for s, (src, norm, scaled) in enumerate(plan):
        x = p_ref[:, src * LANES:(src + 1) * LANES].astype(F32)
        if norm is not None:
            x = _rms(x) * (gq_ref[...] if norm == "q" else gk_ref[...])
        partner = jnp.where(first, pltpu.roll(x, 96, 1), pltpu.roll(x, 32, 1))
        c, sn = (cq_ref[...], sq_ref[...]) if scaled else (c_ref[...], s_ref[...])
        o_ref[:, s * LANES:(s + 1) * LANES] = (x * c + partner * sn).astype(o_ref.dtype)


def _prep(p, tables, gq, gk, plan):
    b, s, n = p.shape
    n_out = len(plan) * LANES
    tab_spec = pl.BlockSpec((ROW_TILE, LANES), lambda bi, i: (i, 0))
    g_spec = pl.BlockSpec((1, LANES), lambda bi, i: (0, 0))
    return pl.pallas_call(
        functools.partial(_prep_kernel, plan=plan),
        grid=(b, s // ROW_TILE),
        in_specs=[pl.BlockSpec((None, ROW_TILE, n), lambda bi, i: (bi, i, 0)),
                  tab_spec, tab_spec, tab_spec, tab_spec, g_spec, g_spec],
        out_specs=pl.BlockSpec((None, ROW_TILE, n_out), lambda bi, i: (bi, i, 0)),
        out_shape=jax.ShapeDtypeStruct((b, s, n_out), BF16),
        compiler_params=_params(("parallel", "parallel"), 32),
        name="qk_prep",
    )(p, *tables, gq, gk)


def _lane_tiles(s):
    return [s[:, j * LANES:(j + 1) * LANES] for j in range(s.shape[1] // LANES)]


def _stack_heads(q_ref, groups):
    return jnp.concatenate([q_ref[:, g * LANES:(g + 1) * LANES] for g in range(groups)], axis=0)


def _attn_oneshot_kernel(sink_ref, q_ref, k_ref, v_ref, o_ref, *, groups, tq, band, use_sink):
    h = pl.program_id(1)
    qi = pl.program_id(2)
    part = groups // 2
    rows = part * tq
    ctx0 = SEQ if band else 0
    if band:
        nband = tq + 2 * BLOCK
        start = jnp.clip(qi * tq - BLOCK, 0, SEQ - nband)
        start = pl.multiple_of(start, BLOCK)
        row = lax.broadcasted_iota(jnp.int32, (rows, nband), 0)
        q_pos = qi * tq + (row & (tq - 1))
        k_pos = start + lax.broadcasted_iota(jnp.int32, (rows, nband), 1)
        valid = jnp.abs(q_pos - k_pos) <= WINDOW
    row1 = lax.broadcasted_iota(jnp.int32, (rows, 1), 0)
    for g0 in range(0, groups, part):
        q = jnp.concatenate([q_ref[:, g * LANES:(g + 1) * LANES] for g in range(g0, g0 + part)], axis=0)
        tiles = _lane_tiles(_dot_nt(q, k_ref[ctx0:ctx0 + CTX_LEN, :]))
        n_ctx_tiles = len(tiles)
        if band:
            tiles += _lane_tiles(jnp.where(valid, _dot_nt(q, k_ref[pl.ds(start, nband), :]), NEG_INF))
        m = jnp.max(functools.reduce(jnp.maximum, tiles), axis=-1, keepdims=True)
        if use_sink:
            sink = jnp.zeros((rows, 1), F32)
            for g in range(part):
                sink = jnp.where(row1 >= g * tq, sink_ref[h * groups + g0 + g] * LOG2E, sink)
            m = jnp.maximum(m, sink)
        m_b = jnp.broadcast_to(m, (rows, LANES))
        ps = [jnp.exp2(t - m_b) for t in tiles]
        l = jnp.sum(functools.reduce(jnp.add, ps), axis=-1, keepdims=True)
        if use_sink:
            l = l + jnp.exp2(sink - m)
        p_c = jnp.concatenate([t.astype(BF16) for t in ps[:n_ctx_tiles]], axis=1)
        o = _dot(p_c, v_ref[ctx0:ctx0 + CTX_LEN, :])
        if band:
            p_b = jnp.concatenate([t.astype(BF16) for t in ps[n_ctx_tiles:]], axis=1)
            o = o + _dot(p_b, v_ref[pl.ds(start, nband), :])
        o = o / l
        for g in range(part):
            o_ref[:, (g0 + g) * LANES:(g0 + g + 1) * LANES] = o[g * tq:(g + 1) * tq, :].astype(o_ref.dtype)


def _attn_oneshot(sink, qk, v, *, q_col, k_col, v_col, groups, kv_heads, q_row0, n_q, tq, band, use_sink):
    b = qk.shape[0]
    k_rows = S_ROWS if band else CTX_LEN
    k_row0 = 0 if band else SEQ // CTX_LEN
    gw = groups * LANES
    return pl.pallas_call(
        functools.partial(_attn_oneshot_kernel, groups=groups, tq=tq, band=band, use_sink=use_sink),
        grid=(b, kv_heads, n_q),
        in_specs=[
            pl.BlockSpec(memory_space=pltpu.SMEM),
            pl.BlockSpec((None, tq, gw), lambda bi, h, i: (bi, i + q_row0, q_col // groups + h)),
            pl.BlockSpec((None, k_rows, LANES), lambda bi, h, i: (bi, k_row0, k_col + h)),
            pl.BlockSpec((None, k_rows, LANES), lambda bi, h, i: (bi, k_row0, v_col + h)),
        ],
        out_specs=pl.BlockSpec((None, tq, gw), lambda bi, h, i: (bi, i, h)),
        out_shape=jax.ShapeDtypeStruct((b, n_q * tq, kv_heads * gw), BF16),
        compiler_params=_params(("parallel", "parallel", "parallel"), 32),
        name="attn_oneshot",
    )(sink, qk, qk, v)


def _attn_flash_kernel(lam_ref, subg_ref, q_ref, k_ref, v_ref, o_ref, q_sc, m_sc, l_sc, acc_sc,
                       *, streams, shared_k, tq, tk, diff_scale):
    dv = acc_sc.shape[1]
    for g in range(streams):
        q_sc[g * tq:(g + 1) * tq, :] = q_ref[:, g * LANES:(g + 1) * LANES]
    m_sc[...] = jnp.full(m_sc.shape, NEG_INF, F32)
    l_sc[...] = jnp.zeros(l_sc.shape, F32)
    acc_sc[...] = jnp.zeros(acc_sc.shape, F32)

    def chunk(off, size):
        kc = k_ref[pl.ds(off, size), :]
        if shared_k:
            s = _dot_nt(q_sc[...], kc)
        else:
            s = jnp.concatenate([_dot_nt(q_sc[g * tq:(g + 1) * tq, :], kc[:, g * LANES:(g + 1) * LANES])
                                 for g in range(streams)], axis=0)
        tiles = _lane_tiles(s)
        m_prev = m_sc[...]
        m_new = jnp.maximum(m_prev, jnp.max(functools.reduce(jnp.maximum, tiles), axis=-1, keepdims=True))
        alpha = jnp.exp2(m_prev - m_new)
        ps = [jnp.exp2(t - m_new) for t in tiles]
        m_sc[...] = m_new
        l_sc[...] = alpha * l_sc[...] + functools.reduce(jnp.add, ps)
        p = jnp.concatenate([t.astype(BF16) for t in ps], axis=1)
        pv = _dot(p, v_ref[pl.ds(off, size), :])
        acc_sc[...] = jnp.concatenate([alpha] * (dv // LANES), axis=1) * acc_sc[...] + pv

    def body(c, carry):
        chunk(pl.multiple_of(c * tk, LANES), tk)
        return carry

    if S_ROWS % tk == 0:
        lax.fori_loop(0, S_ROWS // tk, body, 0)
    else:
        chunk(SEQ, CTX_LEN)
        lax.fori_loop(0, SEQ // tk, body, 0)
    o = acc_sc[...] / jnp.sum(l_sc[...], axis=-1, keepdims=True)
    if diff_scale is None:
        for g in range(streams):
            o_ref[:, g * LANES:(g + 1) * LANES] = o[g * tq:(g + 1) * tq, :].astype(o_ref.dtype)
    else:
        lambda_init, out_scale = diff_scale
        lp = lam_ref[...]
        lam = (jnp.exp(jnp.sum(lp[0:1, :] * lp[1:2, :], axis=-1, keepdims=True))
               - jnp.exp(jnp.sum(lp[2:3, :] * lp[3:4, :], axis=-1, keepdims=True)) + lambda_init)
        o = o[0:tq, :] - lam * o[tq:2 * tq, :]
        o_ref[...] = (_rms(o) * subg_ref[...] * out_scale).astype(o_ref.dtype)


def _attn_flash(lam, subg, qk, v, *, q_col, k_col, v_col, streams, shared_k, dv, heads, tq, tk, diff_scale):
    b = qk.shape[0]
    qw = streams * LANES
    kw = LANES if shared_k else streams * LANES
    n_q = SEQ // tq
    ow = dv if diff_scale is not None else qw
    rows = streams * tq
    return pl.pallas_call(
        functools.partial(_attn_flash_kernel, streams=streams, shared_k=shared_k, tq=tq, tk=tk,
                          diff_scale=diff_scale),
        grid=(b, heads, n_q),
        in_specs=[
            pl.BlockSpec((4, LANES), lambda bi, h, i: (0, 0)),
            pl.BlockSpec((1, dv), lambda bi, h, i: (0, 0)),
            pl.BlockSpec((None, tq, qw), lambda bi, h, i: (bi, i, q_col + h)),
            pl.BlockSpec((None, S_ROWS, kw), lambda bi, h, i: (bi, 0, k_col + h)),
            pl.BlockSpec((None, S_ROWS, dv), lambda bi, h, i: (bi, 0, v_col + h)),
        ],
        out_specs=pl.BlockSpec((None, tq, ow), lambda bi, h, i: (bi, i, h)),
        out_shape=jax.ShapeDtypeStruct((b, SEQ, heads * ow), BF16),
        scratch_shapes=[pltpu.VMEM((rows, LANES), BF16), pltpu.VMEM((rows, LANES), F32),
                        pltpu.VMEM((rows, LANES), F32), pltpu.VMEM((rows, dv), F32)],
        compiler_params=_params(("parallel", "parallel", "parallel"), 48),
        name="attn_flash",
    )(lam, subg, qk, qk, v)


def _outproj_kernel(*refs, split):
    if split:
        ya_ref, yb_ref, yac_ref, ybc_ref, xl_ref, xc_ref = refs[:6]
        lat = pl.program_id(1) < _LAT_TILES
        y_in = jnp.concatenate([jnp.where(lat, ya_ref[...], yac_ref[...]),
                                jnp.where(lat, yb_ref[...], ybc_ref[...])], axis=1)
        x_in = jnp.where(lat, xl_ref[...], xc_ref[...])
        refs = refs[6:]
    else:
        y_in, x_in = refs[0][...], refs[1][...]
        refs = refs[2:]
    w_ref, mod_ref, g_ref, wr_ref, xo_ref, hf_ref, aff_ref = refs
    m = mod_ref[...]
    g = g_ref[...]
    y = _dot(y_in, w_ref[...])
    x1 = x_in + m[2:3, :] * (_rms(y) * g[1:2, :])
    xo_ref[...] = x1
    hf = _rms(x1) * g[2:3, :] * (1.0 + m[4:5, :]) + m[3:4, :]
    hf_ref[...] = hf
    logits = _dot(hf.astype(BF16), wr_ref[...]).T[0:N_EXPERTS, :]
    e = jnp.exp(logits - jnp.max(logits, axis=0, keepdims=True))
    aff_ref[...] = e / jnp.sum(e, axis=0, keepdims=True)


def _outproj(ys, xs, w_bf, mod_tab, g4, wr_bf, *, split):
    b, d = xs[0].shape[0], xs[0].shape[2]
    sy = S_ROWS if split else ys[0].shape[1]
    seg = _seg if split else (lambda i: 1)
    row = lambda bi, i: (bi, i, 0)
    if split:
        half = pl.BlockSpec((None, ROW_TILE, d // 2), _lat_row)
        half_c = pl.BlockSpec((None, ROW_TILE, d // 2), _ctx_row)
        data_specs = [half, half, half_c, half_c,
                      pl.BlockSpec((None, ROW_TILE, d), _lat_row), pl.BlockSpec((None, ROW_TILE, d), _ctx_row)]
    else:
        data_specs = [pl.BlockSpec((None, ROW_TILE, d), row), pl.BlockSpec((None, ROW_TILE, d), row)]
    out_f32 = jax.ShapeDtypeStruct((b, sy, d), F32)
    return pl.pallas_call(
        functools.partial(_outproj_kernel, split=split),
        grid=(b, sy // ROW_TILE),
        in_specs=data_specs + [
            pl.BlockSpec((d, d), lambda bi, i: (0, 0)),
            pl.BlockSpec((None, None, 6, d), lambda bi, i: (bi, seg(i), 0, 0)),
            pl.BlockSpec((4, d), lambda bi, i: (0, 0)),
            pl.BlockSpec((d, LANES), lambda bi, i: (0, 0)),
        ],
        out_specs=[pl.BlockSpec((None, ROW_TILE, d), row), pl.BlockSpec((None, ROW_TILE, d), row),
                   pl.BlockSpec((None, N_EXPERTS, ROW_TILE), lambda bi, i: (bi, 0, i))],
        out_shape=[out_f32, out_f32, jax.ShapeDtypeStruct((b, N_EXPERTS, sy), F32)],
        compiler_params=_params(("parallel", "parallel"), 48),
        name="out_proj",
    )(*ys, *xs, w_bf, mod_tab, g4, wr_bf)


def _row_copy(h_hbm, stage_sc, sem, src_row, dst_row, n_rows):
    return pltpu.make_async_copy(h_hbm.at[pl.ds(src_row, n_rows), :], stage_sc.at[pl.ds(dst_row, n_rows), :], sem)


def _moe_kernel(idx_ref, h_hbm, wg_ref, wu_ref, wd_ref, gate_ref, o_ref, stage_sc, x_sc, hmid_sc, sem, *, n_up, n_down):
    e = pl.program_id(0)
    s = pl.program_id(1)
    last = (e == pl.num_programs(0) - 1) & (s == n_up + n_down - 1)
    m = x_sc.shape[0]
    rows_per_step = m // n_up

    def issue_one(expert, row):
        _row_copy(h_hbm, stage_sc, sem, idx_ref[expert, row], row, 1).start()

    @pl.when(s == 0)
    def _():
        @pl.when(e == 0)
        def _():
            def body(i, carry):
                issue_one(0, i)
                return carry

            lax.fori_loop(0, m, body, 0, unroll=8)

        _row_copy(h_hbm, stage_sc, sem, 0, 0, m).wait()
        x_sc[...] = stage_sc[...].astype(BF16)

    @pl.when(s < n_up)
    def _():
        for i in range(rows_per_step):
            issue_one(e + 1, s * rows_per_step + i)
        x = x_sc[...]
        a = _dot(x, wg_ref[...].astype(BF16))
        u = _dot(x, wu_ref[...].astype(BF16))
        hmid_sc[s] = (a * jax.nn.sigmoid(a) * u).astype(BF16)

    @pl.when(s >= n_up)
    def _():
        hmid = jnp.concatenate([hmid_sc[f] for f in range(n_up)], axis=1)
        y = _dot(hmid, wd_ref[...].astype(BF16))
        o_ref[...] = (y * gate_ref[...]).astype(o_ref.dtype)

    @pl.when(last)
    def _():
        _row_copy(h_hbm, stage_sc, sem, 0, 0, m).wait()


def _moe(idx, h, w_g, w_u, w_d, layer, gates):
    e, m = idx.shape
    d = w_g.shape[2]
    f_dim = w_g.shape[-1]
    t_up, t_down = 256, 1024
    n_up, n_down = f_dim // t_up, d // t_down
    idx = jnp.concatenate([idx, jnp.zeros((1, m), idx.dtype)], axis=0)
    up_tile = lambda ei, s, idx_ref: (layer, ei, 0, jnp.minimum(s, n_up - 1))
    down_tile = lambda s: jnp.maximum(s - n_up, 0)
    grid_spec = pltpu.PrefetchScalarGridSpec(
        num_scalar_prefetch=1,
        grid=(e, n_up + n_down),
        in_specs=[
            pl.BlockSpec(memory_space=pl.ANY),
            pl.BlockSpec((None, None, d, t_up), up_tile),
            pl.BlockSpec((None, None, d, t_up), up_tile),
            pl.BlockSpec((None, None, f_dim, t_down), lambda ei, s, idx_ref: (layer, ei, 0, down_tile(s))),
            pl.BlockSpec((None, m, 1), lambda ei, s, idx_ref: (ei, 0, 0)),
        ],
        out_specs=pl.BlockSpec((None, m, t_down), lambda ei, s, idx_ref: (ei, 0, down_tile(s))),
        scratch_shapes=[pltpu.VMEM((m, d), F32), pltpu.VMEM((m, d), BF16), pltpu.VMEM((n_up, m, t_up), BF16),
                        pltpu.SemaphoreType.DMA(())],
    )
    return pl.pallas_call(
        functools.partial(_moe_kernel, n_up=n_up, n_down=n_down),
        grid_spec=grid_spec,
        out_shape=jax.ShapeDtypeStruct((e, m, d), BF16),
        compiler_params=_params(("arbitrary", "arbitrary"), 58),
        name="moe_experts",
    )(idx, h, w_g, w_u, w_d, gates)


_BF16_ROWS = 16
_WIN = LANES + _BF16_ROWS
_WIN_HEAD = 3 * _BF16_ROWS


def _combine_kernel(starts_ref, tab_ref, y_hbm, x_ref, mod_ref, g_ref, *rest, cap_t, n_latent, has_next):
    if has_next:
        modn_ref, gn_ref, xo_ref, h_ref, buf, sem = rest
    else:
        xo_ref, buf, sem = rest
    b = pl.program_id(0)
    j = pl.program_id(1)
    nb = pl.num_programs(0)
    n_tiles = pl.num_programs(1)
    t = b * n_tiles + j
    cur = lax.rem(t, 2)

    def window_start(bb, jj, e):
        s = starts_ref[bb * N_EXPERTS + e, jj]
        return jnp.minimum(s // _BF16_ROWS * _BF16_ROWS, cap_t - _WIN)

    def needs_tail(bb, jj, e):
        return starts_ref[bb * N_EXPERTS + e, jj + 1] > window_start(bb, jj, e) + _WIN_HEAD

    def head_copy(row, e, slot, n_rows):
        return pltpu.make_async_copy(y_hbm.at[pl.ds(row, n_rows), :], buf.at[slot, pl.ds(e * _WIN, n_rows), :],
                                     sem.at[0, slot])

    def tail_copy(row, e, slot):
        return pltpu.make_async_copy(y_hbm.at[pl.ds(row + _WIN_HEAD, _WIN - _WIN_HEAD), :],
                                     buf.at[slot, pl.ds(e * _WIN + _WIN_HEAD, _WIN - _WIN_HEAD), :], sem.at[1, slot])

    def window_row(bb, jj, e):
        return pl.multiple_of((e * nb + bb) * cap_t + window_start(bb, jj, e), _BF16_ROWS)

    def fetch(bb, jj, slot):
        for e in range(N_EXPERTS):
            row = window_row(bb, jj, e)
            head_copy(row, e, slot, _WIN_HEAD).start()

            @pl.when(needs_tail(bb, jj, e))
            def _():
                tail_copy(row, e, slot).start()

    @pl.when(t == 0)
    def _():
        buf[...] = jnp.zeros(buf.shape, buf.dtype)
        fetch(0, 0, 0)

    @pl.when(t + 1 < nb * n_tiles)
    def _():
        t1 = t + 1
        fetch(t1 // n_tiles, lax.rem(t1, n_tiles), 1 - cur)

    for e in range(N_EXPERTS):
        head_copy(0, e, cur, _WIN_HEAD).wait()

        @pl.when(needs_tail(b, j, e))
        def _():
            tail_copy(window_row(b, j, e), e, cur).wait()

    tab = tab_ref[...]
    lane_f = lax.broadcasted_iota(jnp.int32, (LANES, LANES), 1).astype(F32)
    col_of = []
    for e in range(N_EXPERTS):
        slot_e = tab[:, e:e + 1]
        rel = slot_e - window_start(b, j, e).astype(F32) + float(e * _WIN)
        col_of.append(jnp.broadcast_to(jnp.where(slot_e >= 0.0, rel, -1.0), (LANES, LANES)))
    tiles = []
    for i in range(N_EXPERTS * _WIN // LANES):
        cols = lane_f + float(i * LANES)
        e_lo, e_hi = i * LANES // _WIN, (i * LANES + LANES - 1) // _WIN
        hit = col_of[e_lo] == cols
        if e_hi != e_lo:
            hit = hit | (col_of[e_hi] == cols)
        tiles.append(jnp.where(hit, 1.0, 0.0).astype(BF16))
    moe = _dot(jnp.concatenate(tiles, axis=1), buf[cur])

    is_lat = (j * LANES + lax.broadcasted_iota(jnp.int32, (LANES, 1), 0)) < n_latent

    def mod(ref, kk):
        return jnp.where(is_lat, ref[1, kk:kk + 1, :], ref[0, kk:kk + 1, :])

    x2 = x_ref[...] + mod(mod_ref, 5) * (_rms(moe) * g_ref[3:4, :])
    xo_ref[...] = x2
    if has_next:
        h_ref[...] = (_rms(x2) * gn_ref[0:1, :] * (1.0 + mod(modn_ref, 1)) + mod(modn_ref, 0)).astype(h_ref.dtype)


def _combine(starts, tab, y, x, mod_tab, g4, mod_next, g4_next, *, n_latent):
    b, sx, d = x.shape
    cap_t = y.shape[0] // (b * N_EXPERTS)
    has_next = mod_next is not None
    row = pl.BlockSpec((None, LANES, d), lambda bi, i, s: (bi, i, 0))
    mod_spec = pl.BlockSpec((None, 2, 6, d), lambda bi, i, s: (bi, 0, 0, 0))
    g_spec = pl.BlockSpec((4, d), lambda bi, i, s: (0, 0))
    in_specs = [pl.BlockSpec((None, LANES, LANES), lambda bi, i, s: (bi, i, 0)),
                pl.BlockSpec(memory_space=pl.ANY), row, mod_spec, g_spec]
    ins = [tab, y, x, mod_tab, g4]
    out_specs, out_shape = [row], [jax.ShapeDtypeStruct((b, sx, d), F32)]
    if has_next:
        in_specs += [mod_spec, g_spec]
        ins += [mod_next, g4_next]
        out_specs.append(row)
        out_shape.append(jax.ShapeDtypeStruct((b, sx, d), BF16))
    grid_spec = pltpu.PrefetchScalarGridSpec(
        num_scalar_prefetch=1, grid=(b, sx // LANES), in_specs=in_specs, out_specs=out_specs,
        scratch_shapes=[pltpu.VMEM((2, N_EXPERTS * _WIN, d), BF16), pltpu.SemaphoreType.DMA((2, 2))])
    return pl.pallas_call(
        functools.partial(_combine_kernel, cap_t=cap_t, n_latent=n_latent, has_next=has_next),
        grid_spec=grid_spec, out_shape=out_shape,
        compiler_params=_params(("arbitrary", "arbitrary"), 48), name="moe_combine",
    )(starts, *ins)


def _prefix_count(mask_tiles, tri):
    off = jnp.zeros((N_EXPERTS, 1), F32)
    out, before = [], []
    for mt in mask_tiles:
        before.append(off)
        w = _dot(jnp.where(mt, 1.0, 0.0).astype(BF16), tri) + off
        out.append(w)
        off = w[:, LANES - 1:LANES]
    return out, before


_MIN_EXP = -160.0
_EXP_STEPS = 8
_VALUE_STEPS = 40


def _route_kernel(aff_ref, tab_ref, starts_ref, slot_ref, atile_ref, *, segments):
    r_i = lax.broadcasted_iota(jnp.int32, (LANES, LANES), 0)
    c_i = lax.broadcasted_iota(jnp.int32, (LANES, LANES), 1)
    tri = jnp.where(r_i <= c_i, 1.0, 0.0).astype(BF16)
    shape = (N_EXPERTS, LANES)
    lane_e = lax.broadcasted_iota(jnp.int32, shape, 1)
    starts = jnp.zeros(shape, F32)
    base = 0
    tile0 = 0
    for t0, n, cap in segments:
        nt = n // LANES
        a_t = [aff_ref[:, t0 + j * LANES:t0 + (j + 1) * LANES] for j in range(nt)]

        def count_ge(v, a_t=a_t):
            v_b = jnp.broadcast_to(v, shape)
            cnt = functools.reduce(jnp.add, [jnp.where(t >= v_b, 1.0, 0.0) for t in a_t])
            return jnp.sum(cnt, axis=1, keepdims=True)

        e_lo = jnp.full((N_EXPERTS, 1), _MIN_EXP, F32)
        e_hi = jnp.full((N_EXPERTS, 1), 1.0, F32)
        lo = jnp.zeros((N_EXPERTS, 1), F32)
        hi = jnp.full((N_EXPERTS, 1), 2.0, F32)
        for _ in range(_EXP_STEPS):
            e_mid = jnp.floor((e_lo + e_hi) * 0.5)
            v = jnp.exp2(e_mid)
            ok = count_ge(v) >= cap
            e_lo, lo = jnp.where(ok, e_mid, e_lo), jnp.where(ok, v, lo)
            e_hi, hi = jnp.where(ok, e_hi, e_mid), jnp.where(ok, hi, v)
        for _ in range(_VALUE_STEPS):
            v = (lo + hi) * 0.5
            ok = count_ge(v) >= cap
            lo, hi = jnp.where(ok, v, lo), jnp.where(ok, hi, v)
        lo_b = jnp.broadcast_to(lo, shape)
        hi_b = jnp.broadcast_to(hi, shape)
        gt = [t >= hi_b for t in a_t]
        eq = [(t >= lo_b) & (t < hi_b) for t in a_t]
        need = jnp.broadcast_to(cap - count_ge(hi), shape)
        eq_rank, _ = _prefix_count(eq, tri)
        sel = [g | (q & (r <= need)) for g, q, r in zip(gt, eq, eq_rank)]
        pos, before = _prefix_count(sel, tri)
        for j in range(nt):
            starts = jnp.where(lane_e == tile0 + j, before[j] + float(base), starts)
            slot = jnp.where(sel[j], pos[j] + (base - 1.0), -1.0)
            slot_ref[tile0 + j] = slot
            atile_ref[tile0 + j] = a_t[j]
            padded =jnp.concatenate([slot, jnp.full((LANES - N_EXPERTS, LANES), -1.0, F32)], axis=0)
            tab_ref[t0 + j * LANES:t0 + (j + 1) * LANES, :] = padded.T
        base += cap
        tile0 += nt
    starts_ref[...] = jnp.where(lane_e == tile0, float(base), starts)


def _compact_kernel(starts_ref, slot_ref, aff_ref, idx_ref, gate_ref, acc_i, acc_g, *, segments, rows_per_sample):
    bi = pl.program_id(0)
    row0 = (bi * rows_per_sample).astype(F32)
    r_i = lax.broadcasted_iota(jnp.int32, (LANES, LANES), 0)
    lane1 = lax.broadcasted_iota(jnp.int32, (1, LANES), 1)
    base = 0
    tile0 = 0
    for t0, n, cap in segments:
        nt = n // LANES

        def expert_body(e, carry, t0=t0, nt=nt, cap=cap, base=base, tile0=tile0):
            srow = bi * N_EXPERTS + e
            n_chunks = -(-cap // LANES)

            def scan(j, bounds):
                first = starts_ref[srow, tile0 + j]
                nxt = jnp.where(j + 1 < nt, starts_ref[srow, tile0 + jnp.minimum(j + 1, nt - 1)], base + cap)
                out = []
                for c in range(n_chunks):
                    lo = base + c * LANES
                    out.append(bounds[2 * c] + (nxt <= lo).astype(jnp.int32))
                    out.append(bounds[2 * c + 1] + (first < lo + LANES).astype(jnp.int32))
                return tuple(out)

            bounds = lax.fori_loop(0, nt, scan, (jnp.int32(0),) * (2 * n_chunks), unroll=4)
            for c in range(n_chunks):
                lo = base + c * LANES
                j_lo, j_hi = bounds[2 * c], bounds[2 * c + 1]
                slot_id = (r_i + lo).astype(F32)
                acc_i[...] = jnp.zeros((LANES, LANES), F32)
                acc_g[...] = jnp.zeros((LANES, LANES), F32)

                def tile_body(j, carry2):
                    pe = jnp.broadcast_to(slot_ref[tile0 + j, pl.ds(e, 1), :], (LANES, LANES))
                    ae = jnp.broadcast_to(aff_ref[tile0 + j, pl.ds(e, 1), :], (LANES, LANES))
                    tok = jnp.broadcast_to((lane1 + (t0 + j * LANES)).astype(F32) + row0, (LANES, LANES))
                    hit = pe == slot_id
                    acc_i[...] += jnp.where(hit, tok, 0.0)
                    acc_g[...] += jnp.where(hit, ae, 0.0)
                    return carry2

                lax.fori_loop(j_lo, j_hi, tile_body, 0)
                w = min(LANES, cap - c * LANES)
                col_i = jnp.broadcast_to(jnp.sum(acc_i[...], axis=1, keepdims=True), (LANES, LANES))
                col_g = jnp.broadcast_to(jnp.sum(acc_g[...], axis=1, keepdims=True), (LANES, LANES))
                idx_ref[e, :, lo:lo + w] = col_i.T[0:1, 0:w].astype(jnp.int32)
                gate_ref[e, :, lo:lo + w] = col_g.T[0:1, 0:w]
            return carry

        lax.fori_loop(0, N_EXPERTS, expert_body, 0)
        base += cap
        tile0 += nt


def _route(aff_t, segments, rows_per_sample):
    b, e, sy = aff_t.shape
    k = sum(cap for _, _, cap in segments)
    n_tiles = sum(n // LANES for _, n, _ in segments)
    tiles_spec = pl.BlockSpec((None, n_tiles, e, LANES), lambda bi: (bi, 0, 0, 0))
    tiles_shape = jax.ShapeDtypeStruct((b, n_tiles, e, LANES), F32)
    tab, starts, slots, a_tiles = pl.pallas_call(
        functools.partial(_route_kernel, segments=segments),
        grid=(b,),
        in_specs=[pl.BlockSpec((None, e, sy), lambda bi: (bi, 0, 0))],
        out_specs=[pl.BlockSpec((None, sy, LANES), lambda bi: (bi, 0, 0)),
                   pl.BlockSpec((None, e, LANES), lambda bi: (bi, 0, 0)), tiles_spec, tiles_spec],
        out_shape=[jax.ShapeDtypeStruct((b, sy, LANES), F32), jax.ShapeDtypeStruct((b, e, LANES), F32),
                   tiles_shape, tiles_shape],
        compiler_params=_params(("parallel",), 32),
        name="route_topk",
    )(aff_t)
    starts = starts.astype(jnp.int32).reshape(b * e, LANES)
    grid_spec = pltpu.PrefetchScalarGridSpec(
        num_scalar_prefetch=1, grid=(b,),
        in_specs=[pl.BlockSpec((None, n_tiles, e, LANES), lambda bi, s: (bi, 0, 0, 0)),
                  pl.BlockSpec((None, n_tiles, e, LANES), lambda bi, s: (bi, 0, 0, 0))],
        out_specs=[pl.BlockSpec((None, e, 1, k), lambda bi, s: (bi, 0, 0, 0)),
                   pl.BlockSpec((None, e, 1, k), lambda bi, s: (bi, 0, 0, 0))],
        scratch_shapes=[pltpu.VMEM((LANES, LANES), F32), pltpu.VMEM((LANES, LANES), F32)])
    idx, gates = pl.pallas_call(
        functools.partial(_compact_kernel, segments=segments, rows_per_sample=rows_per_sample),
        grid_spec=grid_spec,
        out_shape=[jax.ShapeDtypeStruct((b, e, 1, k), jnp.int32), jax.ShapeDtypeStruct((b, e, 1, k), F32)],
        compiler_params=_params(("parallel",), 32),
        name="route_compact",
    )(starts, slots, a_tiles)
    return tab, idx, gates, starts


def _moe_block(x, hf, aff_t, segments, w_g, w_u, w_d, layer, mod_tab, g4, mod_next, g4_next, *, n_latent):
    b, sy = aff_t.shape[0], aff_t.shape[2]
    tab, idx, gates, starts = _route(aff_t, segments, sy)
    k = idx.shape[-1]
    idx_em = jnp.swapaxes(idx[:, :, 0, :], 0, 1).reshape(N_EXPERTS, b * k)
    gates_em = jnp.swapaxes(gates[:, :, 0, :], 0, 1).reshape(N_EXPERTS, b * k, 1)
    y = _moe(idx_em, hf.reshape(b * sy, hf.shape[-1]), w_g, w_u, w_d, layer, gates_em)
    y2d = y.reshape(N_EXPERTS * b * k, y.shape[-1])
    return _combine(starts, tab, y2d, x, mod_tab, g4, mod_next, g4_next, n_latent=n_latent)


_SCALE = HEAD_DIM ** -0.5 * LOG2E
_PLAN_AB = tuple([(s, None, True) for s in range(0, 8)] + [(s, "q", True) for s in range(12, 20)]
                 + [(s, None, False) for s in range(8, 10)] + [(s, "k", False) for s in range(20, 22)])
_PLAN_C = tuple([(s, None, True) for s in range(0, 16)] + [(s, None, False) for s in range(16, 32)])


def kernel(x, c, ctx, c_ctx, mod_w, mod_b, norm_g, ab_w_in, ab_w_out, ab_sink, ab_q_norm, ab_k_norm,
           dif_w_in, dif_w_out, dif_lambda, dif_subln, router_w, exp_w_gate, exp_w_up, exp_w_down):
    b = x.shape[0]
    d = D_MODEL
    cv = jnp.concatenate([c, c_ctx[None, :], jnp.zeros((8 - b - 1, d), F32)], axis=0)
    mods = _mods(cv, mod_w, mod_b).reshape(DEPTH, 8, 6, d)
    mod_tab = jnp.stack([jnp.broadcast_to(mods[:, b][:, None], (DEPTH, b, 6, d)), mods[:, :b]], axis=2)
    tables = _rope_tables(_SCALE)
    ones = jnp.ones((1, LANES), F32)
    wr = jnp.pad(router_w, ((0, 0), (0, 0), (0, LANES - N_EXPERTS))).astype(BF16)

    h0 = _norm_mod(x, ctx, norm_g[0], mod_tab[0])
    p0 = _matmul(h0.reshape(b * S_ROWS, d), ab_w_in[0], BF16).reshape(b, S_ROWS, AB_IN)
    qk0 = _prep(p0, tables, ab_q_norm[0][None, :], ab_k_norm[0][None, :], _PLAN_AB)
    sink = ab_sink[0]
    common_a = dict(q_col=0, k_col=16, v_col=10, groups=4, kv_heads=A_KV_HEADS)
    common_b = dict(q_col=8, k_col=18, v_col=22, groups=4, kv_heads=B_KV_HEADS)
    ctx_q = dict(q_row0=SEQ // CTX_LEN, n_q=1, tq=CTX_LEN, band=False)
    ya_l = _attn_oneshot(sink, qk0, p0, **common_a, q_row0=0, n_q=SEQ // 256, tq=256, band=True, use_sink=True)
    ya_c = _attn_oneshot(sink, qk0, p0, **common_a, **ctx_q, use_sink=True)
    yb_c = _attn_oneshot(sink, qk0, p0, **common_b, **ctx_q, use_sink=False)
    yb_l = _attn_flash(jnp.zeros((4, LANES), F32), ones, qk0, p0, q_col=2, k_col=18, v_col=22, streams=4,
                       shared_k=True, dv=HEAD_DIM, heads=B_KV_HEADS, tq=256, tk=2048, diff_scale=None)
    x1, hf0, aff0 = _outproj((ya_l, yb_l, ya_c, yb_c), (x, ctx), _cast_bf16(ab_w_out[0]), mod_tab[0], norm_g[0],
                             wr[0], split=True)
    cap_l = EC_FACTOR * SEQ // N_EXPERTS
    cap_c = EC_FACTOR * CTX_LEN // N_EXPERTS
    x2, h1 = _moe_block(x1, hf0, aff0, ((0, SEQ, cap_l), (SEQ, CTX_LEN, cap_c)),
                        exp_w_gate, exp_w_up, exp_w_down, 0, mod_tab[0], norm_g[0], mod_tab[1], norm_g[1],
                        n_latent=SEQ)

    p1 = _matmul(h1.reshape(b * S_ROWS, d), dif_w_in[0], BF16).reshape(b, S_ROWS, C_IN)
    qk1 = _prep(p1, tables, ones, ones, _PLAN_C)
    lambda_init = 0.8 - 0.6 * math.exp(-0.3 * 1)
    y1 = _attn_flash(dif_lambda[0], dif_subln[0][None, :], qk1, p1, q_col=0, k_col=8, v_col=16, streams=2,
                     shared_k=False, dv=C_V_DIM, heads=C_HEADS, tq=512, tk=2048,
                     diff_scale=(lambda_init, 1.0 - lambda_init))
    x3, hf1, aff1 = _outproj((y1,), (x2,), _cast_bf16(dif_w_out[0]), mod_tab[1], norm_g[1], wr[1], split=False)
    (out,) = _moe_block(x3, hf1, aff1, ((0, SEQ, cap_l),), exp_w_gate, exp_w_up, exp_w_down, 1,
                        mod_tab[1], norm_g[1], None, None, n_latent=SEQ)
    return out
```
